```python
import math
import jax, jax.numpy as jnp
from jax import lax
import numpy as np

D_MODEL = 1024
BATCH = 16
SEQ = 2048
DEPTH = 1

CHUNK = 64
SB_BLOCK = 128
SB_HEADS = 8
SB_HEAD_DIM = 64
SB_WIDTH = SB_HEADS * SB_HEAD_DIM
RET_HEADS = 8
RET_QK_DIM = 64
RET_V_DIM = 128
RET_QK_WIDTH = RET_HEADS * RET_QK_DIM
RET_V_WIDTH = RET_HEADS * RET_V_DIM
IN_COLS = 3 * SB_WIDTH + 2 * RET_QK_WIDTH + 2 * RET_V_WIDTH
N_BRANCHES = 2
ROPE_BASE = 10000.0
N_GROUPS = 4
EXPERTS_PER_GROUP = 4
N_EXPERTS = N_GROUPS * EXPERTS_PER_GROUP
TOP_K_IN_GROUP = 2
D_FF_EXPERT = 512
EPS = 1e-6

kernel_name = "hybrid_stickbreak_retention_hmoe"


def rms_norm(x, g):
    xf = x.astype(jnp.float32)
    y = xf * lax.rsqrt(jnp.mean(xf * xf, axis=-1, keepdims=True) + EPS)
    return (y * g.astype(jnp.float32)).astype(x.dtype)


def to_heads(t, n_heads):
    b, s, _ = t.shape
    return t.reshape(b, s, n_heads, -1).transpose(0, 2, 1, 3)


def from_heads(t):
    b, h, s, d = t.shape
    return t.transpose(0, 2, 1, 3).reshape(b, s, h * d)


def rotary(x, positions):
    half = x.shape[-1] // 2
    inv_freq = ROPE_BASE ** (-jnp.arange(half, dtype=jnp.float32) / half)
    ang = positions.astype(jnp.float32)[:, None] * inv_freq[None, :]
    cos, sin = jnp.cos(ang), jnp.sin(ang)
    x1, x2 = x[..., :half], x[..., half:]
    return jnp.concatenate([x1 * cos - x2 * sin, x2 * cos + x1 * sin], axis=-1)


def stick_breaking_attention(q, k, v):
    s_len, d = q.shape[2], q.shape[3]
    scale = 1.0 / math.sqrt(d)
    outs = []
    for blk in range(s_len // SB_BLOCK):
        q0 = blk * SB_BLOCK
        k_end = q0 + SB_BLOCK
        qb = q[:, :, q0:k_end]
        kb = k[:, :, :k_end]
        vb = v[:, :, :k_end]
        z = jnp.einsum('bhqd,bhkd->bhqk', qb, kb) * scale
        t_idx = q0 + jnp.arange(SB_BLOCK)[:, None]
        s_idx = jnp.arange(k_end)[None, :]
        mask = s_idx < t_idx
        log_keep = jnp.where(mask, jax.nn.log_sigmoid(-z), 0.0)
        suffix = lax.cumsum(log_keep, axis=3, reverse=True) - log_keep
        weights = jnp.where(mask, jnp.exp(jax.nn.log_sigmoid(z) + suffix), 0.0)
        outs.append(jnp.einsum('bhqk,bhkd->bhqd', weights, vb))
    return jnp.concatenate(outs, axis=2)


def retention_chunkwise(q, k, v):
    b, h, s_len, dk = q.shape
    dv = v.shape[-1]
    nc = s_len // CHUNK
    log_gamma = jnp.log(1.0 - 2.0 ** (-5.0 - jnp.arange(h, dtype=jnp.float32)))
    idx = jnp.arange(CHUNK, dtype=jnp.float32)
    dist = jnp.abs(idx[:, None] - idx[None, :])
    d_intra = jnp.exp(log_gamma[:, None, None] * dist)
    qc = q.reshape(b, h, nc, CHUNK, dk)
    kc = k.reshape(b, h, nc, CHUNK, dk)
    vc = v.reshape(b, h, nc, CHUNK, dv)
    scores = jnp.einsum('bhncd,bhnmd->bhncm', qc, kc) * d_intra[None, :, None]
    intra = jnp.einsum('bhncm,bhnme->bhnce', scores, vc)
    k_decay = jnp.exp(log_gamma[:, None] * (CHUNK - 1 - idx)[None, :])
    kv = jnp.einsum('bhnmd,bhnme->bhnde', kc * k_decay[None, :, None, :, None], vc)
    chunk_decay = jnp.exp(log_gamma * CHUNK)[None, :, None, None]

    def step(state, kv_n):
        return state * chunk_decay + kv_n, state

    init = jnp.zeros((b, h, dk, dv), jnp.float32)
    _, prev = lax.scan(step, init, jnp.moveaxis(kv, 2, 0))
    prev = jnp.moveaxis(prev, 0, 2)
    q_decay = jnp.exp(log_gamma[:, None] * (idx + 1.0)[None, :])
    cross = jnp.einsum('bhncd,bhnde->bhnce', qc * q_decay[None, :, None, :, None], prev)
    return (intra + cross).reshape(b, h, s_len, dv)


def hierarchical_moe(xn, w_group_router, b_group_router, w_expert_router, b_expert_router,
                     w_exp_gate, w_exp_up, w_exp_down):
    b, s_len, d = xn.shape
    xt = xn.reshape(b * s_len, d)
    n_tok = xt.shape[0]
    group_logits = (xt @ w_group_router + b_group_router).astype(jnp.float32)
    group_probs = jax.nn.softmax(group_logits, axis=-1)
    p_group, g_sel = lax.top_k(group_probs, 1)
    expert_logits = (xt @ w_expert_router + b_expert_router).astype(jnp.float32)
    expert_logits = expert_logits.reshape(n_tok, N_GROUPS, EXPERTS_PER_GROUP)
    in_group = jnp.take_along_axis(expert_logits, g_sel[:, :, None], axis=1)[:, 0]
    top_vals, top_idx = lax.top_k(in_group, TOP_K_IN_GROUP)
    gate = jax.nn.softmax(top_vals, axis=-1) * p_group
    expert_id = g_sel * EXPERTS_PER_GROUP + top_idx
    combine = jnp.sum(jax.nn.one_hot(expert_id, N_EXPERTS, dtype=jnp.float32) * gate[..., None], axis=1)
    combine = combine.astype(xt.dtype)
    y = jnp.zeros_like(xt)
    for e in range(N_EXPERTS):
        hidden = jax.nn.silu(xt @ w_exp_gate[e]) * (xt @ w_exp_up[e])
        y = y + combine[:, e:e + 1] * (hidden @ w_exp_down[e])
    return y.reshape(b, s_len, d)


def setup_inputs(seed: int = 0) -> dict:
    key = jax.random.key(seed)
    ks = jax.random.split(key, 20)
    f32 = jnp.float32

    def w(k, shape, fan_in):
        return jax.random.normal(k, shape, f32) * (fan_in ** -0.5)

    def gain(k, shape):
        return 1.0 + 0.02 * jax.random.normal(k, shape, f32)

    L = DEPTH
    return {
        "x": jax.random.normal(ks[0], (BATCH, SEQ, D_MODEL), f32),
        "norm_mix_g": gain(ks[1], (L, D_MODEL)),
        "w_in": w(ks[2], (L, D_MODEL, IN_COLS), D_MODEL),
        "w_gate": w(ks[3], (L, D_MODEL, N_BRANCHES * D_MODEL), D_MODEL),
        "b_gate": 0.02 * jax.random.normal(ks[4], (L, N_BRANCHES * D_MODEL), f32),
        "w_sb_out": w(ks[5], (L, SB_WIDTH, D_MODEL), SB_WIDTH),
        "w_ret_out": w(ks[6], (L, RET_V_WIDTH, D_MODEL), RET_V_WIDTH),
        "ret_norm_g": gain(ks[7], (L, RET_V_WIDTH)),
        "w_out": w(ks[8], (L, D_MODEL, D_MODEL), D_MODEL),
        "norm_ffn_g": gain(ks[9], (L, D_MODEL)),
        "w_group_router": w(ks[10], (L, D_MODEL, N_GROUPS), D_MODEL),
        "b_group_router": 0.01 * jax.random.normal(ks[11], (L, N_GROUPS), f32),
        "w_expert_router": w(ks[12], (L, D_MODEL, N_EXPERTS), D_MODEL),
        "b_expert_router": 0.01 * jax.random.normal(ks[13], (L, N_EXPERTS), f32),
        "w_exp_gate": w(ks[14], (L, N_EXPERTS, D_MODEL, D_FF_EXPERT), D_MODEL),
        "w_exp_up": w(ks[15], (L, N_EXPERTS, D_MODEL, D_FF_EXPERT), D_MODEL),
        "w_exp_down": w(ks[16], (L, N_EXPERTS, D_FF_EXPERT, D_MODEL), D_FF_EXPERT),
        "norm_final_g": gain(ks[17], (D_MODEL,)),
    }


def reference(x, norm_mix_g, w_in, w_gate, b_gate, w_sb_out, w_ret_out, ret_norm_g, w_out,
              norm_ffn_g, w_group_router, b_group_router, w_expert_router, b_expert_router,
              w_exp_gate, w_exp_up, w_exp_down, norm_final_g):
    h = x
    s_len = x.shape[1]
    positions = jnp.arange(s_len, dtype=jnp.int32)
    split_sizes = [SB_WIDTH, SB_WIDTH, SB_WIDTH, RET_QK_WIDTH, RET_QK_WIDTH, RET_V_WIDTH, RET_V_WIDTH]
    split_at = [int(i) for i in np.cumsum(split_sizes)[:-1]]
    for layer in range(DEPTH):
        xn = rms_norm(h, norm_mix_g[layer])
        proj = xn @ w_in[layer]
        sb_q, sb_k, sb_v, r_q, r_k, r_v, r_g = jnp.split(proj, split_at, axis=-1)
        y_sb = stick_breaking_attention(to_heads(sb_q, SB_HEADS).astype(jnp.float32),
                                        to_heads(sb_k, SB_HEADS).astype(jnp.float32),
                                        to_heads(sb_v, SB_HEADS).astype(jnp.float32))
        y_sb = from_heads(y_sb).astype(x.dtype) @ w_sb_out[layer]
        rq = rotary(to_heads(r_q, RET_HEADS).astype(jnp.float32), positions)
        rk = rotary(to_heads(r_k, RET_HEADS).astype(jnp.float32), positions) * (RET_QK_DIM ** -0.5)
        rv = to_heads(r_v, RET_HEADS).astype(jnp.float32)
        y_ret = retention_chunkwise(rq, rk, rv)
        y_ret = y_ret * lax.rsqrt(jnp.mean(y_ret * y_ret, axis=-1, keepdims=True) + EPS)
        y_ret = from_heads(y_ret) * ret_norm_g[layer].astype(jnp.float32)
        y_ret = (jax.nn.silu(r_g.astype(jnp.float32)) * y_ret).astype(x.dtype) @ w_ret_out[layer]
        gates = jax.nn.sigmoid(xn @ w_gate[layer] + b_gate[layer])
        g_sb, g_ret = jnp.split(gates, N_BRANCHES, axis=-1)
        h = h + (g_sb * y_sb + g_ret * y_ret) @ w_out[layer]
        hn = rms_norm(h, norm_ffn_g[layer])
        h = h + hierarchical_moe(hn, w_group_router[layer], b_group_router[layer],
                                 w_expert_router[layer], b_expert_router[layer],
                                 w_exp_gate[layer], w_exp_up[layer], w_exp_down[layer])
    return rms_norm(h, norm_final_g)
```

```python
import functools
import math

import numpy as np
import jax
import jax.numpy as jnp
from jax import lax
from jax.experimental import pallas as pl
from jax.experimental.pallas import tpu as pltpu

F32 = jnp.float32
BF16 = jnp.bfloat16

D_MODEL = 1024
SB_HEADS = 8
SB_HEAD_DIM = 64
SB_WIDTH = SB_HEADS * SB_HEAD_DIM
RET_HEADS = 8
RET_QK_DIM = 64
RET_V_DIM = 128
RET_QK_WIDTH = RET_HEADS * RET_QK_DIM
RET_V_WIDTH = RET_HEADS * RET_V_DIM
IN_COLS = 3 * SB_WIDTH + 2 * RET_QK_WIDTH + 2 * RET_V_WIDTH
GATE_COLS = 2 * D_MODEL
ALL_COLS = IN_COLS + GATE_COLS
CHUNK = 64
ROPE_BASE = 10000.0
N_GROUPS = 4
EXPERTS_PER_GROUP = 4
N_EXPERTS = N_GROUPS * EXPERTS_PER_GROUP
D_FF = 512
EPS = 1e-6

LANES = 128
PROJ_TN = 512
SB_BLK = 256
RET_BLK = 256
MIX_TM = 512
MOE_TM = 256
ROUTER_ROWS = 32
PAIRS_PER_GROUP = 6
N_CLASSES = N_GROUPS * PAIRS_PER_GROUP
INFO_COLS = LANES
HEXT_COLS = D_MODEL + INFO_COLS
VMEM_LIMIT = 56 * 1024 * 1024

_PAIR_A = (0, 0, 0, 1, 1, 2)
_PAIR_B = (1, 2, 3, 2, 3, 3)


def _proj_kernel(x_ref, g_ref, w_ref, b_ref, cos_ref, sin_ref, o_ref, xn_ref):
    j = pl.program_id(1)

    @pl.when(j == 0)
    def _():
        x = x_ref[...]
        ms = jnp.mean(x * x, axis=-1, keepdims=True)
        xn_ref[...] = ((x * lax.rsqrt(ms + EPS)) * g_ref[...]).astype(BF16)

    acc = jnp.dot(xn_ref[...], w_ref[...], preferred_element_type=F32)

    def rotary(a, scale):
        lane = lax.broadcasted_iota(jnp.int32, (1, LANES), 1)
        first = (lane % RET_QK_DIM) < (RET_QK_DIM // 2)
        cos = cos_ref[...]
        sin = sin_ref[...]
        for p in range(PROJ_TN // LANES):
            seg = a[:, p * LANES:(p + 1) * LANES]
            swapped = jnp.where(first, pltpu.roll(seg, LANES - RET_QK_DIM // 2, 1),
                                pltpu.roll(seg, RET_QK_DIM // 2, 1))
            r = seg * cos + swapped * sin
            if scale != 1.0:
                r = r * scale
            o_ref[:, p * LANES:(p + 1) * LANES] = r.astype(BF16)

    @pl.when(j == 0)
    def _():
        o_ref[...] = (acc * (SB_HEAD_DIM ** -0.5)).astype(BF16)

    @pl.when((j == 1) | (j == 2) | (j == 5) | (j == 6))
    def _():
        o_ref[...] = acc.astype(BF16)

    @pl.when(j == 3)
    def _():
        rotary(acc, 1.0)

    @pl.when(j == 4)
    def _():
        rotary(acc, RET_QK_DIM ** -0.5)

    @pl.when((j == 7) | (j == 8))
    def _():
        o_ref[...] = (acc * jax.nn.sigmoid(acc)).astype(BF16)

    @pl.when(j >= IN_COLS // PROJ_TN)
    def _():
        o_ref[...] = jax.nn.sigmoid(acc + b_ref[...]).astype(BF16)


def _proj_call(x2, g, w_all, b_gate, cos_t, sin_t, seq):
    t = x2.shape[0]
    n_in = IN_COLS // PROJ_TN
    return pl.pallas_call(
        _proj_kernel,
        grid=(t // seq, ALL_COLS // PROJ_TN),
        in_specs=[
            pl.BlockSpec((seq, D_MODEL), lambda i, j: (i, 0)),
            pl.BlockSpec((1, D_MODEL), lambda i, j: (0, 0)),
            pl.BlockSpec((D_MODEL, PROJ_TN), lambda i, j: (0, j)),
            pl.BlockSpec((1, PROJ_TN), lambda i, j: (0, jnp.maximum(j - n_in, 0))),
            pl.BlockSpec((seq, LANES), lambda i, j: (0, 0)),
            pl.BlockSpec((seq, LANES), lambda i, j: (0, 0)),
        ],
        out_specs=pl.BlockSpec((seq, PROJ_TN), lambda i, j: (i, j)),
        out_shape=jax.ShapeDtypeStruct((t, ALL_COLS), BF16),
        scratch_shapes=[pltpu.VMEM((seq, D_MODEL), BF16)],
        compiler_params=pltpu.CompilerParams(
            dimension_semantics=("arbitrary", "arbitrary"), vmem_limit_bytes=VMEM_LIMIT),
        name="proj",
    )(x2, g, w_all, b_gate, cos_t, sin_t)


def _sb_kernel(q_ref, k_ref, v_ref, o_ref, u_ref, acc_ref, car_ref):
    seq = q_ref.shape[0]
    nq = seq // SB_BLK
    row = lax.broadcasted_iota(jnp.int32, (SB_BLK, SB_BLK), 0)
    col = lax.broadcasted_iota(jnp.int32, (SB_BLK, SB_BLK), 1)
    u_ref[...] = (row >= col).astype(BF16)
    lane = lax.broadcasted_iota(jnp.int32, (1, LANES), 1)
    head_masks = (lane < SB_HEAD_DIM, lane >= SB_HEAD_DIM)

    def block(q_heads, kb, diagonal):
        k_blk = k_ref[pl.ds(pl.multiple_of(kb * SB_BLK, SB_BLK), SB_BLK), :]
        v_blk = v_ref[pl.ds(pl.multiple_of(kb * SB_BLK, SB_BLK), SB_BLK), :]
        u = u_ref[...]
        causal = col < row
        for h in range(2):
            z = lax.dot_general(q_heads[h], k_blk, (((1,), (1,)), ((), ())),
                                preferred_element_type=F32)
            lk = jnp.minimum(-z, 0.0) - jnp.log1p(jnp.exp(-jnp.abs(z)))
            if diagonal:
                lk = jnp.where(causal, lk, 0.0)
            hi = lk.astype(BF16)
            lo = (lk - hi.astype(F32)).astype(BF16)
            sinc = (jnp.dot(hi, u, preferred_element_type=F32)
                    + jnp.dot(lo, u, preferred_element_type=F32))
            car = car_ref[h]
            w = jnp.exp(z + sinc + jnp.concatenate([car, car], axis=1))
            if diagonal:
                w = jnp.where(causal, w, 0.0)
            vh = jnp.where(head_masks[h], v_blk, jnp.zeros_like(v_blk))
            acc_ref[...] += jnp.dot(w.astype(BF16), vh, preferred_element_type=F32)
            car_ref[h] = car + jnp.broadcast_to(sinc[:, 0:1], (SB_BLK, LANES))

    def q_block(qi, carry):
        q_blk = q_ref[pl.ds(pl.multiple_of(qi * SB_BLK, SB_BLK), SB_BLK), :]
        q_heads = tuple(jnp.where(m, q_blk, jnp.zeros_like(q_blk)) for m in head_masks)
        acc_ref[...] = jnp.zeros_like(acc_ref)
        car_ref[...] = jnp.zeros_like(car_ref)
        block(q_heads, qi, True)

        def k_step(it, c):
            block(q_heads, qi - 1 - it, False)
            return c

        lax.fori_loop(0, qi, k_step, 0)
        o_ref[pl.ds(pl.multiple_of(qi * SB_BLK, SB_BLK), SB_BLK), :] = acc_ref[...].astype(BF16)
        return carry

    lax.fori_loop(0, nq, q_block, 0)


def _sb_call(proj, seq):
    t = proj.shape[0]
    n_pairs = SB_WIDTH // LANES
    return pl.pallas_call(
        _sb_kernel,
        grid=(t // seq, n_pairs),
        in_specs=[
            pl.BlockSpec((seq, LANES), lambda b, p: (b, p)),
            pl.BlockSpec((seq, LANES), lambda b, p: (b, n_pairs + p)),
            pl.BlockSpec((seq, LANES), lambda b, p: (b, 2 * n_pairs + p)),
        ],
        out_specs=pl.BlockSpec((seq, LANES), lambda b, p: (b, p)),
        out_shape=jax.ShapeDtypeStruct((t, SB_WIDTH), BF16),
        scratch_shapes=[
            pltpu.VMEM((SB_BLK, SB_BLK), BF16),
            pltpu.VMEM((SB_BLK, LANES), F32),
            pltpu.VMEM((2, SB_BLK, LANES), F32),
        ],
        compiler_params=pltpu.CompilerParams(
            dimension_semantics=("arbitrary", "arbitrary"), vmem_limit_bytes=VMEM_LIMIT),
        name="stickbreak",
    )(proj, proj, proj)


def _ret_tables():
    h = np.arange(RET_HEADS, dtype=np.float64)
    log_gamma = np.log(1.0 - 2.0 ** (-5.0 - h))
    idx = np.arange(RET_BLK, dtype=np.float64)
    t, s = idx[:, None], idx[None, :]
    same = (t // CHUNK) == (s // CHUNK)
    earlier = (s // CHUNK) < (t // CHUNK)
    expo = np.where(same, np.abs(t - s), np.where(earlier, t - s, 0.0))
    dmat = np.exp(log_gamma[:, None, None] * expo) * (same | earlier)
    qdec = np.exp(log_gamma[:, None] * (idx + 1.0)[None, :])
    kdec = np.exp(log_gamma[:, None] * (RET_BLK - 1.0 - idx)[None, :])
    cdec = np.exp(log_gamma * RET_BLK)
    rep = lambda a: np.broadcast_to(a[..., None], a.shape + (LANES,))
    shp = (RET_HEADS // 2, 2)
    return (jnp.asarray(dmat.reshape(shp + (RET_BLK, RET_BLK)), F32),
            jnp.asarray(rep(qdec).reshape(shp + (RET_BLK, LANES)), F32),
            jnp.asarray(rep(kdec).reshape(shp + (RET_BLK, LANES)), F32),
            jnp.asarray(rep(cdec[:, None]).reshape(shp + (1, LANES)), F32))


def _ret_kernel(q_ref, k_ref, v_ref, g_ref, gn_ref, dm_ref, qd_ref, kd_ref, cd_ref, o_ref, st_ref):
    seq = q_ref.shape[0]
    lane = lax.broadcasted_iota(jnp.int32, (1, LANES), 1)
    head_masks = (lane < RET_QK_DIM, lane >= RET_QK_DIM)
    st_ref[...] = jnp.zeros_like(st_ref)

    def sblock(n, carry):
        rows = pl.ds(pl.multiple_of(n * RET_BLK, RET_BLK), RET_BLK)
        q_blk = q_ref[rows, :]
        k_blk = k_ref[rows, :]
        for h in range(2):
            cols = slice(h * RET_V_DIM, (h + 1) * RET_V_DIM)
            qm = jnp.where(head_masks[h], q_blk, jnp.zeros_like(q_blk))
            km = jnp.where(head_masks[h], k_blk, jnp.zeros_like(k_blk))
            v_h = v_ref[rows, cols]
            scores = lax.dot_general(qm, k_blk, (((1,), (1,)), ((), ())),
                                     preferred_element_type=F32)
            p = (scores * dm_ref[0, h]).astype(BF16)
            y = jnp.dot(p, v_h, preferred_element_type=F32)
            state = st_ref[h]
            qd = (qm.astype(F32) * qd_ref[0, h]).astype(BF16)
            y = y + jnp.dot(qd, state.astype(BF16), preferred_element_type=F32)
            kd = (km.astype(F32) * kd_ref[0, h]).astype(BF16)
            kv = lax.dot_general(kd, v_h, (((0,), (0,)), ((), ())),
                                 preferred_element_type=F32)
            st_ref[h] = state * cd_ref[0, h] + kv
            ms = jnp.mean(y * y, axis=-1, keepdims=True)
            yn = (y * lax.rsqrt(ms + EPS)) * gn_ref[:, cols]
            o_ref[rows, cols] = (g_ref[rows, cols].astype(F32) * yn).astype(BF16)
        return carry

    lax.fori_loop(0, seq // RET_BLK, sblock, 0)


def _ret_call(proj, ret_norm_g, seq):
    t = proj.shape[0]
    n_pairs = RET_HEADS // 2
    dmat, qdec, kdec, cdec = _ret_tables()
    q0 = (3 * SB_WIDTH) // LANES
    k0 = q0 + RET_QK_WIDTH // LANES
    v0 = (3 * SB_WIDTH + 2 * RET_QK_WIDTH) // (2 * RET_V_DIM)
    g0 = v0 + RET_V_WIDTH // (2 * RET_V_DIM)
    return pl.pallas_call(
        _ret_kernel,
        grid=(t // seq, n_pairs),
        in_specs=[
            pl.BlockSpec((seq, LANES), lambda b, p: (b, q0 + p)),
            pl.BlockSpec((seq, LANES), lambda b, p: (b, k0 + p)),
            pl.BlockSpec((seq, 2 * RET_V_DIM), lambda b, p: (b, v0 + p)),
            pl.BlockSpec((seq, 2 * RET_V_DIM), lambda b, p: (b, g0 + p)),
            pl.BlockSpec((1, 2 * RET_V_DIM), lambda b, p: (0, p)),
            pl.BlockSpec((1, 2, RET_BLK, RET_BLK), lambda b, p: (p, 0, 0, 0)),
            pl.BlockSpec((1, 2, RET_BLK, LANES), lambda b, p: (p, 0, 0, 0)),
            pl.BlockSpec((1, 2, RET_BLK, LANES), lambda b, p: (p, 0, 0, 0)),
            pl.BlockSpec((1, 2, 1, LANES), lambda b, p: (p, 0, 0, 0)),
        ],
        out_specs=pl.BlockSpec((seq, 2 * RET_V_DIM), lambda b, p: (b, p)),
        out_shape=jax.ShapeDtypeStruct((t, RET_V_WIDTH), BF16),
        scratch_shapes=[pltpu.VMEM((2, LANES, RET_V_DIM), F32)],
        compiler_params=pltpu.CompilerParams(
            dimension_semantics=("arbitrary", "arbitrary"), vmem_limit_bytes=VMEM_LIMIT),
        name="retention",
    )(proj, proj, proj, proj, ret_norm_g, dmat, qdec, kdec, cdec)


def _mix_kernel(ysb_ref, yret_ref, gs0_ref, gs1_ref, gr0_ref, gr1_ref, x_ref,
                wsb_ref, wret_ref, wout_ref, gffn_ref, wrh_ref, wrl_ref, br_ref,
                hext_ref, cls_ref):
    tm = x_ref.shape[0]
    a = jnp.dot(ysb_ref[...], wsb_ref[...], preferred_element_type=F32)
    b = jnp.dot(yret_ref[...], wret_ref[...], preferred_element_type=F32)
    g_sb = jnp.concatenate([gs0_ref[...], gs1_ref[...]], axis=1).astype(F32)
    g_ret = jnp.concatenate([gr0_ref[...], gr1_ref[...]], axis=1).astype(F32)
    mixed = (g_sb * a + g_ret * b).astype(BF16)
    h = x_ref[...] + jnp.dot(mixed, wout_ref[...], preferred_element_type=F32)
    hext_ref[:, 0:D_MODEL] = h

    ms = jnp.mean(h * h, axis=-1, keepdims=True)
    hn = (h * lax.rsqrt(ms + EPS)) * gffn_ref[...]
    hn_hi = hn.astype(BF16)
    hn_lo = (hn - hn_hi.astype(F32)).astype(BF16)
    nt = (((1,), (1,)), ((), ()))
    logits = (lax.dot_general(wrh_ref[...], hn_hi, nt, preferred_element_type=F32)
              + lax.dot_general(wrh_ref[...], hn_lo, nt, preferred_element_type=F32)
              + lax.dot_general(wrl_ref[...], hn_hi, nt, preferred_element_type=F32)
              + br_ref[...])

    def first_argmax(vals):
        m = functools.reduce(jnp.maximum, vals)
        idx = jnp.full(m.shape, len(vals) - 1, jnp.int32)
        for i in range(len(vals) - 2, -1, -1):
            idx = jnp.where(vals[i] >= m, i, idx)
        return m, idx

    gl = [logits[r:r + 1, :] for r in range(N_GROUPS)]
    gmax, gsel = first_argmax(gl)
    p_group = 1.0 / functools.reduce(lambda s, v: s + v, [jnp.exp(v - gmax) for v in gl])
    el = []
    for e in range(EXPERTS_PER_GROUP):
        v = logits[N_GROUPS + 3 * EXPERTS_PER_GROUP + e:N_GROUPS + 3 * EXPERTS_PER_GROUP + e + 1, :]
        for g in range(N_GROUPS - 2, -1, -1):
            r = N_GROUPS + g * EXPERTS_PER_GROUP + e
            v = jnp.where(gsel == g, logits[r:r + 1, :], v)
        el.append(v)
    m1, i1 = first_argmax(el)
    rest = [jnp.where(i1 == e, -jnp.inf, el[e]) for e in range(EXPERTS_PER_GROUP)]
    m2, i2 = first_argmax(rest)
    tt = jnp.exp(m2 - m1)
    gate1 = p_group * (1.0 / (1.0 + tt))
    gate2 = p_group * (tt / (1.0 + tt))
    lo_first = i1 < i2
    ea = jnp.minimum(i1, i2)
    eb = jnp.maximum(i1, i2)
    w_a = jnp.where(lo_first, gate1, gate2)
    w_b = jnp.where(lo_first, gate2, gate1)
    pid = jnp.where(ea == 0, eb - 1, jnp.where(ea == 1, eb + 1, PAIRS_PER_GROUP - 1))
    cls = gsel * PAIRS_PER_GROUP + pid
    cls_ref[...] = jnp.broadcast_to(cls, cls_ref.shape)
    info = jnp.concatenate([w_a, w_b, jnp.zeros((INFO_COLS - 2, tm), F32)], axis=0)
    hext_ref[:, D_MODEL:HEXT_COLS] = info.T


def _mix_call(ysb, yret, proj, x2, wsb, wret, wout, gffn, wr_hi, wr_lo, br):
    t = x2.shape[0]
    tm = MIX_TM
    gate0 = IN_COLS // PROJ_TN
    const = lambda i: (0, 0)
    return pl.pallas_call(
        _mix_kernel,
        grid=(t // tm,),
        in_specs=[
            pl.BlockSpec((tm, SB_WIDTH), lambda i: (i, 0)),
            pl.BlockSpec((tm, RET_V_WIDTH), lambda i: (i, 0)),
            pl.BlockSpec((tm, PROJ_TN), lambda i: (i, gate0)),
            pl.BlockSpec((tm, PROJ_TN), lambda i: (i, gate0 + 1)),
            pl.BlockSpec((tm, PROJ_TN), lambda i: (i, gate0 + 2)),
            pl.BlockSpec((tm, PROJ_TN), lambda i: (i, gate0 + 3)),
            pl.BlockSpec((tm, D_MODEL), lambda i: (i, 0)),
            pl.BlockSpec((SB_WIDTH, D_MODEL), const),
            pl.BlockSpec((RET_V_WIDTH, D_MODEL), const),
            pl.BlockSpec((D_MODEL, D_MODEL), const),
            pl.BlockSpec((1, D_MODEL), const),
            pl.BlockSpec((ROUTER_ROWS, D_MODEL), const),
            pl.BlockSpec((ROUTER_ROWS, D_MODEL), const),
            pl.BlockSpec((ROUTER_ROWS, 1), const),
        ],
        out_specs=[
            pl.BlockSpec((tm, HEXT_COLS), lambda i: (i, 0)),
            pl.BlockSpec((8, tm), lambda i: (0, i)),
        ],
        out_shape=[
            jax.ShapeDtypeStruct((t, HEXT_COLS), F32),
            jax.ShapeDtypeStruct((8, t), jnp.int32),
        ],
        compiler_params=pltpu.CompilerParams(
            dimension_semantics=("arbitrary",), vmem_limit_bytes=VMEM_LIMIT),
        name="mix_router",
    )(ysb, yret, proj, proj, proj, proj, x2, wsb, wret, wout, gffn, wr_hi, wr_lo, br)


def _moe_kernel(src_ref, nvalid_ref, ea_ref, eb_ref,
                hext_ref, wga_ref, wua_ref, wda_ref, wgb_ref, wub_ref, wdb_ref,
                gffn_ref, gfin_ref, out_ref, hbuf, obuf, gsem, ssem):
    i = pl.program_id(0)
    tm = hbuf.shape[0]
    nv = nvalid_ref[i]
    base = i * tm

    @pl.when(i == 0)
    def _():
        hbuf[...] = jnp.zeros_like(hbuf)

    def gather_copy(r):
        tok = src_ref[base + r]
        return pltpu.make_async_copy(hext_ref.at[pl.ds(tok, 1), :], hbuf.at[pl.ds(r, 1), :], gsem)

    def scatter_copy(r):
        tok = src_ref[base + r]
        return pltpu.make_async_copy(obuf.at[pl.ds(r, 1), :], out_ref.at[pl.ds(tok, 1), :], ssem)

    def for_rows(fn):
        def body(r, c):
            fn(r)
            return c
        lax.fori_loop(0, nv, body, 0)

    @pl.when(nv > 0)
    def _():
        for_rows(lambda r: gather_copy(r).start())
        for_rows(lambda r: gather_copy(r).wait())
        h = hbuf[:, 0:D_MODEL]
        w_a = hbuf[:, D_MODEL:D_MODEL + 1]
        w_b = hbuf[:, D_MODEL + 1:D_MODEL + 2]
        ms = jnp.mean(h * h, axis=-1, keepdims=True)
        hn = ((h * lax.rsqrt(ms + EPS)) * gffn_ref[...]).astype(BF16)

        def expert(wg_ref, wu_ref, wd_ref):
            gate = jnp.dot(hn, wg_ref[0], preferred_element_type=F32)
            up = jnp.dot(hn, wu_ref[0], preferred_element_type=F32)
            hidden = ((gate * jax.nn.sigmoid(gate)) * up).astype(BF16)
            return jnp.dot(hidden, wd_ref[0], preferred_element_type=F32)

        y = w_a * expert(wga_ref, wua_ref, wda_ref) + w_b * expert(wgb_ref, wub_ref, wdb_ref)
        h2 = h + y
        ms2 = jnp.mean(h2 * h2, axis=-1, keepdims=True)
        obuf[...] = (h2 * lax.rsqrt(ms2 + EPS)) * gfin_ref[...]
        for_rows(lambda r: scatter_copy(r).start())
        for_rows(lambda r: scatter_copy(r).wait())


def _moe_call(src, nvalid, ea, eb, hext, wg, wu, wd, gffn, gfin):
    t = hext.shape[0]
    tm = MOE_TM
    n_tiles = nvalid.shape[0]
    wa_map = lambda i, src, nv, ea, eb: (ea[i], 0, 0)
    wb_map = lambda i, src, nv, ea, eb: (eb[i], 0, 0)
    const = lambda i, src, nv, ea, eb: (0, 0)
    return pl.pallas_call(
        _moe_kernel,
        grid_spec=pltpu.PrefetchScalarGridSpec(
            num_scalar_prefetch=4,
            grid=(n_tiles,),
            in_specs=[
                pl.BlockSpec(memory_space=pl.ANY),
                pl.BlockSpec((1, D_MODEL, D_FF), wa_map),
                pl.BlockSpec((1, D_MODEL, D_FF), wa_map),
                pl.BlockSpec((1, D_FF, D_MODEL), wa_map),
                pl.BlockSpec((1, D_MODEL, D_FF), wb_map),
                pl.BlockSpec((1, D_MODEL, D_FF), wb_map),
                pl.BlockSpec((1, D_FF, D_MODEL), wb_map),
                pl.BlockSpec((1, D_MODEL), const),
                pl.BlockSpec((1, D_MODEL), const),
            ],
            out_specs=pl.BlockSpec(memory_space=pl.ANY),
            scratch_shapes=[
                pltpu.VMEM((tm, HEXT_COLS), F32),
                pltpu.VMEM((tm, D_MODEL), F32),
                pltpu.SemaphoreType.DMA(()),
                pltpu.SemaphoreType.DMA(()),
            ],
        ),
        out_shape=jax.ShapeDtypeStruct((t, D_MODEL), F32),
        compiler_params=pltpu.CompilerParams(
            dimension_semantics=("arbitrary",), vmem_limit_bytes=VMEM_LIMIT),
        name="experts",
    )(src, nvalid, ea, eb, hext, wg, wu, wd, wg, wu, wd, gffn, gfin)


def _routing_plan(cls, tm):
    t = cls.shape[0]
    n_tiles = t // tm + N_CLASSES
    onehot = (cls[:, None] == jnp.arange(N_CLASSES, dtype=jnp.int32)[None, :]).astype(jnp.int32)
    counts = jnp.sum(onehot, axis=0)
    rank = jnp.sum((jnp.cumsum(onehot, axis=0) - onehot) * onehot, axis=1)
    tiles_c = (counts + tm - 1) // tm
    tile_end = jnp.cumsum(tiles_c)
    tile_off = tile_end - tiles_c
    dest = tile_off[cls] * tm + rank
    src = jnp.zeros((n_tiles * tm,), jnp.int32).at[dest].set(jnp.arange(t, dtype=jnp.int32))
    tile = jnp.arange(n_tiles, dtype=jnp.int32)
    used = tile < tile_end[-1]
    tile_cls = jnp.minimum(jnp.searchsorted(tile_end, tile, side="right"), N_CLASSES - 1).astype(jnp.int32)
    last_cls = tile_cls[jnp.maximum(tile_end[-1] - 1, 0)]
    tile_cls = jnp.where(used, tile_cls, last_cls)
    nvalid = jnp.where(used, jnp.clip(counts[tile_cls] - (tile - tile_off[tile_cls]) * tm, 0, tm), 0)
    grp = tile_cls // PAIRS_PER_GROUP
    pid = tile_cls % PAIRS_PER_GROUP
    ea = grp * EXPERTS_PER_GROUP + jnp.asarray(_PAIR_A, jnp.int32)[pid]
    eb = grp * EXPERTS_PER_GROUP + jnp.asarray(_PAIR_B, jnp.int32)[pid]
    return src, nvalid.astype(jnp.int32), ea.astype(jnp.int32), eb.astype(jnp.int32)


def _rope_tables(seq):
    half = RET_QK_DIM // 2
    inv_freq = ROPE_BASE ** (-jnp.arange(half, dtype=F32) / half)
    ang = jnp.arange(seq, dtype=F32)[:, None] * inv_freq[None, :]
    cos, sin = jnp.cos(ang), jnp.sin(ang)
    reps = LANES // RET_QK_DIM
    cos_t = jnp.tile(jnp.concatenate([cos, cos], axis=1), (1, reps))
    sin_t = jnp.tile(jnp.concatenate([-sin, sin], axis=1), (1, reps))
    return cos_t, sin_t


def kernel(x, norm_mix_g, w_in, w_gate, b_gate, w_sb_out, w_ret_out, ret_norm_g, w_out,
           norm_ffn_g, w_group_router, b_group_router, w_expert_router, b_expert_router,
           w_exp_gate, w_exp_up, w_exp_down, norm_final_g):
    bsz, seq, d = x.shape
    assert d == D_MODEL and w_in.shape[0] == 1 and seq % SB_BLK == 0 and (bsz * seq) % MIX_TM == 0
    t = bsz * seq
    x2 = x.reshape(t, d)
    w_all = jnp.concatenate([w_in[0], w_gate[0]], axis=1).astype(BF16)
    cos_t, sin_t = _rope_tables(seq)
    proj = _proj_call(x2, norm_mix_g[0][None, :], w_all, b_gate[0][None, :], cos_t, sin_t, seq)
    ysb = _sb_call(proj, seq)
    yret = _ret_call(proj, ret_norm_g[0][None, :], seq)

    wr = jnp.concatenate([w_group_router[0], w_expert_router[0]], axis=1).T
    wr = jnp.pad(wr, ((0, ROUTER_ROWS - wr.shape[0]), (0, 0)))
    wr_hi = wr.astype(BF16)
    wr_lo = (wr - wr_hi.astype(F32)).astype(BF16)
    br = jnp.concatenate([b_group_router[0], b_expert_router[0]])
    br = jnp.pad(br, (0, ROUTER_ROWS - br.shape[0]))[:, None]
    hext, cls8 = _mix_call(ysb, yret, proj, x2, w_sb_out[0].astype(BF16), w_ret_out[0].astype(BF16),
                           w_out[0].astype(BF16), norm_ffn_g[0][None, :], wr_hi, wr_lo, br)

    src, nvalid, ea, eb = _routing_plan(cls8[0], MOE_TM)
    out = _moe_call(src, nvalid, ea, eb, hext, w_exp_gate[0].astype(BF16), w_exp_up[0].astype(BF16),
                    w_exp_down[0].astype(BF16), norm_ffn_g[0][None, :], norm_final_g[None, :])
    return out.reshape(bsz, seq, d)
```

```python
import functools
import math

import numpy as np
import jax
import jax.numpy as jnp
from jax import lax
from jax.experimental import pallas as pl
from jax.experimental.pallas import tpu as pltpu

F32 = jnp.float32
BF16 = jnp.bfloat16

D_MODEL = 1024
SB_HEADS = 8
SB_HEAD_DIM = 64
SB_WIDTH = SB_HEADS * SB_HEAD_DIM
RET_HEADS = 8
RET_QK_DIM = 64
RET_V_DIM = 128
RET_QK_WIDTH = RET_HEADS * RET_QK_DIM
RET_V_WIDTH = RET_HEADS * RET_V_DIM
IN_COLS = 3 * SB_WIDTH + 2 * RET_QK_WIDTH + 2 * RET_V_WIDTH
GATE_COLS = 2 * D_MODEL
ALL_COLS = IN_COLS + GATE_COLS
CHUNK = 64
ROPE_BASE = 10000.0
N_GROUPS = 4
EXPERTS_PER_GROUP = 4
N_EXPERTS = N_GROUPS * EXPERTS_PER_GROUP
D_FF = 512
EPS = 1e-6

LANES = 128
PROJ_TN = 512
SB_BLK = 256
RET_BLK = 256
MIX_TM = 512
MOE_TM = 256
ROUTER_ROWS = 32
PAIRS_PER_GROUP = 6
N_CLASSES = N_GROUPS * PAIRS_PER_GROUP
INFO_COLS = LANES
HEXT_COLS = D_MODEL + INFO_COLS
VMEM_LIMIT = 56 * 1024 * 1024
LOG2E = math.log2(math.e)
SB_SKIP_LOG2 = 153.0

_PAIR_A = (0, 0, 0, 1, 1, 2)
_PAIR_B = (1, 2, 3, 2, 3, 3)


def _proj_kernel(x_ref, g_ref, w_ref, b_ref, cos_ref, sin_ref, o_ref, xn_ref):
    j = pl.program_id(1)

    @pl.when(j == 0)
    def _():
        x = x_ref[...]
        ms = jnp.mean(x * x, axis=-1, keepdims=True)
        xn_ref[...] = ((x * lax.rsqrt(ms + EPS)) * g_ref[...]).astype(BF16)

    acc = jnp.dot(xn_ref[...], w_ref[...], preferred_element_type=F32)

    def rotary(a, scale):
        lane = lax.broadcasted_iota(jnp.int32, (1, LANES), 1)
        first = (lane % RET_QK_DIM) < (RET_QK_DIM // 2)
        cos = cos_ref[...]
        sin = sin_ref[...]
        for p in range(PROJ_TN // LANES):
            seg = a[:, p * LANES:(p + 1) * LANES]
            swapped = jnp.where(first, pltpu.roll(seg, LANES - RET_QK_DIM // 2, 1),
                                pltpu.roll(seg, RET_QK_DIM // 2, 1))
            r = seg * cos + swapped * sin
            if scale != 1.0:
                r = r * scale
            o_ref[:, p * LANES:(p + 1) * LANES] = r.astype(BF16)

    @pl.when(j == 0)
    def _():
        o_ref[...] = (acc * (SB_HEAD_DIM ** -0.5)).astype(BF16)

    @pl.when((j == 1) | (j == 2) | (j == 5) | (j == 6))
    def _():
        o_ref[...] = acc.astype(BF16)

    @pl.when(j == 3)
    def _():
        rotary(acc, 1.0)

    @pl.when(j == 4)
    def _():
        rotary(acc, RET_QK_DIM ** -0.5)

    @pl.when((j == 7) | (j == 8))
    def _():
        o_ref[...] = (acc * jax.nn.sigmoid(acc)).astype(BF16)

    @pl.when(j >= IN_COLS // PROJ_TN)
    def _():
        o_ref[...] = jax.nn.sigmoid(acc + b_ref[...]).astype(BF16)


def _proj_call(x2, g, w_all, b_gate, cos_t, sin_t, seq):
    t = x2.shape[0]
    n_in = IN_COLS // PROJ_TN
    return pl.pallas_call(
        _proj_kernel,
        grid=(t // seq, ALL_COLS // PROJ_TN),
        in_specs=[
            pl.BlockSpec((seq, D_MODEL), lambda i, j: (i, 0)),
            pl.BlockSpec((1, D_MODEL), lambda i, j: (0, 0)),
            pl.BlockSpec((D_MODEL, PROJ_TN), lambda i, j: (0, j)),
            pl.BlockSpec((1, PROJ_TN), lambda i, j: (0, jnp.maximum(j - n_in, 0))),
            pl.BlockSpec((seq, LANES), lambda i, j: (0, 0)),
            pl.BlockSpec((seq, LANES), lambda i, j: (0, 0)),
        ],
        out_specs=pl.BlockSpec((seq, PROJ_TN), lambda i, j: (i, j)),
        out_shape=jax.ShapeDtypeStruct((t, ALL_COLS), BF16),
        scratch_shapes=[pltpu.VMEM((seq, D_MODEL), BF16)],
        compiler_params=pltpu.CompilerParams(
            dimension_semantics=("arbitrary", "arbitrary"), vmem_limit_bytes=VMEM_LIMIT),
        name="proj",
    )(x2, g, w_all, b_gate, cos_t, sin_t)


def _sb_kernel(q_ref, k_ref, v_ref, o_ref, u_ref, acc_ref, car_ref):
    seq = q_ref.shape[0]
    nq = seq // SB_BLK
    row = lax.broadcasted_iota(jnp.int32, (SB_BLK, SB_BLK), 0)
    col = lax.broadcasted_iota(jnp.int32, (SB_BLK, SB_BLK), 1)
    u_ref[...] = (row >= col).astype(BF16)
    lane = lax.broadcasted_iota(jnp.int32, (1, LANES), 1)
    head_masks = (lane < SB_HEAD_DIM, lane >= SB_HEAD_DIM)

    heads = range(2)
    causal = col < row

    def rows_of(blk):
        return pl.ds(pl.multiple_of(blk * SB_BLK, SB_BLK), SB_BLK)

    def sweep(q_heads, steps, fresh):
        u = u_ref[...]
        k_blks = [k_ref[rows_of(kb), :] for kb, _ in steps]
        v_blks = [v_ref[rows_of(kb), :] for kb, _ in steps]
        z2 = [[lax.dot_general(q_heads[h], k_blk, (((1,), (1,)), ((), ())),
                               preferred_element_type=F32) * LOG2E for h in heads]
              for k_blk in k_blks]
        sinc = []
        for (_, diagonal), z2_b in zip(steps, z2):
            nlk = [jnp.maximum(z, 0.0) + jnp.log2(1.0 + jnp.exp2(-jnp.abs(z))) for z in z2_b]
            if diagonal:
                nlk = [jnp.where(causal, a, 0.0) for a in nlk]
            hi = [a.astype(BF16) for a in nlk]
            lo = [(a - b.astype(F32)).astype(BF16) for a, b in zip(nlk, hi)]
            sinc.append([jnp.dot(hi[h], u, preferred_element_type=F32)
                         + jnp.dot(lo[h], u, preferred_element_type=F32) for h in heads])
        car = [None, None] if fresh else [car_ref[h] for h in heads]
        contrib = None
        for i, (_, diagonal) in enumerate(steps):
            for h in heads:
                e = z2[i][h] - sinc[i][h]
                if car[h] is not None:
                    e = e - jnp.concatenate([car[h], car[h]], axis=1)
                w = jnp.exp2(e)
                if diagonal:
                    w = jnp.where(causal, w, 0.0)
                vh = jnp.where(head_masks[h], v_blks[i], jnp.zeros_like(v_blks[i]))
                pv = jnp.dot(w.astype(BF16), vh, preferred_element_type=F32)
                contrib = pv if contrib is None else contrib + pv
                tot = jnp.broadcast_to(sinc[i][h][:, 0:1], (SB_BLK, LANES))
                car[h] = tot if car[h] is None else car[h] + tot
        for h in heads:
            car_ref[h] = car[h]
        if fresh:
            acc_ref[...] = contrib
        else:
            acc_ref[...] += contrib

    def q_heads_of(qi):
        q_blk = q_ref[rows_of(qi), :]
        return tuple(jnp.where(m, q_blk, jnp.zeros_like(q_blk)) for m in head_masks)

    sweep(q_heads_of(0), [(0, True)], True)
    o_ref[0:SB_BLK, :] = acc_ref[...].astype(BF16)

    def q_block(qi, carry):
        q_heads = q_heads_of(qi)
        sweep(q_heads, [(qi, True), (qi - 1, False)], True)

        def more(c):
            kb, min_carry = c
            return (kb >= 0) & (min_carry < SB_SKIP_LOG2)

        def k_step(c):
            kb, _ = c
            sweep(q_heads, [(kb, False)], False)
            return kb - 1, jnp.min(car_ref[...])

        lax.while_loop(more, k_step, (qi - 2, jnp.min(car_ref[...])))
        o_ref[rows_of(qi), :] = acc_ref[...].astype(BF16)
        return carry

    lax.fori_loop(1, nq, q_block, 0)


def _sb_call(proj, seq):
    t = proj.shape[0]
    n_pairs = SB_WIDTH // LANES
    return pl.pallas_call(
        _sb_kernel,
        grid=(t // seq, n_pairs),
        in_specs=[
            pl.BlockSpec((seq, LANES), lambda b, p: (b, p)),
            pl.BlockSpec((seq, LANES), lambda b, p: (b, n_pairs + p)),
            pl.BlockSpec((seq, LANES), lambda b, p: (b, 2 * n_pairs + p)),
        ],
        out_specs=pl.BlockSpec((seq, LANES), lambda b, p: (b, p)),
        out_shape=jax.ShapeDtypeStruct((t, SB_WIDTH), BF16),
        scratch_shapes=[
            pltpu.VMEM((SB_BLK, SB_BLK), BF16),
            pltpu.VMEM((SB_BLK, LANES), F32),
            pltpu.VMEM((2, SB_BLK, LANES), F32),
        ],
        compiler_params=pltpu.CompilerParams(
            dimension_semantics=("arbitrary", "arbitrary"), vmem_limit_bytes=VMEM_LIMIT),
        name="stickbreak",
    )(proj, proj, proj)


def _ret_tables():
    h = np.arange(RET_HEADS, dtype=np.float64)
    log_gamma = np.log(1.0 - 2.0 ** (-5.0 - h))
    idx = np.arange(RET_BLK, dtype=np.float64)
    t, s = idx[:, None], idx[None, :]
    same = (t // CHUNK) == (s // CHUNK)
    earlier = (s // CHUNK) < (t // CHUNK)
    expo = np.where(same, np.abs(t - s), np.where(earlier, t - s, 0.0))
    dmat = np.exp(log_gamma[:, None, None] * expo) * (same | earlier)
    qdec = np.exp(log_gamma[:, None] * (idx + 1.0)[None, :])
    kdec = np.exp(log_gamma[:, None] * (RET_BLK - 1.0 - idx)[None, :])
    cdec = np.exp(log_gamma * RET_BLK)
    rep = lambda a: np.broadcast_to(a[..., None], a.shape + (LANES,))
    shp = (RET_HEADS // 2, 2)
    return (jnp.asarray(dmat.reshape(shp + (RET_BLK, RET_BLK)), F32),
            jnp.asarray(rep(qdec).reshape(shp + (RET_BLK, LANES)), F32),
            jnp.asarray(rep(kdec).reshape(shp + (RET_BLK, LANES)), F32),
            jnp.asarray(rep(cdec[:, None]).reshape(shp + (1, LANES)), F32))


def _ret_kernel(q_ref, k_ref, v_ref, g_ref, gn_ref, dm_ref, qd_ref, kd_ref, cd_ref, o_ref, st_ref):
    seq = q_ref.shape[0]
    lane = lax.broadcasted_iota(jnp.int32, (1, LANES), 1)
    head_masks = (lane < RET_QK_DIM, lane >= RET_QK_DIM)
    st_ref[...] = jnp.zeros_like(st_ref)

    def sblock(n, carry):
        rows = pl.ds(pl.multiple_of(n * RET_BLK, RET_BLK), RET_BLK)
        q_blk = q_ref[rows, :]
        k_blk = k_ref[rows, :]
        for h in range(2):
            cols = slice(h * RET_V_DIM, (h + 1) * RET_V_DIM)
            qm = jnp.where(head_masks[h], q_blk, jnp.zeros_like(q_blk))
            km = jnp.where(head_masks[h], k_blk, jnp.zeros_like(k_blk))
            v_h = v_ref[rows, cols]
            scores = lax.dot_general(qm, k_blk, (((1,), (1,)), ((), ())),
                                     preferred_element_type=F32)
            p = (scores * dm_ref[0, h]).astype(BF16)
            y = jnp.dot(p, v_h, preferred_element_type=F32)
            state = st_ref[h]
            qd = (qm.astype(F32) * qd_ref[0, h]).astype(BF16)
            y = y + jnp.dot(qd, state.astype(BF16), preferred_element_type=F32)
            kd = (km.astype(F32) * kd_ref[0, h]).astype(BF16)
            kv = lax.dot_general(kd, v_h, (((0,), (0,)), ((), ())),
                                 preferred_element_type=F32)
            st_ref[h] = state * cd_ref[0, h] + kv
            ms = jnp.mean(y * y, axis=-1, keepdims=True)
            yn = (y * lax.rsqrt(ms + EPS)) * gn_ref[:, cols]
            o_ref[rows, cols] = (g_ref[rows, cols].astype(F32) * yn).astype(BF16)
        return carry

    lax.fori_loop(0, seq // RET_BLK, sblock, 0)


def _ret_call(proj, ret_norm_g, seq):
    t = proj.shape[0]
    n_pairs = RET_HEADS // 2
    dmat, qdec, kdec, cdec = _ret_tables()
    q0 = (3 * SB_WIDTH) // LANES
    k0 = q0 + RET_QK_WIDTH // LANES
    v0 = (3 * SB_WIDTH + 2 * RET_QK_WIDTH) // (2 * RET_V_DIM)
    g0 = v0 + RET_V_WIDTH // (2 * RET_V_DIM)
    return pl.pallas_call(
        _ret_kernel,
        grid=(t // seq, n_pairs),
        in_specs=[
            pl.BlockSpec((seq, LANES), lambda b, p: (b, q0 + p)),
            pl.BlockSpec((seq, LANES), lambda b, p: (b, k0 + p)),
            pl.BlockSpec((seq, 2 * RET_V_DIM), lambda b, p: (b, v0 + p)),
            pl.BlockSpec((seq, 2 * RET_V_DIM), lambda b, p: (b, g0 + p)),
            pl.BlockSpec((1, 2 * RET_V_DIM), lambda b, p: (0, p)),
            pl.BlockSpec((1, 2, RET_BLK, RET_BLK), lambda b, p: (p, 0, 0, 0)),
            pl.BlockSpec((1, 2, RET_BLK, LANES), lambda b, p: (p, 0, 0, 0)),
            pl.BlockSpec((1, 2, RET_BLK, LANES), lambda b, p: (p, 0, 0, 0)),
            pl.BlockSpec((1, 2, 1, LANES), lambda b, p: (p, 0, 0, 0)),
        ],
        out_specs=pl.BlockSpec((seq, 2 * RET_V_DIM), lambda b, p: (b, p)),
        out_shape=jax.ShapeDtypeStruct((t, RET_V_WIDTH), BF16),
        scratch_shapes=[pltpu.VMEM((2, LANES, RET_V_DIM), F32)],
        compiler_params=pltpu.CompilerParams(
            dimension_semantics=("arbitrary", "arbitrary"), vmem_limit_bytes=VMEM_LIMIT),
        name="retention",
    )(proj, proj, proj, proj, ret_norm_g, dmat, qdec, kdec, cdec)


def _mix_kernel(ysb_ref, yret_ref, gs0_ref, gs1_ref, gr0_ref, gr1_ref, x_ref,
                wsb_ref, wret_ref, wout_ref, gffn_ref, wrh_ref, wrl_ref, br_ref, tri_ref,
                hext_ref, cls_ref, cnt_ref):
    tm = x_ref.shape[0]

    @pl.when(pl.program_id(0) == 0)
    def _():
        cnt_ref[...] = jnp.zeros_like(cnt_ref)

    a = jnp.dot(ysb_ref[...], wsb_ref[...], preferred_element_type=F32)
    b = jnp.dot(yret_ref[...], wret_ref[...], preferred_element_type=F32)
    g_sb = jnp.concatenate([gs0_ref[...], gs1_ref[...]], axis=1).astype(F32)
    g_ret = jnp.concatenate([gr0_ref[...], gr1_ref[...]], axis=1).astype(F32)
    mixed = (g_sb * a + g_ret * b).astype(BF16)
    h = x_ref[...] + jnp.dot(mixed, wout_ref[...], preferred_element_type=F32)
    hext_ref[:, 0:D_MODEL] = h

    ms = jnp.mean(h * h, axis=-1, keepdims=True)
    hn = (h * lax.rsqrt(ms + EPS)) * gffn_ref[...]
    hn_hi = hn.astype(BF16)
    hn_lo = (hn - hn_hi.astype(F32)).astype(BF16)
    nt = (((1,), (1,)), ((), ()))
    logits = (lax.dot_general(wrh_ref[...], hn_hi, nt, preferred_element_type=F32)
              + lax.dot_general(wrh_ref[...], hn_lo, nt, preferred_element_type=F32)
              + lax.dot_general(wrl_ref[...], hn_hi, nt, preferred_element_type=F32)
              + br_ref[...])

    def first_argmax(vals):
        m = functools.reduce(jnp.maximum, vals)
        idx = jnp.full(m.shape, len(vals) - 1, jnp.int32)
        for i in range(len(vals) - 2, -1, -1):
            idx = jnp.where(vals[i] >= m, i, idx)
        return m, idx

    gl = [logits[r:r + 1, :] for r in range(N_GROUPS)]
    gmax, gsel = first_argmax(gl)
    p_group = 1.0 / functools.reduce(lambda s, v: s + v, [jnp.exp(v - gmax) for v in gl])
    el = []
    for e in range(EXPERTS_PER_GROUP):
        v = logits[N_GROUPS + 3 * EXPERTS_PER_GROUP + e:N_GROUPS + 3 * EXPERTS_PER_GROUP + e + 1, :]
        for g in range(N_GROUPS - 2, -1, -1):
            r = N_GROUPS + g * EXPERTS_PER_GROUP + e
            v = jnp.where(gsel == g, logits[r:r + 1, :], v)
        el.append(v)
    m1, i1 = first_argmax(el)
    rest = [jnp.where(i1 == e, -jnp.inf, el[e]) for e in range(EXPERTS_PER_GROUP)]
    m2, i2 = first_argmax(rest)
    tt = jnp.exp(m2 - m1)
    gate1 = p_group * (1.0 / (1.0 + tt))
    gate2 = p_group * (tt / (1.0 + tt))
    lo_first = i1 < i2
    ea = jnp.minimum(i1, i2)
    eb = jnp.maximum(i1, i2)
    w_a = jnp.where(lo_first, gate1, gate2)
    w_b = jnp.where(lo_first, gate2, gate1)
    pid = jnp.where(ea == 0, eb - 1, jnp.where(ea == 1, eb + 1, PAIRS_PER_GROUP - 1))
    cls = gsel * PAIRS_PER_GROUP + pid
    info = jnp.concatenate([w_a, w_b, jnp.zeros((INFO_COLS - 2, tm), F32)], axis=0)
    hext_ref[:, D_MODEL:HEXT_COLS] = info.T

    class_row = lax.broadcasted_iota(jnp.int32, (ROUTER_ROWS, tm), 0)
    onehot = class_row == cls
    onehot_bf = onehot.astype(BF16)
    before = jnp.dot(onehot_bf, tri_ref[...], preferred_element_type=F32)
    cnt = cnt_ref[...]
    seen = before + jnp.concatenate([cnt] * (tm // LANES), axis=1)
    rank = jnp.sum(jnp.where(onehot, seen, 0.0), axis=0, keepdims=True)
    cnt_ref[...] = cnt + jnp.dot(onehot_bf, jnp.ones((tm, LANES), BF16), preferred_element_type=F32)
    cls_ref[...] = jnp.concatenate(
        [cls, rank.astype(jnp.int32), jnp.zeros((cls_ref.shape[0] - 2, tm), jnp.int32)], axis=0)


def _mix_call(ysb, yret, proj, x2, wsb, wret, wout, gffn, wr_hi, wr_lo, br):
    t = x2.shape[0]
    tm = MIX_TM
    gate0 = IN_COLS // PROJ_TN
    const = lambda i: (0, 0)
    idx = jnp.arange(tm, dtype=jnp.int32)
    tri = (idx[:, None] < idx[None, :]).astype(BF16)
    return pl.pallas_call(
        _mix_kernel,
        grid=(t // tm,),
        in_specs=[
            pl.BlockSpec((tm, SB_WIDTH), lambda i: (i, 0)),
            pl.BlockSpec((tm, RET_V_WIDTH), lambda i: (i, 0)),
            pl.BlockSpec((tm, PROJ_TN), lambda i: (i, gate0)),
            pl.BlockSpec((tm, PROJ_TN), lambda i: (i, gate0 + 1)),
            pl.BlockSpec((tm, PROJ_TN), lambda i: (i, gate0 + 2)),
            pl.BlockSpec((tm, PROJ_TN), lambda i: (i, gate0 + 3)),
            pl.BlockSpec((tm, D_MODEL), lambda i: (i, 0)),
            pl.BlockSpec((SB_WIDTH, D_MODEL), const),
            pl.BlockSpec((RET_V_WIDTH, D_MODEL), const),
            pl.BlockSpec((D_MODEL, D_MODEL), const),
            pl.BlockSpec((1, D_MODEL), const),
            pl.BlockSpec((ROUTER_ROWS, D_MODEL), const),
            pl.BlockSpec((ROUTER_ROWS, D_MODEL), const),
            pl.BlockSpec((ROUTER_ROWS, 1), const),
            pl.BlockSpec((tm, tm), const),
        ],
        out_specs=[
            pl.BlockSpec((tm, HEXT_COLS), lambda i: (i, 0)),
            pl.BlockSpec((8, tm), lambda i: (0, i)),
            pl.BlockSpec((ROUTER_ROWS, LANES), const),
        ],
        out_shape=[
            jax.ShapeDtypeStruct((t, HEXT_COLS), F32),
            jax.ShapeDtypeStruct((8, t), jnp.int32),
            jax.ShapeDtypeStruct((ROUTER_ROWS, LANES), F32),
        ],
        compiler_params=pltpu.CompilerParams(
            dimension_semantics=("arbitrary",), vmem_limit_bytes=VMEM_LIMIT),
        name="mix_router",
    )(ysb, yret, proj, proj, proj, proj, x2, wsb, wret, wout, gffn, wr_hi, wr_lo, br, tri)


def _moe_kernel(src_ref, nvalid_ref, ea_ref, eb_ref,
                hext_ref, wga_ref, wua_ref, wda_ref, wgb_ref, wub_ref, wdb_ref,
                gffn_ref, gfin_ref, out_ref, hbuf, obuf, gsem, ssem):
    i = pl.program_id(0)
    n_tiles = pl.num_programs(0)
    tm = hbuf.shape[1]
    nv = nvalid_ref[i]
    nxt = jnp.minimum(i + 1, n_tiles - 1)
    has_next = (i + 1 < n_tiles) & (nvalid_ref[nxt] > 0)

    def start_gather(tile, buf_slot, pred):
        for r in range(tm):
            tok = src_ref[tile * tm + r]

            @pl.when(pred)
            def _():
                pltpu.make_async_copy(hext_ref.at[pl.ds(tok, 1), :],
                                      hbuf.at[buf_slot, pl.ds(r, 1), :], gsem.at[buf_slot]).start()

    def wait_gather(buf_slot):
        pltpu.make_async_copy(hext_ref.at[pl.ds(0, tm), :], hbuf.at[buf_slot], gsem.at[buf_slot]).wait()

    def start_scatter(tile, buf_slot, rows_valid):
        for r in range(tm):
            tok = src_ref[tile * tm + r]

            @pl.when(r < rows_valid)
            def _():
                pltpu.make_async_copy(obuf.at[buf_slot, pl.ds(r, 1), :],
                                      out_ref.at[pl.ds(tok, 1), :], ssem.at[buf_slot]).start()

    def wait_scatter(buf_slot, rows_valid):
        @pl.when(rows_valid == tm)
        def _():
            pltpu.make_async_copy(obuf.at[buf_slot], out_ref.at[pl.ds(0, tm), :], ssem.at[buf_slot]).wait()

        @pl.when(rows_valid < tm)
        def _():
            for r in range(tm):
                @pl.when(r < rows_valid)
                def _():
                    pltpu.make_async_copy(obuf.at[buf_slot, pl.ds(r, 1), :],
                                          out_ref.at[pl.ds(0, 1), :], ssem.at[buf_slot]).wait()

    @pl.when(i == 0)
    def _():
        start_gather(0, 0, nv > 0)

    def tile_body(slot):
        @pl.when(i >= 2)
        def _():
            wait_scatter(slot, nvalid_ref[jnp.maximum(i - 2, 0)])

        wait_gather(slot)
        start_gather(nxt, 1 - slot, has_next)
        hrows = hbuf[slot]
        h = hrows[:, 0:D_MODEL]
        w_a = hrows[:, D_MODEL:D_MODEL + 1]
        w_b = hrows[:, D_MODEL + 1:D_MODEL + 2]
        ms = jnp.mean(h * h, axis=-1, keepdims=True)
        hn = ((h * lax.rsqrt(ms + EPS)) * gffn_ref[...]).astype(BF16)

        def expert(wg_ref, wu_ref, wd_ref):
            gate = jnp.dot(hn, wg_ref[0], preferred_element_type=F32)
            up = jnp.dot(hn, wu_ref[0], preferred_element_type=F32)
            hidden = ((gate * jax.nn.sigmoid(gate)) * up).astype(BF16)
            return jnp.dot(hidden, wd_ref[0], preferred_element_type=F32)

        y = w_a * expert(wga_ref, wua_ref, wda_ref) + w_b * expert(wgb_ref, wub_ref, wdb_ref)
        h2 = h + y
        ms2 = jnp.mean(h2 * h2, axis=-1, keepdims=True)
        obuf[slot] = (h2 * lax.rsqrt(ms2 + EPS)) * gfin_ref[...]
        start_scatter(i, slot, nv)

        @pl.when(jnp.logical_not(has_next))
        def _():
            @pl.when(i >= 1)
            def _():
                wait_scatter(1 - slot, nvalid_ref[jnp.maximum(i - 1, 0)])

            wait_scatter(slot, nv)

    for parity in range(2):
        pl.when((nv > 0) & (i % 2 == parity))(functools.partial(tile_body, parity))


def _moe_call(src, nvalid, ea, eb, hext, wg, wu, wd, gffn, gfin):
    t = hext.shape[0]
    tm = MOE_TM
    n_tiles = nvalid.shape[0]
    wa_map = lambda i, src, nv, ea, eb: (ea[i], 0, 0)
    wb_map = lambda i, src, nv, ea, eb: (eb[i], 0, 0)
    const = lambda i, src, nv, ea, eb: (0, 0)
    return pl.pallas_call(
        _moe_kernel,
        grid_spec=pltpu.PrefetchScalarGridSpec(
            num_scalar_prefetch=4,
            grid=(n_tiles,),
            in_specs=[
                pl.BlockSpec(memory_space=pl.ANY),
                pl.BlockSpec((1, D_MODEL, D_FF), wa_map),
                pl.BlockSpec((1, D_MODEL, D_FF), wa_map),
                pl.BlockSpec((1, D_FF, D_MODEL), wa_map),
                pl.BlockSpec((1, D_MODEL, D_FF), wb_map),
                pl.BlockSpec((1, D_MODEL, D_FF), wb_map),
                pl.BlockSpec((1, D_FF, D_MODEL), wb_map),
                pl.BlockSpec((1, D_MODEL), const),
                pl.BlockSpec((1, D_MODEL), const),
            ],
            out_specs=pl.BlockSpec(memory_space=pl.ANY),
            scratch_shapes=[
                pltpu.VMEM((2, tm, HEXT_COLS), F32),
                pltpu.VMEM((2, tm, D_MODEL), F32),
                pltpu.SemaphoreType.DMA((2,)),
                pltpu.SemaphoreType.DMA((2,)),
            ],
        ),
        out_shape=jax.ShapeDtypeStruct((t, D_MODEL), F32),
        compiler_params=pltpu.CompilerParams(
            dimension_semantics=("arbitrary",), vmem_limit_bytes=VMEM_LIMIT),
        name="experts",
    )(src, nvalid, ea, eb, hext, wg, wu, wd, wg, wu, wd, gffn, gfin)


def _routing_plan(cls, rank, counts, tm):
    t = cls.shape[0]
    n_tiles = t // tm + N_CLASSES
    tiles_c = (counts + tm - 1) // tm
    tile_end = jnp.cumsum(tiles_c)
    tile_off = tile_end - tiles_c
    onehot = cls[:, None] == jnp.arange(N_CLASSES, dtype=jnp.int32)[None, :]
    dest = jnp.sum(jnp.where(onehot, tile_off[None, :], 0), axis=1) * tm + rank
    src = jnp.zeros((n_tiles * tm,), jnp.int32).at[dest].set(jnp.arange(t, dtype=jnp.int32))
    tile = jnp.arange(n_tiles, dtype=jnp.int32)
    used = tile < tile_end[-1]
    tile_cls = jnp.minimum(jnp.sum((tile[:, None] >= tile_end[None, :]).astype(jnp.int32), axis=1),
                           N_CLASSES - 1)
    cls_onehot = tile_cls[:, None] == jnp.arange(N_CLASSES, dtype=jnp.int32)[None, :]
    pick = lambda table: jnp.sum(jnp.where(cls_onehot, table[None, :], 0), axis=1)
    nvalid = jnp.where(used, jnp.clip(pick(counts) - (tile - pick(tile_off)) * tm, 0, tm), 0)
    classes = np.arange(N_CLASSES)
    ea_tab = jnp.asarray((classes // PAIRS_PER_GROUP) * EXPERTS_PER_GROUP
                         + np.asarray(_PAIR_A)[classes % PAIRS_PER_GROUP], jnp.int32)
    eb_tab = jnp.asarray((classes // PAIRS_PER_GROUP) * EXPERTS_PER_GROUP
                         + np.asarray(_PAIR_B)[classes % PAIRS_PER_GROUP], jnp.int32)
    last_used = jnp.sum(jnp.where(tile == tile_end[-1] - 1, tile_cls, 0))
    tile_cls = jnp.where(used, tile_cls, last_used)
    cls_onehot = tile_cls[:, None] == jnp.arange(N_CLASSES, dtype=jnp.int32)[None, :]
    return src, nvalid.astype(jnp.int32), pick(ea_tab), pick(eb_tab)


def _rope_tables(seq):
    half = RET_QK_DIM // 2
    inv_freq = ROPE_BASE ** (-jnp.arange(half, dtype=F32) / half)
    ang = jnp.arange(seq, dtype=F32)[:, None] * inv_freq[None, :]
    cos, sin = jnp.cos(ang), jnp.sin(ang)
    reps = LANES // RET_QK_DIM
    cos_t = jnp.tile(jnp.concatenate([cos, cos], axis=1), (1, reps))
    sin_t = jnp.tile(jnp.concatenate([-sin, sin], axis=1), (1, reps))
    return cos_t, sin_t


def kernel(x, norm_mix_g, w_in, w_gate, b_gate, w_sb_out, w_ret_out, ret_norm_g, w_out,
           norm_ffn_g, w_group_router, b_group_router, w_expert_router, b_expert_router,
           w_exp_gate, w_exp_up, w_exp_down, norm_final_g):
    bsz, seq, d = x.shape
    assert d == D_MODEL and w_in.shape[0] == 1 and seq % SB_BLK == 0 and (bsz * seq) % MIX_TM == 0
    t = bsz * seq
    x2 = x.reshape(t, d)
    w_all = jnp.concatenate([w_in[0], w_gate[0]], axis=1).astype(BF16)
    cos_t, sin_t = _rope_tables(seq)
    proj = _proj_call(x2, norm_mix_g[0][None, :], w_all, b_gate[0][None, :], cos_t, sin_t, seq)
    ysb = _sb_call(proj, seq)
    yret = _ret_call(proj, ret_norm_g[0][None, :], seq)

    wr = jnp.concatenate([w_group_router[0], w_expert_router[0]], axis=1).T
    wr = jnp.pad(wr, ((0, ROUTER_ROWS - wr.shape[0]), (0, 0)))
    wr_hi = wr.astype(BF16)
    wr_lo = (wr - wr_hi.astype(F32)).astype(BF16)
    br = jnp.concatenate([b_group_router[0], b_expert_router[0]])
    br = jnp.pad(br, (0, ROUTER_ROWS - br.shape[0]))[:, None]
    hext, cls8, cnt = _mix_call(ysb, yret, proj, x2, w_sb_out[0].astype(BF16), w_ret_out[0].astype(BF16),
                           w_out[0].astype(BF16), norm_ffn_g[0][None, :], wr_hi, wr_lo, br)

    counts = cnt[:N_CLASSES, 0].astype(jnp.int32)
    src, nvalid, ea, eb = _routing_plan(cls8[0], cls8[1], counts, MOE_TM)
    out = _moe_call(src, nvalid, ea, eb, hext, w_exp_gate[0].astype(BF16), w_exp_up[0].astype(BF16),
                    w_exp_down[0].astype(BF16), norm_ffn_g[0][None, :], norm_final_g[None, :])
    return out.reshape(bsz, seq, d)
```

```python
import functools
import math

import numpy as np
import jax
import jax.numpy as jnp
from jax import lax
from jax.experimental import pallas as pl
from jax.experimental.pallas import tpu as pltpu

F32 = jnp.float32
BF16 = jnp.bfloat16

D_MODEL = 1024
SB_HEADS = 8
SB_HEAD_DIM = 64
SB_WIDTH = SB_HEADS * SB_HEAD_DIM
RET_HEADS = 8
RET_QK_DIM = 64
RET_V_DIM = 128
RET_QK_WIDTH = RET_HEADS * RET_QK_DIM
RET_V_WIDTH = RET_HEADS * RET_V_DIM
IN_COLS = 3 * SB_WIDTH + 2 * RET_QK_WIDTH + 2 * RET_V_WIDTH
GATE_COLS = 2 * D_MODEL
ALL_COLS = IN_COLS + GATE_COLS
CHUNK = 64
ROPE_BASE = 10000.0
N_GROUPS = 4
EXPERTS_PER_GROUP = 4
N_EXPERTS = N_GROUPS * EXPERTS_PER_GROUP
D_FF = 512
EPS = 1e-6

LANES = 128
PROJ_TN = 1280
GATE_BLK = 512
PROJ_ROWS = 256
SB_BLK = 256
RET_BLK = 256
RET_UNROLL = 2
MIX_TM = 512
MOE_TM = 256
MOE_INVERT_UNROLL = 16
ROUTER_ROWS = 32
PAIRS_PER_GROUP = 6
N_CLASSES = N_GROUPS * PAIRS_PER_GROUP
INFO_COLS = LANES
HEXT_COLS = D_MODEL + INFO_COLS
VMEM_LIMIT = 56 * 1024 * 1024
LOG2E = math.log2(math.e)
SB_SKIP_LOG2 = 153.0

_PAIR_A = (0, 0, 0, 1, 1, 2)
_PAIR_B = (1, 2, 3, 2, 3, 3)


def _proj_plain_kernel(x_ref, g_ref, w_ref, s_ref, o_ref, xn_ref):
    @pl.when(pl.program_id(1) == 0)
    def _():
        x = x_ref[...]
        ms = jnp.mean(x * x, axis=-1, keepdims=True)
        xn_ref[...] = ((x * lax.rsqrt(ms + EPS)) * g_ref[...]).astype(BF16)

    for rows in _row_chunks(xn_ref.shape[0]):
        acc = jnp.dot(xn_ref[rows, :], w_ref[...], preferred_element_type=F32)
        o_ref[rows, :] = (acc * s_ref[...]).astype(BF16)


def _row_chunks(n_rows):
    return [slice(r, r + PROJ_ROWS) for r in range(0, n_rows, PROJ_ROWS)]


def _proj_rot_kernel(xn_ref, w_ref, s_ref, cos_ref, sin_ref, o_ref):
    for rows in _row_chunks(xn_ref.shape[0]):
        acc = jnp.dot(xn_ref[rows, :], w_ref[...], preferred_element_type=F32)
        cos = cos_ref[rows, :]
        sin = sin_ref[rows, :]
        for p in range(w_ref.shape[1] // LANES):
            cols = slice(p * LANES, (p + 1) * LANES)
            seg = acc[:, cols]
            rot = seg * cos + pltpu.roll(seg, LANES // 2, 1) * sin
            o_ref[rows, cols] = (rot * s_ref[:, cols]).astype(BF16)


def _proj_act_kernel(xn_ref, w_ref, b_ref, m_ref, o_ref):
    for rows in _row_chunks(xn_ref.shape[0]):
        acc = jnp.dot(xn_ref[rows, :], w_ref[...], preferred_element_type=F32)
        sig = 0.5 * jnp.tanh(0.5 * (acc + b_ref[...])) + 0.5
        o_ref[rows, :] = (sig * (acc * m_ref[0:1, :] + m_ref[1:2, :])).astype(BF16)


def _proj_params():
    return pltpu.CompilerParams(dimension_semantics=("arbitrary", "arbitrary"),
                                vmem_limit_bytes=VMEM_LIMIT)


def _proj_tile(n):
    tn = PROJ_TN
    while n % tn:
        tn -= LANES
    return tn


def _proj_plain_call(x2, g, w, scale, seq):
    t, n = x2.shape[0], w.shape[1]
    tn = _proj_tile(n)
    return pl.pallas_call(
        _proj_plain_kernel,
        grid=(t // seq, n // tn),
        in_specs=[
            pl.BlockSpec((seq, D_MODEL), lambda i, j: (i, 0)),
            pl.BlockSpec((1, D_MODEL), lambda i, j: (0, 0)),
            pl.BlockSpec((D_MODEL, tn), lambda i, j: (0, j)),
            pl.BlockSpec((1, tn), lambda i, j: (0, j)),
        ],
        out_specs=[
            pl.BlockSpec((seq, tn), lambda i, j: (i, j)),
            pl.BlockSpec((seq, D_MODEL), lambda i, j: (i, 0)),
        ],
        out_shape=[jax.ShapeDtypeStruct((t, n), BF16), jax.ShapeDtypeStruct((t, D_MODEL), BF16)],
        compiler_params=_proj_params(),
        name="proj_plain",
    )(x2, g, w, scale)


def _proj_rot_call(xn, w, scale, cos_t, sin_t, seq):
    t, n = xn.shape[0], w.shape[1]
    tn = _proj_tile(n)
    return pl.pallas_call(
        _proj_rot_kernel,
        grid=(t // seq, n // tn),
        in_specs=[
            pl.BlockSpec((seq, D_MODEL), lambda i, j: (i, 0)),
            pl.BlockSpec((D_MODEL, tn), lambda i, j: (0, j)),
            pl.BlockSpec((1, tn), lambda i, j: (0, j)),
            pl.BlockSpec((seq, LANES), lambda i, j: (0, 0)),
            pl.BlockSpec((seq, LANES), lambda i, j: (0, 0)),
        ],
        out_specs=pl.BlockSpec((seq, tn), lambda i, j: (i, j)),
        out_shape=jax.ShapeDtypeStruct((t, n), BF16),
        compiler_params=_proj_params(),
        name="proj_rotary",
    )(xn, w, scale, cos_t, sin_t)


def _proj_act_call(xn, w, bias, mode, seq):
    t, n = xn.shape[0], w.shape[1]
    tn = _proj_tile(n)
    return pl.pallas_call(
        _proj_act_kernel,
        grid=(t // seq, n // tn),
        in_specs=[
            pl.BlockSpec((seq, D_MODEL), lambda i, j: (i, 0)),
            pl.BlockSpec((D_MODEL, tn), lambda i, j: (0, j)),
            pl.BlockSpec((1, tn), lambda i, j: (0, j)),
            pl.BlockSpec((2, tn), lambda i, j: (0, j)),
        ],
        out_specs=pl.BlockSpec((seq, tn), lambda i, j: (i, j)),
        out_shape=jax.ShapeDtypeStruct((t, n), BF16),
        compiler_params=_proj_params(),
        name="proj_act",
    )(xn, w, bias, mode)


def _rot_perm():
    half = RET_QK_DIM // 2
    order = [p * LANES + hh * RET_QK_DIM + part * half + d
             for p in range(RET_QK_WIDTH // LANES) for part in range(2) for hh in range(2)
             for d in range(half)]
    return np.asarray(order, np.int32)


def _sb_kernel(q_ref, k_ref, v_ref, o_ref, u_ref, acc_ref, car_ref):
    seq = q_ref.shape[0]
    nq = seq // SB_BLK
    row = lax.broadcasted_iota(jnp.int32, (SB_BLK, SB_BLK), 0)
    col = lax.broadcasted_iota(jnp.int32, (SB_BLK, SB_BLK), 1)
    u_ref[...] = (row >= col).astype(BF16)
    lane = lax.broadcasted_iota(jnp.int32, (1, LANES), 1)
    head_masks = (lane < SB_HEAD_DIM, lane >= SB_HEAD_DIM)

    heads = range(2)
    causal = col < row

    def rows_of(blk):
        return pl.ds(pl.multiple_of(blk * SB_BLK, SB_BLK), SB_BLK)

    def sweep(q_heads, steps, fresh):
        u = u_ref[...]
        k_blks = [k_ref[rows_of(kb), :] for kb, _ in steps]
        v_blks = [v_ref[rows_of(kb), :] for kb, _ in steps]
        z2 = [[lax.dot_general(q_heads[h], k_blk, (((1,), (1,)), ((), ())),
                               preferred_element_type=F32) * LOG2E for h in heads]
              for k_blk in k_blks]
        sinc = []
        for (_, diagonal), z2_b in zip(steps, z2):
            nlk = [jnp.maximum(z, 0.0) + jnp.log2(1.0 + jnp.exp2(-jnp.abs(z))) for z in z2_b]
            if diagonal:
                nlk = [jnp.where(causal, a, 0.0) for a in nlk]
            hi = [a.astype(BF16) for a in nlk]
            lo = [(a - b.astype(F32)).astype(BF16) for a, b in zip(nlk, hi)]
            sinc.append([jnp.dot(hi[h], u, preferred_element_type=F32)
                         + jnp.dot(lo[h], u, preferred_element_type=F32) for h in heads])
        car = [None, None] if fresh else [car_ref[h] for h in heads]
        contrib = None
        for i, (_, diagonal) in enumerate(steps):
            for h in heads:
                e = z2[i][h] - sinc[i][h]
                if car[h] is not None:
                    e = e - jnp.concatenate([car[h], car[h]], axis=1)
                w = jnp.exp2(e)
                if diagonal:
                    w = jnp.where(causal, w, 0.0)
                vh = jnp.where(head_masks[h], v_blks[i], jnp.zeros_like(v_blks[i]))
                pv = jnp.dot(w.astype(BF16), vh, preferred_element_type=F32)
                contrib = pv if contrib is None else contrib + pv
                tot = jnp.broadcast_to(sinc[i][h][:, 0:1], (SB_BLK, LANES))
                car[h] = tot if car[h] is None else car[h] + tot
        for h in heads:
            car_ref[h] = car[h]
        if fresh:
            acc_ref[...] = contrib
        else:
            acc_ref[...] += contrib

    def q_heads_of(qi):
        q_blk = q_ref[rows_of(qi), :]
        return tuple(jnp.where(m, q_blk, jnp.zeros_like(q_blk)) for m in head_masks)

    sweep(q_heads_of(0), [(0, True)], True)
    o_ref[0:SB_BLK, :] = acc_ref[...].astype(BF16)

    def q_block(qi, carry):
        q_heads = q_heads_of(qi)
        sweep(q_heads, [(qi, True), (qi - 1, False)], True)

        def more(c):
            kb, min_carry = c
            return (kb >= 0) & (min_carry < SB_SKIP_LOG2)

        def k_step(c):
            kb, _ = c
            sweep(q_heads, [(kb, False)], False)
            return kb - 1, jnp.min(car_ref[...])

        lax.while_loop(more, k_step, (qi - 2, jnp.min(car_ref[...])))
        o_ref[rows_of(qi), :] = acc_ref[...].astype(BF16)
        return carry

    lax.fori_loop(1, nq, q_block, 0)


def _sb_call(proj, seq):
    t = proj.shape[0]
    n_pairs = SB_WIDTH // LANES
    return pl.pallas_call(
        _sb_kernel,
        grid=(t // seq, n_pairs),
        in_specs=[
            pl.BlockSpec((seq, LANES), lambda b, p: (b, p)),
            pl.BlockSpec((seq, LANES), lambda b, p: (b, n_pairs + p)),
            pl.BlockSpec((seq, LANES), lambda b, p: (b, 2 * n_pairs + p)),
        ],
        out_specs=pl.BlockSpec((seq, LANES), lambda b, p: (b, p)),
        out_shape=jax.ShapeDtypeStruct((t, SB_WIDTH), BF16),
        scratch_shapes=[
            pltpu.VMEM((SB_BLK, SB_BLK), BF16),
            pltpu.VMEM((SB_BLK, LANES), F32),
            pltpu.VMEM((2, SB_BLK, LANES), F32),
        ],
        compiler_params=pltpu.CompilerParams(
            dimension_semantics=("arbitrary", "arbitrary"), vmem_limit_bytes=VMEM_LIMIT),
        name="stickbreak",
    )(proj, proj, proj)


def _ret_tables():
    h = np.arange(RET_HEADS, dtype=np.float64)
    log_gamma = np.log(1.0 - 2.0 ** (-5.0 - h))
    idx = np.arange(RET_BLK, dtype=np.float64)
    t, s = idx[:, None], idx[None, :]
    same = (t // CHUNK) == (s // CHUNK)
    earlier = (s // CHUNK) < (t // CHUNK)
    expo = np.where(same, np.abs(t - s), np.where(earlier, t - s, 0.0))
    dmat = np.exp(log_gamma[:, None, None] * expo) * (same | earlier)
    qdec = np.exp(log_gamma[:, None] * (idx + 1.0)[None, :])
    kdec = np.exp(log_gamma[:, None] * (RET_BLK - 1.0 - idx)[None, :])
    cdec = np.exp(log_gamma * RET_BLK)
    rep = lambda a: np.broadcast_to(a[..., None], a.shape + (LANES,))
    shp = (RET_HEADS // 2, 2)
    return (jnp.asarray(dmat.reshape(shp + (RET_BLK, RET_BLK)), F32),
            jnp.asarray(rep(qdec).reshape(shp + (RET_BLK, LANES)), F32),
            jnp.asarray(rep(kdec).reshape(shp + (RET_BLK, LANES)), F32),
            jnp.asarray(rep(cdec[:, None]).reshape(shp + (1, LANES)), F32))


def _ret_kernel(q_ref, k_ref, v_ref, g_ref, gn_ref, dm_ref, qd_ref, kd_ref, cd_ref, o_ref, st_ref):
    seq = q_ref.shape[0]
    lane = lax.broadcasted_iota(jnp.int32, (1, LANES), 1)
    head0 = (lane % RET_QK_DIM) < (RET_QK_DIM // 2)
    head_masks = (head0, jnp.logical_not(head0))
    heads = range(2)
    nt = (((1,), (1,)), ((), ()))
    tn = (((0,), (0,)), ((), ()))
    st_ref[...] = jnp.zeros_like(st_ref)

    def step(n, carry):
        subs = range(RET_UNROLL)
        rows = [pl.ds(pl.multiple_of((n * RET_UNROLL + u) * RET_BLK, RET_BLK), RET_BLK) for u in subs]
        col = [slice(h * RET_V_DIM, (h + 1) * RET_V_DIM) for h in heads]
        q_blk = [q_ref[r, :] for r in rows]
        k_blk = [k_ref[r, :] for r in rows]
        qm = [[jnp.where(head_masks[h], q_blk[u], jnp.zeros_like(q_blk[u])) for h in heads] for u in subs]
        km = [[jnp.where(head_masks[h], k_blk[u], jnp.zeros_like(k_blk[u])) for h in heads] for u in subs]
        v = [[v_ref[rows[u], col[h]] for h in heads] for u in subs]
        scores = [[lax.dot_general(qm[u][h], k_blk[u], nt, preferred_element_type=F32)
                   for h in heads] for u in subs]
        kd = [[(km[u][h].astype(F32) * kd_ref[0, h]).astype(BF16) for h in heads] for u in subs]
        kv = [[lax.dot_general(kd[u][h], v[u][h], tn, preferred_element_type=F32)
               for h in heads] for u in subs]
        qd = [[(qm[u][h].astype(F32) * qd_ref[0, h]).astype(BF16) for h in heads] for u in subs]
        p = [[(scores[u][h] * dm_ref[0, h]).astype(BF16) for h in heads] for u in subs]
        intra = [[jnp.dot(p[u][h], v[u][h], preferred_element_type=F32) for h in heads] for u in subs]
        state = [st_ref[h] for h in heads]
        for u in subs:
            for h in heads:
                y = intra[u][h] + jnp.dot(qd[u][h], state[h].astype(BF16), preferred_element_type=F32)
                state[h] = state[h] * cd_ref[0, h] + kv[u][h]
                ms = jnp.mean(y * y, axis=-1, keepdims=True)
                yn = (y * lax.rsqrt(ms + EPS)) * gn_ref[:, col[h]]
                o_ref[rows[u], col[h]] = (g_ref[rows[u], col[h]].astype(F32) * yn).astype(BF16)
        for h in heads:
            st_ref[h] = state[h]
        return carry

    lax.fori_loop(0, seq // (RET_BLK * RET_UNROLL), step, 0)


def _ret_call(rqk, plain, act, ret_norm_g, seq):
    t = rqk.shape[0]
    n_pairs = RET_HEADS // 2
    dmat, qdec, kdec, cdec = _ret_tables()
    k0 = RET_QK_WIDTH // LANES
    v0 = (3 * SB_WIDTH) // (2 * RET_V_DIM)
    return pl.pallas_call(
        _ret_kernel,
        grid=(t // seq, n_pairs),
        in_specs=[
            pl.BlockSpec((seq, LANES), lambda b, p: (b, p)),
            pl.BlockSpec((seq, LANES), lambda b, p: (b, k0 + p)),
            pl.BlockSpec((seq, 2 * RET_V_DIM), lambda b, p: (b, v0 + p)),
            pl.BlockSpec((seq, 2 * RET_V_DIM), lambda b, p: (b, p)),
            pl.BlockSpec((1, 2 * RET_V_DIM), lambda b, p: (0, p)),
            pl.BlockSpec((1, 2, RET_BLK, RET_BLK), lambda b, p: (p, 0, 0, 0)),
            pl.BlockSpec((1, 2, RET_BLK, LANES), lambda b, p: (p, 0, 0, 0)),
            pl.BlockSpec((1, 2, RET_BLK, LANES), lambda b, p: (p, 0, 0, 0)),
            pl.BlockSpec((1, 2, 1, LANES), lambda b, p: (p, 0, 0, 0)),
        ],
        out_specs=pl.BlockSpec((seq, 2 * RET_V_DIM), lambda b, p: (b, p)),
        out_shape=jax.ShapeDtypeStruct((t, RET_V_WIDTH), BF16),
        scratch_shapes=[pltpu.VMEM((2, LANES, RET_V_DIM), F32)],
        compiler_params=pltpu.CompilerParams(
            dimension_semantics=("arbitrary", "arbitrary"), vmem_limit_bytes=VMEM_LIMIT),
        name="retention",
    )(rqk, rqk, plain, act, ret_norm_g, dmat, qdec, kdec, cdec)


def _mix_kernel(ysb_ref, yret_ref, gs0_ref, gs1_ref, gr0_ref, gr1_ref, x_ref,
                wsb_ref, wret_ref, wout_ref, gffn_ref, wrh_ref, wrl_ref, br_ref, tri_ref,
                hext_ref, cls_ref, cnt_ref):
    tm = x_ref.shape[0]

    @pl.when(pl.program_id(0) == 0)
    def _():
        cnt_ref[...] = jnp.zeros_like(cnt_ref)

    a = jnp.dot(ysb_ref[...], wsb_ref[...], preferred_element_type=F32)
    b = jnp.dot(yret_ref[...], wret_ref[...], preferred_element_type=F32)
    g_sb = jnp.concatenate([gs0_ref[...], gs1_ref[...]], axis=1).astype(F32)
    g_ret = jnp.concatenate([gr0_ref[...], gr1_ref[...]], axis=1).astype(F32)
    mixed = (g_sb * a + g_ret * b).astype(BF16)
    h = x_ref[...] + jnp.dot(mixed, wout_ref[...], preferred_element_type=F32)
    hext_ref[:, 0:D_MODEL] = h

    ms = jnp.mean(h * h, axis=-1, keepdims=True)
    hn = (h * lax.rsqrt(ms + EPS)) * gffn_ref[...]
    hn_hi = hn.astype(BF16)
    hn_lo = (hn - hn_hi.astype(F32)).astype(BF16)
    nt = (((1,), (1,)), ((), ()))
    logits = (lax.dot_general(wrh_ref[...], hn_hi, nt, preferred_element_type=F32)
              + lax.dot_general(wrh_ref[...], hn_lo, nt, preferred_element_type=F32)
              + lax.dot_general(wrl_ref[...], hn_hi, nt, preferred_element_type=F32)
              + br_ref[...])

    def first_argmax(vals):
        m = functools.reduce(jnp.maximum, vals)
        idx = jnp.full(m.shape, len(vals) - 1, jnp.int32)
        for i in range(len(vals) - 2, -1, -1):
            idx = jnp.where(vals[i] >= m, i, idx)
        return m, idx

    gl = [logits[r:r + 1, :] for r in range(N_GROUPS)]
    gmax, gsel = first_argmax(gl)
    p_group = 1.0 / functools.reduce(lambda s, v: s + v, [jnp.exp(v - gmax) for v in gl])
    el = []
    for e in range(EXPERTS_PER_GROUP):
        v = logits[N_GROUPS + 3 * EXPERTS_PER_GROUP + e:N_GROUPS + 3 * EXPERTS_PER_GROUP + e + 1, :]
        for g in range(N_GROUPS - 2, -1, -1):
            r = N_GROUPS + g * EXPERTS_PER_GROUP + e
            v = jnp.where(gsel == g, logits[r:r + 1, :], v)
        el.append(v)
    m1, i1 = first_argmax(el)
    rest = [jnp.where(i1 == e, -jnp.inf, el[e]) for e in range(EXPERTS_PER_GROUP)]
    m2, i2 = first_argmax(rest)
    tt = jnp.exp(m2 - m1)
    gate1 = p_group * (1.0 / (1.0 + tt))
    gate2 = p_group * (tt / (1.0 + tt))
    lo_first = i1 < i2
    ea = jnp.minimum(i1, i2)
    eb = jnp.maximum(i1, i2)
    w_a = jnp.where(lo_first, gate1, gate2)
    w_b = jnp.where(lo_first, gate2, gate1)
    pid = jnp.where(ea == 0, eb - 1, jnp.where(ea == 1, eb + 1, PAIRS_PER_GROUP - 1))
    cls = gsel * PAIRS_PER_GROUP + pid
    info = jnp.concatenate([w_a, w_b, jnp.zeros((INFO_COLS - 2, tm), F32)], axis=0)
    hext_ref[:, D_MODEL:HEXT_COLS] = info.T

    class_row = lax.broadcasted_iota(jnp.int32, (ROUTER_ROWS, tm), 0)
    onehot = class_row == cls
    onehot_bf = onehot.astype(BF16)
    before = jnp.dot(onehot_bf, tri_ref[...], preferred_element_type=F32)
    cnt = cnt_ref[...]
    seen = before + jnp.concatenate([cnt] * (tm // LANES), axis=1)
    rank = jnp.sum(jnp.where(onehot, seen, 0.0), axis=0, keepdims=True)
    cnt_ref[...] = cnt + jnp.dot(onehot_bf, jnp.ones((tm, LANES), BF16), preferred_element_type=F32)
    cls_ref[...] = jnp.concatenate(
        [cls, rank.astype(jnp.int32), jnp.zeros((cls_ref.shape[0] - 2, tm), jnp.int32)], axis=0)


def _mix_call(ysb, yret, act, x2, wsb, wret, wout, gffn, wr_hi, wr_lo, br):
    t = x2.shape[0]
    tm = MIX_TM
    gate0 = RET_V_WIDTH // GATE_BLK
    const = lambda i: (0, 0)
    idx = jnp.arange(tm, dtype=jnp.int32)
    tri = (idx[:, None] < idx[None, :]).astype(BF16)
    return pl.pallas_call(
        _mix_kernel,
        grid=(t // tm,),
        in_specs=[
            pl.BlockSpec((tm, SB_WIDTH), lambda i: (i, 0)),
            pl.BlockSpec((tm, RET_V_WIDTH), lambda i: (i, 0)),
            pl.BlockSpec((tm, GATE_BLK), lambda i: (i, gate0)),
            pl.BlockSpec((tm, GATE_BLK), lambda i: (i, gate0 + 1)),
            pl.BlockSpec((tm, GATE_BLK), lambda i: (i, gate0 + 2)),
            pl.BlockSpec((tm, GATE_BLK), lambda i: (i, gate0 + 3)),
            pl.BlockSpec((tm, D_MODEL), lambda i: (i, 0)),
            pl.BlockSpec((SB_WIDTH, D_MODEL), const),
            pl.BlockSpec((RET_V_WIDTH, D_MODEL), const),
            pl.BlockSpec((D_MODEL, D_MODEL), const),
            pl.BlockSpec((1, D_MODEL), const),
            pl.BlockSpec((ROUTER_ROWS, D_MODEL), const),
            pl.BlockSpec((ROUTER_ROWS, D_MODEL), const),
            pl.BlockSpec((ROUTER_ROWS, 1), const),
            pl.BlockSpec((tm, tm), const),
        ],
        out_specs=[
            pl.BlockSpec((tm, HEXT_COLS), lambda i: (i, 0)),
            pl.BlockSpec((8, tm), lambda i: (0, i)),
            pl.BlockSpec((ROUTER_ROWS, LANES), const),
        ],
        out_shape=[
            jax.ShapeDtypeStruct((t, HEXT_COLS), F32),
            jax.ShapeDtypeStruct((8, t), jnp.int32),
            jax.ShapeDtypeStruct((ROUTER_ROWS, LANES), F32),
        ],
        compiler_params=pltpu.CompilerParams(
            dimension_semantics=("arbitrary",), vmem_limit_bytes=VMEM_LIMIT),
        name="mix_router",
    )(ysb, yret, act, act, act, act, x2, wsb, wret, wout, gffn, wr_hi, wr_lo, br, tri)


def _moe_kernel(dest_ref, nvalid_ref, ea_ref, eb_ref,
                hext_ref, wga_ref, wua_ref, wda_ref, wgb_ref, wub_ref, wdb_ref,
                gffn_ref, gfin_ref, out_ref, hbuf, obuf, src, gsem, ssem):
    i = pl.program_id(0)
    n_tiles = pl.num_programs(0)
    tm = hbuf.shape[1]
    nv = nvalid_ref[i]
    nxt = jnp.minimum(i + 1, n_tiles - 1)
    nv_next = jnp.where(i + 1 < n_tiles, nvalid_ref[nxt], 0)
    has_next = nv_next > 0

    def gather_copy(tile, buf_slot, r):
        tok = src[tile * tm + r]
        return pltpu.make_async_copy(hext_ref.at[pl.ds(tok, 1), :],
                                     hbuf.at[buf_slot, pl.ds(r, 1), :], gsem.at[buf_slot])

    def scatter_copy(tile, buf_slot, r):
        tok = src[tile * tm + r]
        return pltpu.make_async_copy(obuf.at[buf_slot, pl.ds(r, 1), :],
                                     out_ref.at[pl.ds(tok, 1), :], ssem.at[buf_slot])

    def start_rows(copy_of_row, rows_valid):
        for r in range(tm):
            @pl.when(r < rows_valid)
            def _():
                copy_of_row(r).start(priority=r % 2)

    def wait_rows(full_copy, row_copy, rows_valid):
        @pl.when(rows_valid == tm)
        def _():
            full_copy.wait()

        @pl.when(rows_valid < tm)
        def _():
            for r in range(tm):
                @pl.when(r < rows_valid)
                def _():
                    row_copy(r).wait()

    def wait_gather(buf_slot, rows_valid):
        wait_rows(pltpu.make_async_copy(hext_ref.at[pl.ds(0, tm), :], hbuf.at[buf_slot], gsem.at[buf_slot]),
                  lambda r: pltpu.make_async_copy(hext_ref.at[pl.ds(0, 1), :],
                                                  hbuf.at[buf_slot, pl.ds(r, 1), :], gsem.at[buf_slot]),
                  rows_valid)

    def wait_scatter(buf_slot, rows_valid):
        wait_rows(pltpu.make_async_copy(obuf.at[buf_slot], out_ref.at[pl.ds(0, tm), :], ssem.at[buf_slot]),
                  lambda r: pltpu.make_async_copy(obuf.at[buf_slot, pl.ds(r, 1), :],
                                                  out_ref.at[pl.ds(0, 1), :], ssem.at[buf_slot]),
                  rows_valid)

    @pl.when(i == 0)
    def _():
        def invert(c, carry):
            for u in range(MOE_INVERT_UNROLL):
                t = c * MOE_INVERT_UNROLL + u
                src[dest_ref[t]] = t
            return carry

        lax.fori_loop(0, dest_ref.shape[0] // MOE_INVERT_UNROLL, invert, 0)
        hbuf[...] = jnp.zeros_like(hbuf)
        start_rows(functools.partial(gather_copy, 0, 0), nv)

    def tile_body(slot):
        @pl.when(i >= 2)
        def _():
            wait_scatter(slot, nvalid_ref[jnp.maximum(i - 2, 0)])

        wait_gather(slot, nv)
        start_rows(functools.partial(gather_copy, nxt, 1 - slot), nv_next)
        hrows = hbuf[slot]
        h = hrows[:, 0:D_MODEL]
        w_a = hrows[:, D_MODEL:D_MODEL + 1]
        w_b = hrows[:, D_MODEL + 1:D_MODEL + 2]
        ms = jnp.mean(h * h, axis=-1, keepdims=True)
        hn = ((h * lax.rsqrt(ms + EPS)) * gffn_ref[...]).astype(BF16)

        def expert(wg_ref, wu_ref, wd_ref):
            gate = jnp.dot(hn, wg_ref[0], preferred_element_type=F32)
            up = jnp.dot(hn, wu_ref[0], preferred_element_type=F32)
            hidden = ((gate * jax.nn.sigmoid(gate)) * up).astype(BF16)
            return jnp.dot(hidden, wd_ref[0], preferred_element_type=F32)

        y = w_a * expert(wga_ref, wua_ref, wda_ref) + w_b * expert(wgb_ref, wub_ref, wdb_ref)
        h2 = h + y
        ms2 = jnp.mean(h2 * h2, axis=-1, keepdims=True)
        obuf[slot] = (h2 * lax.rsqrt(ms2 + EPS)) * gfin_ref[...]
        start_rows(functools.partial(scatter_copy, i, slot), nv)

        @pl.when(jnp.logical_not(has_next))
        def _():
            @pl.when(i >= 1)
            def _():
                wait_scatter(1 - slot, nvalid_ref[jnp.maximum(i - 1, 0)])

            wait_scatter(slot, nv)

    for parity in range(2):
        pl.when((nv > 0) & (i % 2 == parity))(functools.partial(tile_body, parity))


def _moe_call(dest, nvalid, ea, eb, hext, wg, wu, wd, gffn, gfin):
    t = hext.shape[0]
    tm = MOE_TM
    n_tiles = nvalid.shape[0]
    assert t % MOE_INVERT_UNROLL == 0
    wa_map = lambda i, dest, nv, ea, eb: (ea[i], 0, 0)
    wb_map = lambda i, dest, nv, ea, eb: (eb[i], 0, 0)
    const = lambda i, dest, nv, ea, eb: (0, 0)
    return pl.pallas_call(
        _moe_kernel,
        grid_spec=pltpu.PrefetchScalarGridSpec(
            num_scalar_prefetch=4,
            grid=(n_tiles,),
            in_specs=[
                pl.BlockSpec(memory_space=pl.ANY),
                pl.BlockSpec((1, D_MODEL, D_FF), wa_map),
                pl.BlockSpec((1, D_MODEL, D_FF), wa_map),
                pl.BlockSpec((1, D_FF, D_MODEL), wa_map),
                pl.BlockSpec((1, D_MODEL, D_FF), wb_map),
                pl.BlockSpec((1, D_MODEL, D_FF), wb_map),
                pl.BlockSpec((1, D_FF, D_MODEL), wb_map),
                pl.BlockSpec((1, D_MODEL), const),
                pl.BlockSpec((1, D_MODEL), const),
            ],
            out_specs=pl.BlockSpec(memory_space=pl.ANY),
            scratch_shapes=[
                pltpu.VMEM((2, tm, HEXT_COLS), F32),
                pltpu.VMEM((2, tm, D_MODEL), F32),
                pltpu.SMEM((n_tiles * tm,), jnp.int32),
                pltpu.SemaphoreType.DMA((2,)),
                pltpu.SemaphoreType.DMA((2,)),
            ],
        ),
        out_shape=jax.ShapeDtypeStruct((t, D_MODEL), F32),
        compiler_params=pltpu.CompilerParams(
            dimension_semantics=("arbitrary",), vmem_limit_bytes=VMEM_LIMIT),
        name="experts",
    )(dest, nvalid, ea, eb, hext, wg, wu, wd, wg, wu, wd, gffn, gfin)


def _routing_plan(cls, rank, counts, tm):
    t = cls.shape[0]
    n_tiles = t // tm + N_CLASSES
    tiles_c = (counts + tm - 1) // tm
    tile_end = jnp.cumsum(tiles_c)
    tile_off = tile_end - tiles_c
    onehot = cls[:, None] == jnp.arange(N_CLASSES, dtype=jnp.int32)[None, :]
    dest = jnp.sum(jnp.where(onehot, tile_off[None, :], 0), axis=1) * tm + rank
    tile = jnp.arange(n_tiles, dtype=jnp.int32)
    used = tile < tile_end[-1]
    tile_cls = jnp.minimum(jnp.sum((tile[:, None] >= tile_end[None, :]).astype(jnp.int32), axis=1),
                           N_CLASSES - 1)
    cls_onehot = tile_cls[:, None] == jnp.arange(N_CLASSES, dtype=jnp.int32)[None, :]
    pick = lambda table: jnp.sum(jnp.where(cls_onehot, table[None, :], 0), axis=1)
    nvalid = jnp.where(used, jnp.clip(pick(counts) - (tile - pick(tile_off)) * tm, 0, tm), 0)
    classes = np.arange(N_CLASSES)
    ea_tab = jnp.asarray((classes // PAIRS_PER_GROUP) * EXPERTS_PER_GROUP
                         + np.asarray(_PAIR_A)[classes % PAIRS_PER_GROUP], jnp.int32)
    eb_tab = jnp.asarray((classes // PAIRS_PER_GROUP) * EXPERTS_PER_GROUP
                         + np.asarray(_PAIR_B)[classes % PAIRS_PER_GROUP], jnp.int32)
    last_used = jnp.sum(jnp.where(tile == tile_end[-1] - 1, tile_cls, 0))
    tile_cls = jnp.where(used, tile_cls, last_used)
    cls_onehot = tile_cls[:, None] == jnp.arange(N_CLASSES, dtype=jnp.int32)[None, :]
    return dest.astype(jnp.int32), nvalid.astype(jnp.int32), pick(ea_tab), pick(eb_tab)


def _rope_tables(seq):
    half = RET_QK_DIM // 2
    inv_freq = ROPE_BASE ** (-jnp.arange(half, dtype=F32) / half)
    ang = jnp.arange(seq, dtype=F32)[:, None] * inv_freq[None, :]
    cos, sin = jnp.cos(ang), jnp.sin(ang)
    cos_t = jnp.tile(cos, (1, LANES // half))
    sin_t = jnp.concatenate([-sin, -sin, sin, sin], axis=1)
    return cos_t, sin_t


def kernel(x, norm_mix_g, w_in, w_gate, b_gate, w_sb_out, w_ret_out, ret_norm_g, w_out,
           norm_ffn_g, w_group_router, b_group_router, w_expert_router, b_expert_router,
           w_exp_gate, w_exp_up, w_exp_down, norm_final_g):
    bsz, seq, d = x.shape
    assert d == D_MODEL and w_in.shape[0] == 1 and seq % SB_BLK == 0 and (bsz * seq) % MIX_TM == 0
    t = bsz * seq
    x2 = x.reshape(t, d)
    wi = w_in[0]
    c_rq = 3 * SB_WIDTH
    c_rk = c_rq + RET_QK_WIDTH
    c_rv = c_rk + RET_QK_WIDTH
    c_rg = c_rv + RET_V_WIDTH
    ones = functools.partial(jnp.ones, dtype=F32)
    w_plain = jnp.concatenate([wi[:, :c_rq], wi[:, c_rv:c_rg]], axis=1).astype(BF16)
    s_plain = jnp.concatenate([jnp.full((SB_WIDTH,), SB_HEAD_DIM ** -0.5, F32),
                               ones((2 * SB_WIDTH + RET_V_WIDTH,))])[None, :]
    perm = _rot_perm()
    w_rot = jnp.concatenate([wi[:, c_rq:c_rk][:, perm], wi[:, c_rk:c_rv][:, perm]], axis=1).astype(BF16)
    s_rot = jnp.concatenate([ones((RET_QK_WIDTH,)),
                             jnp.full((RET_QK_WIDTH,), RET_QK_DIM ** -0.5, F32)])[None, :]
    w_act = jnp.concatenate([wi[:, c_rg:], w_gate[0]], axis=1).astype(BF16)
    b_act = jnp.concatenate([jnp.zeros((RET_V_WIDTH,), F32), b_gate[0]])[None, :]
    swish_cols = jnp.concatenate([ones((RET_V_WIDTH,)), jnp.zeros((GATE_COLS,), F32)])
    m_act = jnp.stack([swish_cols, 1.0 - swish_cols])
    cos_t, sin_t = _rope_tables(seq)

    plain, xn = _proj_plain_call(x2, norm_mix_g[0][None, :], w_plain, s_plain, seq)
    rqk = _proj_rot_call(xn, w_rot, s_rot, cos_t, sin_t, seq)
    act = _proj_act_call(xn, w_act, b_act, m_act, seq)
    ysb = _sb_call(plain, seq)
    yret = _ret_call(rqk, plain, act, ret_norm_g[0][None, :], seq)

    wr = jnp.concatenate([w_group_router[0], w_expert_router[0]], axis=1).T
    wr = jnp.pad(wr, ((0, ROUTER_ROWS - wr.shape[0]), (0, 0)))
    wr_hi = wr.astype(BF16)
    wr_lo = (wr - wr_hi.astype(F32)).astype(BF16)
    br = jnp.concatenate([b_group_router[0], b_expert_router[0]])
    br = jnp.pad(br, (0, ROUTER_ROWS - br.shape[0]))[:, None]
    hext, cls8, cnt = _mix_call(ysb, yret, act, x2, w_sb_out[0].astype(BF16), w_ret_out[0].astype(BF16),
                           w_out[0].astype(BF16), norm_ffn_g[0][None, :], wr_hi, wr_lo, br)

    counts = cnt[:N_CLASSES, 0].astype(jnp.int32)
    dest, nvalid, ea, eb = _routing_plan(cls8[0], cls8[1], counts, MOE_TM)
    out = _moe_call(dest, nvalid, ea, eb, hext, w_exp_gate[0].astype(BF16), w_exp_up[0].astype(BF16),
                    w_exp_down[0].astype(BF16), norm_ffn_g[0][None, :], norm_final_g[None, :])
    return out.reshape(bsz, seq, d)
```

```python
import functools
import math

import numpy as np
import jax
import jax.numpy as jnp
from jax import lax
from jax.experimental import pallas as pl
from jax.experimental.pallas import tpu as pltpu

F32 = jnp.float32
BF16 = jnp.bfloat16

D_MODEL = 1024
SB_HEADS = 8
SB_HEAD_DIM = 64
SB_WIDTH = SB_HEADS * SB_HEAD_DIM
RET_HEADS = 8
RET_QK_DIM = 64
RET_V_DIM = 128
RET_QK_WIDTH = RET_HEADS * RET_QK_DIM
RET_V_WIDTH = RET_HEADS * RET_V_DIM
IN_COLS = 3 * SB_WIDTH + 2 * RET_QK_WIDTH + 2 * RET_V_WIDTH
GATE_COLS = 2 * D_MODEL
ALL_COLS = IN_COLS + GATE_COLS
CHUNK = 64
ROPE_BASE = 10000.0
N_GROUPS = 4
EXPERTS_PER_GROUP = 4
N_EXPERTS = N_GROUPS * EXPERTS_PER_GROUP
D_FF = 512
EPS = 1e-6

LANES = 128
PROJ_TN = 1280
GATE_BLK = 512
PROJ_ROWS = 256
SB_BLK = 256
RET_BLK = 256
RET_UNROLL = 2
MIX_TM = 512
MOE_TM = 256
MOE_INVERT_UNROLL = 16
ROUTER_ROWS = 32
PAIRS_PER_GROUP = 6
N_CLASSES = N_GROUPS * PAIRS_PER_GROUP
INFO_COLS = LANES
HEXT_COLS = D_MODEL + INFO_COLS
VMEM_LIMIT = 56 * 1024 * 1024
LOG2E = math.log2(math.e)
SB_SKIP_LOG2 = 156.0

_PAIR_A = (0, 0, 0, 1, 1, 2)
_PAIR_B = (1, 2, 3, 2, 3, 3)


def _proj_plain_kernel(x_ref, g_ref, w_ref, s_ref, o_ref, xn_ref):
    @pl.when(pl.program_id(1) == 0)
    def _():
        x = x_ref[...]
        ms = jnp.mean(x * x, axis=-1, keepdims=True)
        xn_ref[...] = ((x * lax.rsqrt(ms + EPS)) * g_ref[...]).astype(BF16)

    for rows in _row_chunks(xn_ref.shape[0]):
        acc = jnp.dot(xn_ref[rows, :], w_ref[...], preferred_element_type=F32)
        o_ref[rows, :] = (acc * s_ref[...]).astype(BF16)


def _row_chunks(n_rows):
    return [slice(r, r + PROJ_ROWS) for r in range(0, n_rows, PROJ_ROWS)]


def _proj_rot_kernel(xn_ref, w_ref, s_ref, cos_ref, sin_ref, o_ref):
    for rows in _row_chunks(xn_ref.shape[0]):
        acc = jnp.dot(xn_ref[rows, :], w_ref[...], preferred_element_type=F32)
        cos = cos_ref[rows, :]
        sin = sin_ref[rows, :]
        for p in range(w_ref.shape[1] // LANES):
            cols = slice(p * LANES, (p + 1) * LANES)
            seg = acc[:, cols]
            rot = seg * cos + pltpu.roll(seg, LANES // 2, 1) * sin
            o_ref[rows, cols] = (rot * s_ref[:, cols]).astype(BF16)


def _proj_act_kernel(xn_ref, w_ref, b_ref, m_ref, o_ref):
    for rows in _row_chunks(xn_ref.shape[0]):
        acc = jnp.dot(xn_ref[rows, :], w_ref[...], preferred_element_type=F32)
        sig = 0.5 * jnp.tanh(0.5 * (acc + b_ref[...])) + 0.5
        o_ref[rows, :] = (sig * (acc * m_ref[0:1, :] + m_ref[1:2, :])).astype(BF16)


def _proj_params():
    return pltpu.CompilerParams(dimension_semantics=("arbitrary", "arbitrary"),
                                vmem_limit_bytes=VMEM_LIMIT)


def _proj_tile(n):
    tn = PROJ_TN
    while n % tn:
        tn -= LANES
    return tn


def _proj_plain_call(x2, g, w, scale, seq):
    t, n = x2.shape[0], w.shape[1]
    tn = _proj_tile(n)
    return pl.pallas_call(
        _proj_plain_kernel,
        grid=(t // seq, n // tn),
        in_specs=[
            pl.BlockSpec((seq, D_MODEL), lambda i, j: (i, 0)),
            pl.BlockSpec((1, D_MODEL), lambda i, j: (0, 0)),
            pl.BlockSpec((D_MODEL, tn), lambda i, j: (0, j)),
            pl.BlockSpec((1, tn), lambda i, j: (0, j)),
        ],
        out_specs=[
            pl.BlockSpec((seq, tn), lambda i, j: (i, j)),
            pl.BlockSpec((seq, D_MODEL), lambda i, j: (i, 0)),
        ],
        out_shape=[jax.ShapeDtypeStruct((t, n), BF16), jax.ShapeDtypeStruct((t, D_MODEL), BF16)],
        compiler_params=_proj_params(),
        name="proj_plain",
    )(x2, g, w, scale)


def _proj_rot_call(xn, w, scale, cos_t, sin_t, seq):
    t, n = xn.shape[0], w.shape[1]
    tn = _proj_tile(n)
    return pl.pallas_call(
        _proj_rot_kernel,
        grid=(t // seq, n // tn),
        in_specs=[
            pl.BlockSpec((seq, D_MODEL), lambda i, j: (i, 0)),
            pl.BlockSpec((D_MODEL, tn), lambda i, j: (0, j)),
            pl.BlockSpec((1, tn), lambda i, j: (0, j)),
            pl.BlockSpec((seq, LANES), lambda i, j: (0, 0)),
            pl.BlockSpec((seq, LANES), lambda i, j: (0, 0)),
        ],
        out_specs=pl.BlockSpec((seq, tn), lambda i, j: (i, j)),
        out_shape=jax.ShapeDtypeStruct((t, n), BF16),
        compiler_params=_proj_params(),
        name="proj_rotary",
    )(xn, w, scale, cos_t, sin_t)


def _proj_act_call(xn, w, bias, mode, seq):
    t, n = xn.shape[0], w.shape[1]
    tn = _proj_tile(n)
    return pl.pallas_call(
        _proj_act_kernel,
        grid=(t // seq, n // tn),
        in_specs=[
            pl.BlockSpec((seq, D_MODEL), lambda i, j: (i, 0)),
            pl.BlockSpec((D_MODEL, tn), lambda i, j: (0, j)),
            pl.BlockSpec((1, tn), lambda i, j: (0, j)),
            pl.BlockSpec((2, tn), lambda i, j: (0, j)),
        ],
        out_specs=pl.BlockSpec((seq, tn), lambda i, j: (i, j)),
        out_shape=jax.ShapeDtypeStruct((t, n), BF16),
        compiler_params=_proj_params(),
        name="proj_act",
    )(xn, w, bias, mode)


def _rot_perm():
    half = RET_QK_DIM // 2
    order = [p * LANES + hh * RET_QK_DIM + part * half + d
             for p in range(RET_QK_WIDTH // LANES) for part in range(2) for hh in range(2)
             for d in range(half)]
    return np.asarray(order, np.int32)


def _sb_kernel(q_ref, k_ref, v_ref, o_ref, u_ref, acc_ref, car_ref):
    seq = q_ref.shape[0]
    nq = seq // SB_BLK
    row = lax.broadcasted_iota(jnp.int32, (SB_BLK, SB_BLK), 0)
    col = lax.broadcasted_iota(jnp.int32, (SB_BLK, SB_BLK), 1)
    u_ref[...] = (row >= col).astype(BF16)
    lane = lax.broadcasted_iota(jnp.int32, (1, LANES), 1)
    head_masks = (lane < SB_HEAD_DIM, lane >= SB_HEAD_DIM)

    heads = range(2)
    causal = col < row

    def rows_of(blk):
        return pl.ds(pl.multiple_of(blk * SB_BLK, SB_BLK), SB_BLK)

    def q_heads_of(qi):
        q_blk = q_ref[rows_of(qi), :]
        return tuple(jnp.where(m, q_blk, jnp.zeros_like(q_blk)) for m in head_masks)

    def sweep(jobs):
        u = u_ref[...]
        q_heads = [q_heads_of(qi) for _, qi, _, _ in jobs]
        z2 = [[[lax.dot_general(q_heads[j][h], k_ref[rows_of(kb), :], (((1,), (1,)), ((), ())),
                                preferred_element_type=F32) for h in heads]
               for kb, _ in steps] for j, (_, _, steps, _) in enumerate(jobs)]
        sinc = []
        for j, (_, _, steps, _) in enumerate(jobs):
            sinc.append([])
            for i, (_, diagonal) in enumerate(steps):
                nlk = [jnp.maximum(z, 0.0) + jnp.log2(1.0 + jnp.exp2(-jnp.abs(z))) for z in z2[j][i]]
                if diagonal:
                    nlk = [jnp.where(causal, a, 0.0) for a in nlk]
                sinc[j].append([jnp.dot(a.astype(BF16), u, preferred_element_type=F32) for a in nlk])
        for j, (slot, _, steps, fresh) in enumerate(jobs):
            car = [None, None] if fresh else [car_ref[slot, h] for h in heads]
            contrib = None
            for i, (kb, diagonal) in enumerate(steps):
                v_blk = v_ref[rows_of(kb), :]
                for h in heads:
                    e = z2[j][i][h] - sinc[j][i][h]
                    if car[h] is not None:
                        e = e - jnp.concatenate([car[h], car[h]], axis=1)
                    w = jnp.exp2(e)
                    if diagonal:
                        w = jnp.where(causal, w, 0.0)
                    vh = jnp.where(head_masks[h], v_blk, jnp.zeros_like(v_blk))
                    pv = jnp.dot(w.astype(BF16), vh, preferred_element_type=F32)
                    contrib = pv if contrib is None else contrib + pv
                    tot = jnp.broadcast_to(sinc[j][i][h][:, 0:1], (SB_BLK, LANES))
                    car[h] = tot if car[h] is None else car[h] + tot
            for h in heads:
                car_ref[slot, h] = car[h]
            if fresh:
                acc_ref[slot] = contrib
            else:
                acc_ref[slot] += contrib

    def finish(slot, qi, first_kb):
        if first_kb is not None:
            def more(c):
                kb, min_carry = c
                return (kb >= 0) & (min_carry < SB_SKIP_LOG2)

            def k_step(c):
                kb, _ = c
                sweep([(slot, qi, [(kb, False)], False)])
                return kb - 1, jnp.min(car_ref[slot])

            lax.while_loop(more, k_step, (first_kb, jnp.min(car_ref[slot])))
        o_ref[rows_of(qi), :] = acc_ref[slot].astype(BF16)

    sweep([(0, 0, [(0, True)], True), (1, 1, [(1, True), (0, False)], True)])
    finish(0, 0, None)
    finish(1, 1, None)

    def q_pair(p, carry):
        qa = 2 * p
        qb = qa + 1
        sweep([(0, qa, [(qa, True), (qa - 1, False)], True),
               (1, qb, [(qb, True), (qa, False)], True)])
        finish(0, qa, qa - 2)
        finish(1, qb, qa - 1)
        return carry

    lax.fori_loop(1, nq // 2, q_pair, 0)


def _sb_call(proj, seq):
    t = proj.shape[0]
    n_pairs = SB_WIDTH // LANES
    return pl.pallas_call(
        _sb_kernel,
        grid=(t // seq, n_pairs),
        in_specs=[
            pl.BlockSpec((seq, LANES), lambda b, p: (b, p)),
            pl.BlockSpec((seq, LANES), lambda b, p: (b, n_pairs + p)),
            pl.BlockSpec((seq, LANES), lambda b, p: (b, 2 * n_pairs + p)),
        ],
        out_specs=pl.BlockSpec((seq, LANES), lambda b, p: (b, p)),
        out_shape=jax.ShapeDtypeStruct((t, SB_WIDTH), BF16),
        scratch_shapes=[
            pltpu.VMEM((SB_BLK, SB_BLK), BF16),
            pltpu.VMEM((2, SB_BLK, LANES), F32),
            pltpu.VMEM((2, 2, SB_BLK, LANES), F32),
        ],
        compiler_params=pltpu.CompilerParams(
            dimension_semantics=("arbitrary", "arbitrary"), vmem_limit_bytes=VMEM_LIMIT),
        name="stickbreak",
    )(proj, proj, proj)


def _ret_tables():
    h = np.arange(RET_HEADS, dtype=np.float64)
    log_gamma = np.log(1.0 - 2.0 ** (-5.0 - h))
    idx = np.arange(RET_BLK, dtype=np.float64)
    t, s = idx[:, None], idx[None, :]
    same = (t // CHUNK) == (s // CHUNK)
    earlier = (s // CHUNK) < (t // CHUNK)
    expo = np.where(same, np.abs(t - s), np.where(earlier, t - s, 0.0))
    dmat = np.exp(log_gamma[:, None, None] * expo) * (same | earlier)
    qdec = np.exp(log_gamma[:, None] * (idx + 1.0)[None, :])
    kdec = np.exp(log_gamma[:, None] * (RET_BLK - 1.0 - idx)[None, :])
    cdec = np.exp(log_gamma * RET_BLK)
    rep = lambda a: np.broadcast_to(a[..., None], a.shape + (LANES,))
    shp = (RET_HEADS // 2, 2)
    return (jnp.asarray(dmat.reshape(shp + (RET_BLK, RET_BLK)), F32),
            jnp.asarray(rep(qdec).reshape(shp + (RET_BLK, LANES)), F32),
            jnp.asarray(rep(kdec).reshape(shp + (RET_BLK, LANES)), F32),
            jnp.asarray(rep(cdec[:, None]).reshape(shp + (1, LANES)), F32))


def _ret_kernel(q_ref, k_ref, v_ref, g_ref, gn_ref, dm_ref, qd_ref, kd_ref, cd_ref, o_ref, st_ref):
    seq = q_ref.shape[0]
    lane = lax.broadcasted_iota(jnp.int32, (1, LANES), 1)
    head0 = (lane % RET_QK_DIM) < (RET_QK_DIM // 2)
    head_masks = (head0, jnp.logical_not(head0))
    heads = range(2)
    nt = (((1,), (1,)), ((), ()))
    tn = (((0,), (0,)), ((), ()))
    st_ref[...] = jnp.zeros_like(st_ref)

    def step(n, carry):
        subs = range(RET_UNROLL)
        rows = [pl.ds(pl.multiple_of((n * RET_UNROLL + u) * RET_BLK, RET_BLK), RET_BLK) for u in subs]
        col = [slice(h * RET_V_DIM, (h + 1) * RET_V_DIM) for h in heads]
        q_blk = [q_ref[r, :] for r in rows]
        k_blk = [k_ref[r, :] for r in rows]
        qm = [[jnp.where(head_masks[h], q_blk[u], jnp.zeros_like(q_blk[u])) for h in heads] for u in subs]
        km = [[jnp.where(head_masks[h], k_blk[u], jnp.zeros_like(k_blk[u])) for h in heads] for u in subs]
        v = [[v_ref[rows[u], col[h]] for h in heads] for u in subs]
        scores = [[lax.dot_general(qm[u][h], k_blk[u], nt, preferred_element_type=F32)
                   for h in heads] for u in subs]
        kd = [[(km[u][h].astype(F32) * kd_ref[0, h]).astype(BF16) for h in heads] for u in subs]
        kv = [[lax.dot_general(kd[u][h], v[u][h], tn, preferred_element_type=F32)
               for h in heads] for u in subs]
        qd = [[(qm[u][h].astype(F32) * qd_ref[0, h]).astype(BF16) for h in heads] for u in subs]
        p = [[(scores[u][h] * dm_ref[0, h]).astype(BF16) for h in heads] for u in subs]
        intra = [[jnp.dot(p[u][h], v[u][h], preferred_element_type=F32) for h in heads] for u in subs]
        state = [st_ref[h] for h in heads]
        for u in subs:
            for h in heads:
                y = intra[u][h] + jnp.dot(qd[u][h], state[h].astype(BF16), preferred_element_type=F32)
                state[h] = state[h] * cd_ref[0, h] + kv[u][h]
                ms = jnp.mean(y * y, axis=-1, keepdims=True)
                yn = (y * lax.rsqrt(ms + EPS)) * gn_ref[:, col[h]]
                o_ref[rows[u], col[h]] = (g_ref[rows[u], col[h]].astype(F32) * yn).astype(BF16)
        for h in heads:
            st_ref[h] = state[h]
        return carry

    lax.fori_loop(0, seq // (RET_BLK * RET_UNROLL), step, 0)


def _ret_call(rqk, plain, act, ret_norm_g, seq):
    t = rqk.shape[0]
    n_pairs = RET_HEADS // 2
    dmat, qdec, kdec, cdec = _ret_tables()
    k0 = RET_QK_WIDTH // LANES
    v0 = (3 * SB_WIDTH) // (2 * RET_V_DIM)
    return pl.pallas_call(
        _ret_kernel,
        grid=(t // seq, n_pairs),
        in_specs=[
            pl.BlockSpec((seq, LANES), lambda b, p: (b, p)),
            pl.BlockSpec((seq, LANES), lambda b, p: (b, k0 + p)),
            pl.BlockSpec((seq, 2 * RET_V_DIM), lambda b, p: (b, v0 + p)),
            pl.BlockSpec((seq, 2 * RET_V_DIM), lambda b, p: (b, p)),
            pl.BlockSpec((1, 2 * RET_V_DIM), lambda b, p: (0, p)),
            pl.BlockSpec((1, 2, RET_BLK, RET_BLK), lambda b, p: (p, 0, 0, 0)),
            pl.BlockSpec((1, 2, RET_BLK, LANES), lambda b, p: (p, 0, 0, 0)),
            pl.BlockSpec((1, 2, RET_BLK, LANES), lambda b, p: (p, 0, 0, 0)),
            pl.BlockSpec((1, 2, 1, LANES), lambda b, p: (p, 0, 0, 0)),
        ],
        out_specs=pl.BlockSpec((seq, 2 * RET_V_DIM), lambda b, p: (b, p)),
        out_shape=jax.ShapeDtypeStruct((t, RET_V_WIDTH), BF16),
        scratch_shapes=[pltpu.VMEM((2, LANES, RET_V_DIM), F32)],
        compiler_params=pltpu.CompilerParams(
            dimension_semantics=("arbitrary", "arbitrary"), vmem_limit_bytes=VMEM_LIMIT),
        name="retention",
    )(rqk, rqk, plain, act, ret_norm_g, dmat, qdec, kdec, cdec)


def _mix_kernel(ysb_ref, yret_ref, gs0_ref, gs1_ref, gr0_ref, gr1_ref, x_ref,
                wsb_ref, wret_ref, wout_ref, gffn_ref, wrh_ref, wrl_ref, br_ref, tri_ref,
                hext_ref, cls_ref, cnt_ref):
    tm = x_ref.shape[0]

    @pl.when(pl.program_id(0) == 0)
    def _():
        cnt_ref[...] = jnp.zeros_like(cnt_ref)

    a = jnp.dot(ysb_ref[...], wsb_ref[...], preferred_element_type=F32)
    b = jnp.dot(yret_ref[...], wret_ref[...], preferred_element_type=F32)
    g_sb = jnp.concatenate([gs0_ref[...], gs1_ref[...]], axis=1).astype(F32)
    g_ret = jnp.concatenate([gr0_ref[...], gr1_ref[...]], axis=1).astype(F32)
    mixed = (g_sb * a + g_ret * b).astype(BF16)
    h = x_ref[...] + jnp.dot(mixed, wout_ref[...], preferred_element_type=F32)
    hext_ref[:, 0:D_MODEL] = h

    ms = jnp.mean(h * h, axis=-1, keepdims=True)
    hn = (h * lax.rsqrt(ms + EPS)) * gffn_ref[...]
    hn_hi = hn.astype(BF16)
    hn_lo = (hn - hn_hi.astype(F32)).astype(BF16)
    nt = (((1,), (1,)), ((), ()))
    logits = (lax.dot_general(wrh_ref[...], hn_hi, nt, preferred_element_type=F32)
              + lax.dot_general(wrh_ref[...], hn_lo, nt, preferred_element_type=F32)
              + lax.dot_general(wrl_ref[...], hn_hi, nt, preferred_element_type=F32)
              + br_ref[...])

    def first_argmax(vals):
        m = functools.reduce(jnp.maximum, vals)
        idx = jnp.full(m.shape, len(vals) - 1, jnp.int32)
        for i in range(len(vals) - 2, -1, -1):
            idx = jnp.where(vals[i] >= m, i, idx)
        return m, idx

    gl = [logits[r:r + 1, :] for r in range(N_GROUPS)]
    gmax, gsel = first_argmax(gl)
    p_group = 1.0 / functools.reduce(lambda s, v: s + v, [jnp.exp(v - gmax) for v in gl])
    el = []
    for e in range(EXPERTS_PER_GROUP):
        v = logits[N_GROUPS + 3 * EXPERTS_PER_GROUP + e:N_GROUPS + 3 * EXPERTS_PER_GROUP + e + 1, :]
        for g in range(N_GROUPS - 2, -1, -1):
            r = N_GROUPS + g * EXPERTS_PER_GROUP + e
            v = jnp.where(gsel == g, logits[r:r + 1, :], v)
        el.append(v)
    m1, i1 = first_argmax(el)
    rest = [jnp.where(i1 == e, -jnp.inf, el[e]) for e in range(EXPERTS_PER_GROUP)]
    m2, i2 = first_argmax(rest)
    tt = jnp.exp(m2 - m1)
    gate1 = p_group * (1.0 / (1.0 + tt))
    gate2 = p_group * (tt / (1.0 + tt))
    lo_first = i1 < i2
    ea = jnp.minimum(i1, i2)
    eb = jnp.maximum(i1, i2)
    w_a = jnp.where(lo_first, gate1, gate2)
    w_b = jnp.where(lo_first, gate2, gate1)
    pid = jnp.where(ea == 0, eb - 1, jnp.where(ea == 1, eb + 1, PAIRS_PER_GROUP - 1))
    cls = gsel * PAIRS_PER_GROUP + pid
    info = jnp.concatenate([w_a, w_b, jnp.zeros((INFO_COLS - 2, tm), F32)], axis=0)
    hext_ref[:, D_MODEL:HEXT_COLS] = info.T

    class_row = lax.broadcasted_iota(jnp.int32, (ROUTER_ROWS, tm), 0)
    onehot = class_row == cls
    onehot_bf = onehot.astype(BF16)
    before = jnp.dot(onehot_bf, tri_ref[...], preferred_element_type=F32)
    cnt = cnt_ref[...]
    seen = before + jnp.concatenate([cnt] * (tm // LANES), axis=1)
    rank = jnp.sum(jnp.where(onehot, seen, 0.0), axis=0, keepdims=True)
    cnt_ref[...] = cnt + jnp.dot(onehot_bf, jnp.ones((tm, LANES), BF16), preferred_element_type=F32)
    cls_ref[...] = jnp.concatenate(
        [cls, rank.astype(jnp.int32), jnp.zeros((cls_ref.shape[0] - 2, tm), jnp.int32)], axis=0)


def _mix_call(ysb, yret, act, x2, wsb, wret, wout, gffn, wr_hi, wr_lo, br):
    t = x2.shape[0]
    tm = MIX_TM
    gate0 = RET_V_WIDTH // GATE_BLK
    const = lambda i: (0, 0)
    idx = jnp.arange(tm, dtype=jnp.int32)
    tri = (idx[:, None] < idx[None, :]).astype(BF16)
    return pl.pallas_call(
        _mix_kernel,
        grid=(t // tm,),
        in_specs=[
            pl.BlockSpec((tm, SB_WIDTH), lambda i: (i, 0)),
            pl.BlockSpec((tm, RET_V_WIDTH), lambda i: (i, 0)),
            pl.BlockSpec((tm, GATE_BLK), lambda i: (i, gate0)),
            pl.BlockSpec((tm, GATE_BLK), lambda i: (i, gate0 + 1)),
            pl.BlockSpec((tm, GATE_BLK), lambda i: (i, gate0 + 2)),
            pl.BlockSpec((tm, GATE_BLK), lambda i: (i, gate0 + 3)),
            pl.BlockSpec((tm, D_MODEL), lambda i: (i, 0)),
            pl.BlockSpec((SB_WIDTH, D_MODEL), const),
            pl.BlockSpec((RET_V_WIDTH, D_MODEL), const),
            pl.BlockSpec((D_MODEL, D_MODEL), const),
            pl.BlockSpec((1, D_MODEL), const),
            pl.BlockSpec((ROUTER_ROWS, D_MODEL), const),
            pl.BlockSpec((ROUTER_ROWS, D_MODEL), const),
            pl.BlockSpec((ROUTER_ROWS, 1), const),
            pl.BlockSpec((tm, tm), const),
        ],
        out_specs=[
            pl.BlockSpec((tm, HEXT_COLS), lambda i: (i, 0)),
            pl.BlockSpec((8, tm), lambda i: (0, i)),
            pl.BlockSpec((ROUTER_ROWS, LANES), const),
        ],
        out_shape=[
            jax.ShapeDtypeStruct((t, HEXT_COLS), F32),
            jax.ShapeDtypeStruct((8, t), jnp.int32),
            jax.ShapeDtypeStruct((ROUTER_ROWS, LANES), F32),
        ],
        compiler_params=pltpu.CompilerParams(
            dimension_semantics=("arbitrary",), vmem_limit_bytes=VMEM_LIMIT),
        name="mix_router",
    )(ysb, yret, act, act, act, act, x2, wsb, wret, wout, gffn, wr_hi, wr_lo, br, tri)


def _moe_kernel(dest_ref, nvalid_ref, ea_ref, eb_ref,
                hext_ref, wga_ref, wua_ref, wda_ref, wgb_ref, wub_ref, wdb_ref,
                gffn_ref, gfin_ref, out_ref, hbuf, obuf, src, gsem, ssem):
    i = pl.program_id(0)
    n_tiles = pl.num_programs(0)
    tm = hbuf.shape[1]
    nv = nvalid_ref[i]
    nxt = jnp.minimum(i + 1, n_tiles - 1)
    nv_next = jnp.where(i + 1 < n_tiles, nvalid_ref[nxt], 0)
    has_next = nv_next > 0

    def gather_copy(tile, buf_slot, r):
        tok = src[tile * tm + r]
        return pltpu.make_async_copy(hext_ref.at[pl.ds(tok, 1), :],
                                     hbuf.at[buf_slot, pl.ds(r, 1), :], gsem.at[buf_slot])

    def scatter_copy(tile, buf_slot, r):
        tok = src[tile * tm + r]
        return pltpu.make_async_copy(obuf.at[buf_slot, pl.ds(r, 1), :],
                                     out_ref.at[pl.ds(tok, 1), :], ssem.at[buf_slot])

    def start_rows(copy_of_row, rows_valid):
        for r in range(tm):
            @pl.when(r < rows_valid)
            def _():
                copy_of_row(r).start(priority=r % 2)

    def wait_rows(full_copy, row_copy, rows_valid):
        @pl.when(rows_valid == tm)
        def _():
            full_copy.wait()

        @pl.when(rows_valid < tm)
        def _():
            for r in range(tm):
                @pl.when(r < rows_valid)
                def _():
                    row_copy(r).wait()

    def wait_gather(buf_slot, rows_valid):
        wait_rows(pltpu.make_async_copy(hext_ref.at[pl.ds(0, tm), :], hbuf.at[buf_slot], gsem.at[buf_slot]),
                  lambda r: pltpu.make_async_copy(hext_ref.at[pl.ds(0, 1), :],
                                                  hbuf.at[buf_slot, pl.ds(r, 1), :], gsem.at[buf_slot]),
                  rows_valid)

    def wait_scatter(buf_slot, rows_valid):
        wait_rows(pltpu.make_async_copy(obuf.at[buf_slot], out_ref.at[pl.ds(0, tm), :], ssem.at[buf_slot]),
                  lambda r: pltpu.make_async_copy(obuf.at[buf_slot, pl.ds(r, 1), :],
                                                  out_ref.at[pl.ds(0, 1), :], ssem.at[buf_slot]),
                  rows_valid)

    @pl.when(i == 0)
    def _():
        def invert(c, carry):
            for u in range(MOE_INVERT_UNROLL):
                t = c * MOE_INVERT_UNROLL + u
                src[dest_ref[t]] = t
            return carry

        lax.fori_loop(0, dest_ref.shape[0] // MOE_INVERT_UNROLL, invert, 0)
        hbuf[...] = jnp.zeros_like(hbuf)
        start_rows(functools.partial(gather_copy, 0, 0), nv)

    def tile_body(slot):
        @pl.when(i >= 2)
        def _():
            wait_scatter(slot, nvalid_ref[jnp.maximum(i - 2, 0)])

        wait_gather(slot, nv)
        start_rows(functools.partial(gather_copy, nxt, 1 - slot), nv_next)
        hrows = hbuf[slot]
        h = hrows[:, 0:D_MODEL]
        w_a = hrows[:, D_MODEL:D_MODEL + 1]
        w_b = hrows[:, D_MODEL + 1:D_MODEL + 2]
        ms = jnp.mean(h * h, axis=-1, keepdims=True)
        hn = ((h * lax.rsqrt(ms + EPS)) * gffn_ref[...]).astype(BF16)

        def expert(wg_ref, wu_ref, wd_ref):
            gate = jnp.dot(hn, wg_ref[0], preferred_element_type=F32)
            up = jnp.dot(hn, wu_ref[0], preferred_element_type=F32)
            hidden = ((gate * jax.nn.sigmoid(gate)) * up).astype(BF16)
            return jnp.dot(hidden, wd_ref[0], preferred_element_type=F32)

        y = w_a * expert(wga_ref, wua_ref, wda_ref) + w_b * expert(wgb_ref, wub_ref, wdb_ref)
        h2 = h + y
        ms2 = jnp.mean(h2 * h2, axis=-1, keepdims=True)
        obuf[slot] = (h2 * lax.rsqrt(ms2 + EPS)) * gfin_ref[...]
        start_rows(functools.partial(scatter_copy, i, slot), nv)

        @pl.when(jnp.logical_not(has_next))
        def _():
            @pl.when(i >= 1)
            def _():
                wait_scatter(1 - slot, nvalid_ref[jnp.maximum(i - 1, 0)])

            wait_scatter(slot, nv)

    for parity in range(2):
        pl.when((nv > 0) & (i % 2 == parity))(functools.partial(tile_body, parity))


def _moe_call(dest, nvalid, ea, eb, hext, wg, wu, wd, gffn, gfin):
    t = hext.shape[0]
    tm = MOE_TM
    n_tiles = nvalid.shape[0]
    assert t % MOE_INVERT_UNROLL == 0
    wa_map = lambda i, dest, nv, ea, eb: (ea[i], 0, 0)
    wb_map = lambda i, dest, nv, ea, eb: (eb[i], 0, 0)
    const = lambda i, dest, nv, ea, eb: (0, 0)
    return pl.pallas_call(
        _moe_kernel,
        grid_spec=pltpu.PrefetchScalarGridSpec(
            num_scalar_prefetch=4,
            grid=(n_tiles,),
            in_specs=[
                pl.BlockSpec(memory_space=pl.ANY),
                pl.BlockSpec((1, D_MODEL, D_FF), wa_map),
                pl.BlockSpec((1, D_MODEL, D_FF), wa_map),
                pl.BlockSpec((1, D_FF, D_MODEL), wa_map),
                pl.BlockSpec((1, D_MODEL, D_FF), wb_map),
                pl.BlockSpec((1, D_MODEL, D_FF), wb_map),
                pl.BlockSpec((1, D_FF, D_MODEL), wb_map),
                pl.BlockSpec((1, D_MODEL), const),
                pl.BlockSpec((1, D_MODEL), const),
            ],
            out_specs=pl.BlockSpec(memory_space=pl.ANY),
            scratch_shapes=[
                pltpu.VMEM((2, tm, HEXT_COLS), F32),
                pltpu.VMEM((2, tm, D_MODEL), F32),
                pltpu.SMEM((n_tiles * tm,), jnp.int32),
                pltpu.SemaphoreType.DMA((2,)),
                pltpu.SemaphoreType.DMA((2,)),
            ],
        ),
        out_shape=jax.ShapeDtypeStruct((t, D_MODEL), F32),
        compiler_params=pltpu.CompilerParams(
            dimension_semantics=("arbitrary",), vmem_limit_bytes=VMEM_LIMIT),
        name="experts",
    )(dest, nvalid, ea, eb, hext, wg, wu, wd, wg, wu, wd, gffn, gfin)


def _routing_plan(cls, rank, counts, tm):
    t = cls.shape[0]
    n_tiles = t // tm + N_CLASSES
    tiles_c = (counts + tm - 1) // tm
    tile_end = jnp.cumsum(tiles_c)
    tile_off = tile_end - tiles_c
    onehot = cls[:, None] == jnp.arange(N_CLASSES, dtype=jnp.int32)[None, :]
    dest = jnp.sum(jnp.where(onehot, tile_off[None, :], 0), axis=1) * tm + rank
    tile = jnp.arange(n_tiles, dtype=jnp.int32)
    used = tile < tile_end[-1]
    tile_cls = jnp.minimum(jnp.sum((tile[:, None] >= tile_end[None, :]).astype(jnp.int32), axis=1),
                           N_CLASSES - 1)
    cls_onehot = tile_cls[:, None] == jnp.arange(N_CLASSES, dtype=jnp.int32)[None, :]
    pick = lambda table: jnp.sum(jnp.where(cls_onehot, table[None, :], 0), axis=1)
    nvalid = jnp.where(used, jnp.clip(pick(counts) - (tile - pick(tile_off)) * tm, 0, tm), 0)
    classes = np.arange(N_CLASSES)
    ea_tab = jnp.asarray((classes // PAIRS_PER_GROUP) * EXPERTS_PER_GROUP
                         + np.asarray(_PAIR_A)[classes % PAIRS_PER_GROUP], jnp.int32)
    eb_tab = jnp.asarray((classes // PAIRS_PER_GROUP) * EXPERTS_PER_GROUP
                         + np.asarray(_PAIR_B)[classes % PAIRS_PER_GROUP], jnp.int32)
    last_used = jnp.sum(jnp.where(tile == tile_end[-1] - 1, tile_cls, 0))
    tile_cls = jnp.where(used, tile_cls, last_used)
    cls_onehot = tile_cls[:, None] == jnp.arange(N_CLASSES, dtype=jnp.int32)[None, :]
    return dest.astype(jnp.int32), nvalid.astype(jnp.int32), pick(ea_tab), pick(eb_tab)


def _rope_tables(seq):
    half = RET_QK_DIM // 2
    inv_freq = ROPE_BASE ** (-jnp.arange(half, dtype=F32) / half)
    ang = jnp.arange(seq, dtype=F32)[:, None] * inv_freq[None, :]
    cos, sin = jnp.cos(ang), jnp.sin(ang)
    cos_t = jnp.tile(cos, (1, LANES // half))
    sin_t = jnp.concatenate([-sin, -sin, sin, sin], axis=1)
    return cos_t, sin_t


def kernel(x, norm_mix_g, w_in, w_gate, b_gate, w_sb_out, w_ret_out, ret_norm_g, w_out,
           norm_ffn_g, w_group_router, b_group_router, w_expert_router, b_expert_router,
           w_exp_gate, w_exp_up, w_exp_down, norm_final_g):
    bsz, seq, d = x.shape
    assert d == D_MODEL and w_in.shape[0] == 1 and seq % (2 * SB_BLK) == 0 and (bsz * seq) % MIX_TM == 0
    t = bsz * seq
    x2 = x.reshape(t, d)
    wi = w_in[0]
    c_rq = 3 * SB_WIDTH
    c_rk = c_rq + RET_QK_WIDTH
    c_rv = c_rk + RET_QK_WIDTH
    c_rg = c_rv + RET_V_WIDTH
    ones = functools.partial(jnp.ones, dtype=F32)
    w_plain = jnp.concatenate([wi[:, :c_rq], wi[:, c_rv:c_rg]], axis=1).astype(BF16)
    s_plain = jnp.concatenate([jnp.full((SB_WIDTH,), SB_HEAD_DIM ** -0.5 * LOG2E, F32),
                               ones((2 * SB_WIDTH + RET_V_WIDTH,))])[None, :]
    perm = _rot_perm()
    w_rot = jnp.concatenate([wi[:, c_rq:c_rk][:, perm], wi[:, c_rk:c_rv][:, perm]], axis=1).astype(BF16)
    s_rot = jnp.concatenate([ones((RET_QK_WIDTH,)),
                             jnp.full((RET_QK_WIDTH,), RET_QK_DIM ** -0.5, F32)])[None, :]
    w_act = jnp.concatenate([wi[:, c_rg:], w_gate[0]], axis=1).astype(BF16)
    b_act = jnp.concatenate([jnp.zeros((RET_V_WIDTH,), F32), b_gate[0]])[None, :]
    swish_cols = jnp.concatenate([ones((RET_V_WIDTH,)), jnp.zeros((GATE_COLS,), F32)])
    m_act = jnp.stack([swish_cols, 1.0 - swish_cols])
    cos_t, sin_t = _rope_tables(seq)

    plain, xn = _proj_plain_call(x2, norm_mix_g[0][None, :], w_plain, s_plain, seq)
    rqk = _proj_rot_call(xn, w_rot, s_rot, cos_t, sin_t, seq)
    act = _proj_act_call(xn, w_act, b_act, m_act, seq)
    ysb = _sb_call(plain, seq)
    yret = _ret_call(rqk, plain, act, ret_norm_g[0][None, :], seq)

    wr = jnp.concatenate([w_group_router[0], w_expert_router[0]], axis=1).T
    wr = jnp.pad(wr, ((0, ROUTER_ROWS - wr.shape[0]), (0, 0)))
    wr_hi = wr.astype(BF16)
    wr_lo = (wr - wr_hi.astype(F32)).astype(BF16)
    br = jnp.concatenate([b_group_router[0], b_expert_router[0]])
    br = jnp.pad(br, (0, ROUTER_ROWS - br.shape[0]))[:, None]
    hext, cls8, cnt = _mix_call(ysb, yret, act, x2, w_sb_out[0].astype(BF16), w_ret_out[0].astype(BF16),
                           w_out[0].astype(BF16), norm_ffn_g[0][None, :], wr_hi, wr_lo, br)

    counts = cnt[:N_CLASSES, 0].astype(jnp.int32)
    dest, nvalid, ea, eb = _routing_plan(cls8[0], cls8[1], counts, MOE_TM)
    out = _moe_call(dest, nvalid, ea, eb, hext, w_exp_gate[0].astype(BF16), w_exp_up[0].astype(BF16),
                    w_exp_down[0].astype(BF16), norm_ffn_g[0][None, :], norm_final_g[None, :])
    return out.reshape(bsz, seq, d)
```

```python
import functools
import math

import numpy as np
import jax
import jax.numpy as jnp
from jax import lax
from jax.experimental import pallas as pl
from jax.experimental.pallas import tpu as pltpu

F32 = jnp.float32
BF16 = jnp.bfloat16

D_MODEL = 1024
SB_HEADS = 8
SB_HEAD_DIM = 64
SB_WIDTH = SB_HEADS * SB_HEAD_DIM
RET_HEADS = 8
RET_QK_DIM = 64
RET_V_DIM = 128
RET_QK_WIDTH = RET_HEADS * RET_QK_DIM
RET_V_WIDTH = RET_HEADS * RET_V_DIM
IN_COLS = 3 * SB_WIDTH + 2 * RET_QK_WIDTH + 2 * RET_V_WIDTH
GATE_COLS = 2 * D_MODEL
ALL_COLS = IN_COLS + GATE_COLS
CHUNK = 64
ROPE_BASE = 10000.0
N_GROUPS = 4
EXPERTS_PER_GROUP = 4
N_EXPERTS = N_GROUPS * EXPERTS_PER_GROUP
D_FF = 512
EPS = 1e-6

LANES = 128
PROJ_TN = 1280
GATE_BLK = 512
PROJ_ROWS = 256
SB_BLK = 256
SB_GROUP = 4
RET_BLK = 256
RET_UNROLL = 4
MIX_TM = 512
MOE_TM = 256
MOE_INVERT_UNROLL = 16
ROUTER_ROWS = 32
PAIRS_PER_GROUP = 6
N_CLASSES = N_GROUPS * PAIRS_PER_GROUP
INFO_COLS = LANES
HEXT_COLS = D_MODEL + INFO_COLS
VMEM_LIMIT = 56 * 1024 * 1024
LOG2E = math.log2(math.e)
SB_SKIP_LOG2 = 156.0

_PAIR_A = (0, 0, 0, 1, 1, 2)
_PAIR_B = (1, 2, 3, 2, 3, 3)


def _proj_plain_kernel(x_ref, g_ref, w_ref, s_ref, o_ref, xn_ref):
    @pl.when(pl.program_id(1) == 0)
    def _():
        x = x_ref[...]
        ms = jnp.mean(x * x, axis=-1, keepdims=True)
        xn_ref[...] = ((x * lax.rsqrt(ms + EPS)) * g_ref[...]).astype(BF16)

    for rows in _row_chunks(xn_ref.shape[0]):
        acc = jnp.dot(xn_ref[rows, :], w_ref[...], preferred_element_type=F32)
        o_ref[rows, :] = (acc * s_ref[...]).astype(BF16)


def _row_chunks(n_rows):
    return [slice(r, r + PROJ_ROWS) for r in range(0, n_rows, PROJ_ROWS)]


def _proj_rot_kernel(xn_ref, w_ref, s_ref, cos_ref, sin_ref, o_ref):
    for rows in _row_chunks(xn_ref.shape[0]):
        acc = jnp.dot(xn_ref[rows, :], w_ref[...], preferred_element_type=F32)
        cos = cos_ref[rows, :]
        sin = sin_ref[rows, :]
        for p in range(w_ref.shape[1] // LANES):
            cols = slice(p * LANES, (p + 1) * LANES)
            seg = acc[:, cols]
            rot = seg * cos + pltpu.roll(seg, LANES // 2, 1) * sin
            o_ref[rows, cols] = (rot * s_ref[:, cols]).astype(BF16)


def _proj_act_kernel(xn_ref, w_ref, b_ref, m_ref, o_ref):
    for rows in _row_chunks(xn_ref.shape[0]):
        acc = jnp.dot(xn_ref[rows, :], w_ref[...], preferred_element_type=F32)
        sig = 0.5 * jnp.tanh(0.5 * (acc + b_ref[...])) + 0.5
        o_ref[rows, :] = (sig * (acc * m_ref[0:1, :] + m_ref[1:2, :])).astype(BF16)


def _proj_params():
    return pltpu.CompilerParams(dimension_semantics=("arbitrary", "arbitrary"),
                                vmem_limit_bytes=VMEM_LIMIT)


def _proj_tile(n):
    tn = PROJ_TN
    while n % tn:
        tn -= LANES
    return tn


def _proj_plain_call(x2, g, w, scale, seq):
    t, n = x2.shape[0], w.shape[1]
    tn = _proj_tile(n)
    return pl.pallas_call(
        _proj_plain_kernel,
        grid=(t // seq, n // tn),
        in_specs=[
            pl.BlockSpec((seq, D_MODEL), lambda i, j: (i, 0)),
            pl.BlockSpec((1, D_MODEL), lambda i, j: (0, 0)),
            pl.BlockSpec((D_MODEL, tn), lambda i, j: (0, j)),
            pl.BlockSpec((1, tn), lambda i, j: (0, j)),
        ],
        out_specs=[
            pl.BlockSpec((seq, tn), lambda i, j: (i, j)),
            pl.BlockSpec((seq, D_MODEL), lambda i, j: (i, 0)),
        ],
        out_shape=[jax.ShapeDtypeStruct((t, n), BF16), jax.ShapeDtypeStruct((t, D_MODEL), BF16)],
        compiler_params=_proj_params(),
        name="proj_plain",
    )(x2, g, w, scale)


def _proj_rot_call(xn, w, scale, cos_t, sin_t, seq):
    t, n = xn.shape[0], w.shape[1]
    tn = _proj_tile(n)
    return pl.pallas_call(
        _proj_rot_kernel,
        grid=(t // seq, n // tn),
        in_specs=[
            pl.BlockSpec((seq, D_MODEL), lambda i, j: (i, 0)),
            pl.BlockSpec((D_MODEL, tn), lambda i, j: (0, j)),
            pl.BlockSpec((1, tn), lambda i, j: (0, j)),
            pl.BlockSpec((seq, LANES), lambda i, j: (0, 0)),
            pl.BlockSpec((seq, LANES), lambda i, j: (0, 0)),
        ],
        out_specs=pl.BlockSpec((seq, tn), lambda i, j: (i, j)),
        out_shape=jax.ShapeDtypeStruct((t, n), BF16),
        compiler_params=_proj_params(),
        name="proj_rotary",
    )(xn, w, scale, cos_t, sin_t)


def _proj_act_call(xn, w, bias, mode, seq):
    t, n = xn.shape[0], w.shape[1]
    tn = _proj_tile(n)
    return pl.pallas_call(
        _proj_act_kernel,
        grid=(t // seq, n // tn),
        in_specs=[
            pl.BlockSpec((seq, D_MODEL), lambda i, j: (i, 0)),
            pl.BlockSpec((D_MODEL, tn), lambda i, j: (0, j)),
            pl.BlockSpec((1, tn), lambda i, j: (0, j)),
            pl.BlockSpec((2, tn), lambda i, j: (0, j)),
        ],
        out_specs=pl.BlockSpec((seq, tn), lambda i, j: (i, j)),
        out_shape=jax.ShapeDtypeStruct((t, n), BF16),
        compiler_params=_proj_params(),
        name="proj_act",
    )(xn, w, bias, mode)


def _rot_perm():
    half = RET_QK_DIM // 2
    order = [p * LANES + hh * RET_QK_DIM + part * half + d
             for p in range(RET_QK_WIDTH // LANES) for part in range(2) for hh in range(2)
             for d in range(half)]
    return np.asarray(order, np.int32)


def _sb_kernel(q_ref, k_ref, v_ref, o_ref, u_ref, acc_ref, car_ref):
    seq = q_ref.shape[0]
    nq = seq // SB_BLK
    row = lax.broadcasted_iota(jnp.int32, (SB_BLK, SB_BLK), 0)
    col = lax.broadcasted_iota(jnp.int32, (SB_BLK, SB_BLK), 1)
    u_ref[...] = (row >= col).astype(BF16)
    lane = lax.broadcasted_iota(jnp.int32, (1, LANES), 1)
    head_masks = (lane < SB_HEAD_DIM, lane >= SB_HEAD_DIM)

    heads = range(2)
    causal = col < row

    def rows_of(blk):
        return pl.ds(pl.multiple_of(blk * SB_BLK, SB_BLK), SB_BLK)

    def q_heads_of(qi):
        q_blk = q_ref[rows_of(qi), :]
        return tuple(jnp.where(m, q_blk, jnp.zeros_like(q_blk)) for m in head_masks)

    def sweep(jobs):
        u = u_ref[...]
        q_heads = [q_heads_of(qi) for _, qi, _, _ in jobs]
        z2 = [[[lax.dot_general(q_heads[j][h], k_ref[rows_of(kb), :], (((1,), (1,)), ((), ())),
                                preferred_element_type=F32) for h in heads]
               for kb, _ in steps] for j, (_, _, steps, _) in enumerate(jobs)]
        sinc = []
        for j, (_, _, steps, _) in enumerate(jobs):
            sinc.append([])
            for i, (_, diagonal) in enumerate(steps):
                nlk = [jnp.maximum(z, 0.0) + jnp.log2(1.0 + jnp.exp2(-jnp.abs(z))) for z in z2[j][i]]
                if diagonal:
                    nlk = [jnp.where(causal, a, 0.0) for a in nlk]
                sinc[j].append([jnp.dot(a.astype(BF16), u, preferred_element_type=F32) for a in nlk])
        for j, (slot, _, steps, fresh) in enumerate(jobs):
            car = [None, None] if fresh else [car_ref[slot, h] for h in heads]
            contrib = None
            for i, (kb, diagonal) in enumerate(steps):
                v_blk = v_ref[rows_of(kb), :]
                for h in heads:
                    e = z2[j][i][h] - sinc[j][i][h]
                    if car[h] is not None:
                        e = e - jnp.concatenate([car[h], car[h]], axis=1)
                    w = jnp.exp2(e)
                    if diagonal:
                        w = jnp.where(causal, w, 0.0)
                    vh = jnp.where(head_masks[h], v_blk, jnp.zeros_like(v_blk))
                    pv = jnp.dot(w.astype(BF16), vh, preferred_element_type=F32)
                    contrib = pv if contrib is None else contrib + pv
                    tot = jnp.broadcast_to(sinc[j][i][h][:, 0:1], (SB_BLK, LANES))
                    car[h] = tot if car[h] is None else car[h] + tot
            for h in heads:
                car_ref[slot, h] = car[h]
            if fresh:
                acc_ref[slot] = contrib
            else:
                acc_ref[slot] += contrib

    def finish(slot, qi, first_kb):
        if first_kb is not None:
            def more(c):
                kb, min_carry = c
                return (kb >= 0) & (min_carry < SB_SKIP_LOG2)

            def k_step(c):
                kb, _ = c
                sweep([(slot, qi, [(kb, False)], False)])
                return kb - 1, jnp.min(car_ref[slot])

            lax.while_loop(more, k_step, (first_kb, jnp.min(car_ref[slot])))
        o_ref[rows_of(qi), :] = acc_ref[slot].astype(BF16)

    def first_steps(qi):
        return [(qi, True)] + ([(qi - 1, False)] if qi > 0 else [])

    sweep([(s, s, first_steps(s), True) for s in range(SB_GROUP)])
    for s in range(SB_GROUP):
        finish(s, s, s - 2 if s >= 2 else None)

    def q_group(g, carry):
        q0 = g * SB_GROUP
        sweep([(s, q0 + s, [(q0 + s, True), (q0 + s - 1, False)], True) for s in range(SB_GROUP)])
        for s in range(SB_GROUP):
            finish(s, q0 + s, q0 + s - 2)
        return carry

    lax.fori_loop(1, nq // SB_GROUP, q_group, 0)


def _sb_call(proj, seq):
    t = proj.shape[0]
    n_pairs = SB_WIDTH // LANES
    return pl.pallas_call(
        _sb_kernel,
        grid=(t // seq, n_pairs),
        in_specs=[
            pl.BlockSpec((seq, LANES), lambda b, p: (b, p)),
            pl.BlockSpec((seq, LANES), lambda b, p: (b, n_pairs + p)),
            pl.BlockSpec((seq, LANES), lambda b, p: (b, 2 * n_pairs + p)),
        ],
        out_specs=pl.BlockSpec((seq, LANES), lambda b, p: (b, p)),
        out_shape=jax.ShapeDtypeStruct((t, SB_WIDTH), BF16),
        scratch_shapes=[
            pltpu.VMEM((SB_BLK, SB_BLK), BF16),
            pltpu.VMEM((SB_GROUP, SB_BLK, LANES), F32),
            pltpu.VMEM((SB_GROUP, 2, SB_BLK, LANES), F32),
        ],
        compiler_params=pltpu.CompilerParams(
            dimension_semantics=("arbitrary", "arbitrary"), vmem_limit_bytes=VMEM_LIMIT),
        name="stickbreak",
    )(proj, proj, proj)


def _ret_tables():
    h = np.arange(RET_HEADS, dtype=np.float64)
    log_gamma = np.log(1.0 - 2.0 ** (-5.0 - h))
    idx = np.arange(RET_BLK, dtype=np.float64)
    t, s = idx[:, None], idx[None, :]
    same = (t // CHUNK) == (s // CHUNK)
    earlier = (s // CHUNK) < (t // CHUNK)
    expo = np.where(same, np.abs(t - s), np.where(earlier, t - s, 0.0))
    dmat = np.exp(log_gamma[:, None, None] * expo) * (same | earlier)
    qdec = np.exp(log_gamma[:, None] * (idx + 1.0)[None, :])
    kdec = np.exp(log_gamma[:, None] * (RET_BLK - 1.0 - idx)[None, :])
    cdec = np.exp(log_gamma * RET_BLK)
    n_pairs = RET_HEADS // 2
    lane_head = (np.arange(LANES) % RET_QK_DIM) // (RET_QK_DIM // 2)
    col_head = np.arange(2 * RET_V_DIM) // RET_V_DIM
    pair_heads = np.arange(RET_HEADS).reshape(n_pairs, 2)
    qdec_pair = np.stack([qdec[pair_heads[p][lane_head]].T for p in range(n_pairs)])
    kdec_pair = np.stack([kdec[pair_heads[p][lane_head]].T for p in range(n_pairs)])
    cdec_pair = np.stack([cdec[pair_heads[p][col_head]][None, :] for p in range(n_pairs)])
    return (jnp.asarray(dmat.reshape((n_pairs, 2, RET_BLK, RET_BLK)), F32),
            jnp.asarray(qdec_pair, F32), jnp.asarray(kdec_pair, F32), jnp.asarray(cdec_pair, F32))


def _ret_kernel(q_ref, k_ref, v_ref, g_ref, gn_ref, dm_ref, qd_ref, kd_ref, cd_ref, o_ref, st_ref):
    seq = q_ref.shape[0]
    lane = lax.broadcasted_iota(jnp.int32, (1, LANES), 1)
    head0 = (lane % RET_QK_DIM) < (RET_QK_DIM // 2)
    head_masks = (head0, jnp.logical_not(head0))
    st_row = lax.broadcasted_iota(jnp.int32, st_ref.shape, 0)
    st_col = lax.broadcasted_iota(jnp.int32, st_ref.shape, 1)
    own_head = ((st_row % RET_QK_DIM) // (RET_QK_DIM // 2)) == (st_col // RET_V_DIM)
    heads = range(2)
    nt = (((1,), (1,)), ((), ()))
    tn = (((0,), (0,)), ((), ()))
    st_ref[...] = jnp.zeros_like(st_ref)

    def step(n, carry):
        subs = range(RET_UNROLL)
        rows = [pl.ds(pl.multiple_of((n * RET_UNROLL + u) * RET_BLK, RET_BLK), RET_BLK) for u in subs]
        col = [slice(h * RET_V_DIM, (h + 1) * RET_V_DIM) for h in heads]
        q_blk = [q_ref[r, :] for r in rows]
        k_blk = [k_ref[r, :] for r in rows]
        qm = [[jnp.where(head_masks[h], q_blk[u], jnp.zeros_like(q_blk[u])) for h in heads] for u in subs]
        v = [v_ref[r, :] for r in rows]
        scores = [[lax.dot_general(qm[u][h], k_blk[u], nt, preferred_element_type=F32)
                   for h in heads] for u in subs]
        kd = [(k_blk[u].astype(F32) * kd_ref[0]).astype(BF16) for u in subs]
        kv = [jnp.where(own_head, lax.dot_general(kd[u], v[u], tn, preferred_element_type=F32), 0.0)
              for u in subs]
        qd = [(q_blk[u].astype(F32) * qd_ref[0]).astype(BF16) for u in subs]
        p = [[(scores[u][h] * dm_ref[0, h]).astype(BF16) for h in heads] for u in subs]
        intra = [[jnp.dot(p[u][h], v[u][:, col[h]], preferred_element_type=F32) for h in heads]
                 for u in subs]
        state = st_ref[...]
        for u in subs:
            cross = jnp.dot(qd[u], state.astype(BF16), preferred_element_type=F32)
            state = state * cd_ref[0] + kv[u]
            for h in heads:
                y = intra[u][h] + cross[:, col[h]]
                ms = jnp.mean(y * y, axis=-1, keepdims=True)
                yn = (y * lax.rsqrt(ms + EPS)) * gn_ref[:, col[h]]
                o_ref[rows[u], col[h]] = (g_ref[rows[u], col[h]].astype(F32) * yn).astype(BF16)
        st_ref[...] = state
        return carry

    lax.fori_loop(0, seq // (RET_BLK * RET_UNROLL), step, 0)


def _ret_call(rqk, plain, act, ret_norm_g, seq):
    t = rqk.shape[0]
    n_pairs = RET_HEADS // 2
    dmat, qdec, kdec, cdec = _ret_tables()
    k0 = RET_QK_WIDTH // LANES
    v0 = (3 * SB_WIDTH) // (2 * RET_V_DIM)
    return pl.pallas_call(
        _ret_kernel,
        grid=(t // seq, n_pairs),
        in_specs=[
            pl.BlockSpec((seq, LANES), lambda b, p: (b, p)),
            pl.BlockSpec((seq, LANES), lambda b, p: (b, k0 + p)),
            pl.BlockSpec((seq, 2 * RET_V_DIM), lambda b, p: (b, v0 + p)),
            pl.BlockSpec((seq, 2 * RET_V_DIM), lambda b, p: (b, p)),
            pl.BlockSpec((1, 2 * RET_V_DIM), lambda b, p: (0, p)),
            pl.BlockSpec((1, 2, RET_BLK, RET_BLK), lambda b, p: (p, 0, 0, 0)),
            pl.BlockSpec((1, RET_BLK, LANES), lambda b, p: (p, 0, 0)),
            pl.BlockSpec((1, RET_BLK, LANES), lambda b, p: (p, 0, 0)),
            pl.BlockSpec((1, 1, 2 * RET_V_DIM), lambda b, p: (p, 0, 0)),
        ],
        out_specs=pl.BlockSpec((seq, 2 * RET_V_DIM), lambda b, p: (b, p)),
        out_shape=jax.ShapeDtypeStruct((t, RET_V_WIDTH), BF16),
        scratch_shapes=[pltpu.VMEM((LANES, 2 * RET_V_DIM), F32)],
        compiler_params=pltpu.CompilerParams(
            dimension_semantics=("arbitrary", "arbitrary"), vmem_limit_bytes=VMEM_LIMIT),
        name="retention",
    )(rqk, rqk, plain, act, ret_norm_g, dmat, qdec, kdec, cdec)


def _mix_kernel(ysb_ref, yret_ref, gs0_ref, gs1_ref, gr0_ref, gr1_ref, x_ref,
                wsb_ref, wret_ref, wout_ref, gffn_ref, wrh_ref, wrl_ref, br_ref, tri_ref,
                hext_ref, cls_ref, cnt_ref):
    tm = x_ref.shape[0]

    @pl.when(pl.program_id(0) == 0)
    def _():
        cnt_ref[...] = jnp.zeros_like(cnt_ref)

    a = jnp.dot(ysb_ref[...], wsb_ref[...], preferred_element_type=F32)
    b = jnp.dot(yret_ref[...], wret_ref[...], preferred_element_type=F32)
    g_sb = jnp.concatenate([gs0_ref[...], gs1_ref[...]], axis=1).astype(F32)
    g_ret = jnp.concatenate([gr0_ref[...], gr1_ref[...]], axis=1).astype(F32)
    mixed = (g_sb * a + g_ret * b).astype(BF16)
    h = x_ref[...] + jnp.dot(mixed, wout_ref[...], preferred_element_type=F32)
    hext_ref[:, 0:D_MODEL] = h

    ms = jnp.mean(h * h, axis=-1, keepdims=True)
    hn = (h * lax.rsqrt(ms + EPS)) * gffn_ref[...]
    hn_hi = hn.astype(BF16)
    hn_lo = (hn - hn_hi.astype(F32)).astype(BF16)
    nt = (((1,), (1,)), ((), ()))
    logits = (lax.dot_general(wrh_ref[...], hn_hi, nt, preferred_element_type=F32)
              + lax.dot_general(wrh_ref[...], hn_lo, nt, preferred_element_type=F32)
              + lax.dot_general(wrl_ref[...], hn_hi, nt, preferred_element_type=F32)
              + br_ref[...])

    def first_argmax(vals):
        m = functools.reduce(jnp.maximum, vals)
        idx = jnp.full(m.shape, len(vals) - 1, jnp.int32)
        for i in range(len(vals) - 2, -1, -1):
            idx = jnp.where(vals[i] >= m, i, idx)
        return m, idx

    gl = [logits[r:r + 1, :] for r in range(N_GROUPS)]
    gmax, gsel = first_argmax(gl)
    p_group = 1.0 / functools.reduce(lambda s, v: s + v, [jnp.exp(v - gmax) for v in gl])
    el = []
    for e in range(EXPERTS_PER_GROUP):
        v = logits[N_GROUPS + 3 * EXPERTS_PER_GROUP + e:N_GROUPS + 3 * EXPERTS_PER_GROUP + e + 1, :]
        for g in range(N_GROUPS - 2, -1, -1):
            r = N_GROUPS + g * EXPERTS_PER_GROUP + e
            v = jnp.where(gsel == g, logits[r:r + 1, :], v)
        el.append(v)
    m1, i1 = first_argmax(el)
    rest = [jnp.where(i1 == e, -jnp.inf, el[e]) for e in range(EXPERTS_PER_GROUP)]
    m2, i2 = first_argmax(rest)
    tt = jnp.exp(m2 - m1)
    gate1 = p_group * (1.0 / (1.0 + tt))
    gate2 = p_group * (tt / (1.0 + tt))
    lo_first = i1 < i2
    ea = jnp.minimum(i1, i2)
    eb = jnp.maximum(i1, i2)
    w_a = jnp.where(lo_first, gate1, gate2)
    w_b = jnp.where(lo_first, gate2, gate1)
    pid = jnp.where(ea == 0, eb - 1, jnp.where(ea == 1, eb + 1, PAIRS_PER_GROUP - 1))
    cls = gsel * PAIRS_PER_GROUP + pid
    info = jnp.concatenate([w_a, w_b, jnp.zeros((INFO_COLS - 2, tm), F32)], axis=0)
    hext_ref[:, D_MODEL:HEXT_COLS] = info.T

    class_row = lax.broadcasted_iota(jnp.int32, (ROUTER_ROWS, tm), 0)
    onehot = class_row == cls
    onehot_bf = onehot.astype(BF16)
    before = jnp.dot(onehot_bf, tri_ref[...], preferred_element_type=F32)
    cnt = cnt_ref[...]
    seen = before + jnp.concatenate([cnt] * (tm // LANES), axis=1)
    rank = jnp.sum(jnp.where(onehot, seen, 0.0), axis=0, keepdims=True)
    cnt_ref[...] = cnt + jnp.dot(onehot_bf, jnp.ones((tm, LANES), BF16), preferred_element_type=F32)
    cls_ref[...] = jnp.concatenate(
        [cls, rank.astype(jnp.int32), jnp.zeros((cls_ref.shape[0] - 2, tm), jnp.int32)], axis=0)


def _mix_call(ysb, yret, act, x2, wsb, wret, wout, gffn, wr_hi, wr_lo, br):
    t = x2.shape[0]
    tm = MIX_TM
    gate0 = RET_V_WIDTH // GATE_BLK
    const = lambda i: (0, 0)
    idx = jnp.arange(tm, dtype=jnp.int32)
    tri = (idx[:, None] < idx[None, :]).astype(BF16)
    return pl.pallas_call(
        _mix_kernel,
        grid=(t // tm,),
        in_specs=[
            pl.BlockSpec((tm, SB_WIDTH), lambda i: (i, 0)),
            pl.BlockSpec((tm, RET_V_WIDTH), lambda i: (i, 0)),
            pl.BlockSpec((tm, GATE_BLK), lambda i: (i, gate0)),
            pl.BlockSpec((tm, GATE_BLK), lambda i: (i, gate0 + 1)),
            pl.BlockSpec((tm, GATE_BLK), lambda i: (i, gate0 + 2)),
            pl.BlockSpec((tm, GATE_BLK), lambda i: (i, gate0 + 3)),
            pl.BlockSpec((tm, D_MODEL), lambda i: (i, 0)),
            pl.BlockSpec((SB_WIDTH, D_MODEL), const),
            pl.BlockSpec((RET_V_WIDTH, D_MODEL), const),
            pl.BlockSpec((D_MODEL, D_MODEL), const),
            pl.BlockSpec((1, D_MODEL), const),
            pl.BlockSpec((ROUTER_ROWS, D_MODEL), const),
            pl.BlockSpec((ROUTER_ROWS, D_MODEL), const),
            pl.BlockSpec((ROUTER_ROWS, 1), const),
            pl.BlockSpec((tm, tm), const),
        ],
        out_specs=[
            pl.BlockSpec((tm, HEXT_COLS), lambda i: (i, 0)),
            pl.BlockSpec((8, tm), lambda i: (0, i)),
            pl.BlockSpec((ROUTER_ROWS, LANES), const),
        ],
        out_shape=[
            jax.ShapeDtypeStruct((t, HEXT_COLS), F32),
            jax.ShapeDtypeStruct((8, t), jnp.int32),
            jax.ShapeDtypeStruct((ROUTER_ROWS, LANES), F32),
        ],
        compiler_params=pltpu.CompilerParams(
            dimension_semantics=("arbitrary",), vmem_limit_bytes=VMEM_LIMIT),
        name="mix_router",
    )(ysb, yret, act, act, act, act, x2, wsb, wret, wout, gffn, wr_hi, wr_lo, br, tri)


def _moe_kernel(dest_ref, nvalid_ref, ea_ref, eb_ref,
                hext_ref, wga_ref, wua_ref, wda_ref, wgb_ref, wub_ref, wdb_ref,
                gffn_ref, gfin_ref, out_ref, hbuf, obuf, src, gsem, ssem):
    i = pl.program_id(0)
    n_tiles = pl.num_programs(0)
    tm = hbuf.shape[1]
    nv = nvalid_ref[i]
    nxt = jnp.minimum(i + 1, n_tiles - 1)
    nv_next = jnp.where(i + 1 < n_tiles, nvalid_ref[nxt], 0)
    has_next = nv_next > 0

    def gather_copy(tile, buf_slot, r):
        tok = src[tile * tm + r]
        return pltpu.make_async_copy(hext_ref.at[pl.ds(tok, 1), :],
                                     hbuf.at[buf_slot, pl.ds(r, 1), :], gsem.at[buf_slot])

    def scatter_copy(tile, buf_slot, r):
        tok = src[tile * tm + r]
        return pltpu.make_async_copy(obuf.at[buf_slot, pl.ds(r, 1), :],
                                     out_ref.at[pl.ds(tok, 1), :], ssem.at[buf_slot])

    def start_rows(copy_of_row, rows_valid):
        for r in range(tm):
            @pl.when(r < rows_valid)
            def _():
                copy_of_row(r).start(priority=r % 2)

    def wait_rows(full_copy, row_copy, rows_valid):
        @pl.when(rows_valid == tm)
        def _():
            full_copy.wait()

        @pl.when(rows_valid < tm)
        def _():
            for r in range(tm):
                @pl.when(r < rows_valid)
                def _():
                    row_copy(r).wait()

    def wait_gather(buf_slot, rows_valid):
        wait_rows(pltpu.make_async_copy(hext_ref.at[pl.ds(0, tm), :], hbuf.at[buf_slot], gsem.at[buf_slot]),
                  lambda r: pltpu.make_async_copy(hext_ref.at[pl.ds(0, 1), :],
                                                  hbuf.at[buf_slot, pl.ds(r, 1), :], gsem.at[buf_slot]),
                  rows_valid)

    def wait_scatter(buf_slot, rows_valid):
        wait_rows(pltpu.make_async_copy(obuf.at[buf_slot], out_ref.at[pl.ds(0, tm), :], ssem.at[buf_slot]),
                  lambda r: pltpu.make_async_copy(obuf.at[buf_slot, pl.ds(r, 1), :],
                                                  out_ref.at[pl.ds(0, 1), :], ssem.at[buf_slot]),
                  rows_valid)

    @pl.when(i == 0)
    def _():
        def invert(c, carry):
            for u in range(MOE_INVERT_UNROLL):
                t = c * MOE_INVERT_UNROLL + u
                src[dest_ref[t]] = t
            return carry

        lax.fori_loop(0, dest_ref.shape[0] // MOE_INVERT_UNROLL, invert, 0)
        hbuf[...] = jnp.zeros_like(hbuf)
        start_rows(functools.partial(gather_copy, 0, 0), nv)

    def tile_body(slot):
        @pl.when(i >= 2)
        def _():
            wait_scatter(slot, nvalid_ref[jnp.maximum(i - 2, 0)])

        wait_gather(slot, nv)
        start_rows(functools.partial(gather_copy, nxt, 1 - slot), nv_next)
        hrows = hbuf[slot]
        h = hrows[:, 0:D_MODEL]
        w_a = hrows[:, D_MODEL:D_MODEL + 1]
        w_b = hrows[:, D_MODEL + 1:D_MODEL + 2]
        ms = jnp.mean(h * h, axis=-1, keepdims=True)
        hn = ((h * lax.rsqrt(ms + EPS)) * gffn_ref[...]).astype(BF16)

        def expert(wg_ref, wu_ref, wd_ref):
            gate = jnp.dot(hn, wg_ref[0], preferred_element_type=F32)
            up = jnp.dot(hn, wu_ref[0], preferred_element_type=F32)
            hidden = ((gate * jax.nn.sigmoid(gate)) * up).astype(BF16)
            return jnp.dot(hidden, wd_ref[0], preferred_element_type=F32)

        y = w_a * expert(wga_ref, wua_ref, wda_ref) + w_b * expert(wgb_ref, wub_ref, wdb_ref)
        h2 = h + y
        ms2 = jnp.mean(h2 * h2, axis=-1, keepdims=True)
        obuf[slot] = (h2 * lax.rsqrt(ms2 + EPS)) * gfin_ref[...]
        start_rows(functools.partial(scatter_copy, i, slot), nv)

        @pl.when(jnp.logical_not(has_next))
        def _():
            @pl.when(i >= 1)
            def _():
                wait_scatter(1 - slot, nvalid_ref[jnp.maximum(i - 1, 0)])

            wait_scatter(slot, nv)

    for parity in range(2):
        pl.when((nv > 0) & (i % 2 == parity))(functools.partial(tile_body, parity))


def _moe_call(dest, nvalid, ea, eb, hext, wg, wu, wd, gffn, gfin):
    t = hext.shape[0]
    tm = MOE_TM
    n_tiles = nvalid.shape[0]
    assert t % MOE_INVERT_UNROLL == 0
    wa_map = lambda i, dest, nv, ea, eb: (ea[i], 0, 0)
    wb_map = lambda i, dest, nv, ea, eb: (eb[i], 0, 0)
    const = lambda i, dest, nv, ea, eb: (0, 0)
    return pl.pallas_call(
        _moe_kernel,
        grid_spec=pltpu.PrefetchScalarGridSpec(
            num_scalar_prefetch=4,
            grid=(n_tiles,),
            in_specs=[
                pl.BlockSpec(memory_space=pl.ANY),
                pl.BlockSpec((1, D_MODEL, D_FF), wa_map),
                pl.BlockSpec((1, D_MODEL, D_FF), wa_map),
                pl.BlockSpec((1, D_FF, D_MODEL), wa_map),
                pl.BlockSpec((1, D_MODEL, D_FF), wb_map),
                pl.BlockSpec((1, D_MODEL, D_FF), wb_map),
                pl.BlockSpec((1, D_FF, D_MODEL), wb_map),
                pl.BlockSpec((1, D_MODEL), const),
                pl.BlockSpec((1, D_MODEL), const),
            ],
            out_specs=pl.BlockSpec(memory_space=pl.ANY),
            scratch_shapes=[
                pltpu.VMEM((2, tm, HEXT_COLS), F32),
                pltpu.VMEM((2, tm, D_MODEL), F32),
                pltpu.SMEM((n_tiles * tm,), jnp.int32),
                pltpu.SemaphoreType.DMA((2,)),
                pltpu.SemaphoreType.DMA((2,)),
            ],
        ),
        out_shape=jax.ShapeDtypeStruct((t, D_MODEL), F32),
        compiler_params=pltpu.CompilerParams(
            dimension_semantics=("arbitrary",), vmem_limit_bytes=VMEM_LIMIT),
        name="experts",
    )(dest, nvalid, ea, eb, hext, wg, wu, wd, wg, wu, wd, gffn, gfin)


def _routing_plan(cls, rank, counts, tm):
    t = cls.shape[0]
    n_tiles = t // tm + N_CLASSES
    tiles_c = (counts + tm - 1) // tm
    tile_end = jnp.cumsum(tiles_c)
    tile_off = tile_end - tiles_c
    onehot = cls[:, None] == jnp.arange(N_CLASSES, dtype=jnp.int32)[None, :]
    dest = jnp.sum(jnp.where(onehot, tile_off[None, :], 0), axis=1) * tm + rank
    tile = jnp.arange(n_tiles, dtype=jnp.int32)
    used = tile < tile_end[-1]
    tile_cls = jnp.minimum(jnp.sum((tile[:, None] >= tile_end[None, :]).astype(jnp.int32), axis=1),
                           N_CLASSES - 1)
    cls_onehot = tile_cls[:, None] == jnp.arange(N_CLASSES, dtype=jnp.int32)[None, :]
    pick = lambda table: jnp.sum(jnp.where(cls_onehot, table[None, :], 0), axis=1)
    nvalid = jnp.where(used, jnp.clip(pick(counts) - (tile - pick(tile_off)) * tm, 0, tm), 0)
    classes = np.arange(N_CLASSES)
    ea_tab = jnp.asarray((classes // PAIRS_PER_GROUP) * EXPERTS_PER_GROUP
                         + np.asarray(_PAIR_A)[classes % PAIRS_PER_GROUP], jnp.int32)
    eb_tab = jnp.asarray((classes // PAIRS_PER_GROUP) * EXPERTS_PER_GROUP
                         + np.asarray(_PAIR_B)[classes % PAIRS_PER_GROUP], jnp.int32)
    last_used = jnp.sum(jnp.where(tile == tile_end[-1] - 1, tile_cls, 0))
    tile_cls = jnp.where(used, tile_cls, last_used)
    cls_onehot = tile_cls[:, None] == jnp.arange(N_CLASSES, dtype=jnp.int32)[None, :]
    return dest.astype(jnp.int32), nvalid.astype(jnp.int32), pick(ea_tab), pick(eb_tab)


def _rope_tables(seq):
    half = RET_QK_DIM // 2
    inv_freq = ROPE_BASE ** (-jnp.arange(half, dtype=F32) / half)
    ang = jnp.arange(seq, dtype=F32)[:, None] * inv_freq[None, :]
    cos, sin = jnp.cos(ang), jnp.sin(ang)
    cos_t = jnp.tile(cos, (1, LANES // half))
    sin_t = jnp.concatenate([-sin, -sin, sin, sin], axis=1)
    return cos_t, sin_t


def kernel(x, norm_mix_g, w_in, w_gate, b_gate, w_sb_out, w_ret_out, ret_norm_g, w_out,
           norm_ffn_g, w_group_router, b_group_router, w_expert_router, b_expert_router,
           w_exp_gate, w_exp_up, w_exp_down, norm_final_g):
    bsz, seq, d = x.shape
    assert d == D_MODEL and w_in.shape[0] == 1 and seq % (SB_GROUP * SB_BLK) == 0 and seq % (RET_BLK * RET_UNROLL) == 0 and (bsz * seq) % MIX_TM == 0
    t = bsz * seq
    x2 = x.reshape(t, d)
    wi = w_in[0]
    c_rq = 3 * SB_WIDTH
    c_rk = c_rq + RET_QK_WIDTH
    c_rv = c_rk + RET_QK_WIDTH
    c_rg = c_rv + RET_V_WIDTH
    ones = functools.partial(jnp.ones, dtype=F32)
    w_plain = jnp.concatenate([wi[:, :c_rq], wi[:, c_rv:c_rg]], axis=1).astype(BF16)
    s_plain = jnp.concatenate([jnp.full((SB_WIDTH,), SB_HEAD_DIM ** -0.5 * LOG2E, F32),
                               ones((2 * SB_WIDTH + RET_V_WIDTH,))])[None, :]
    perm = _rot_perm()
    w_rot = jnp.concatenate([wi[:, c_rq:c_rk][:, perm], wi[:, c_rk:c_rv][:, perm]], axis=1).astype(BF16)
    s_rot = jnp.concatenate([ones((RET_QK_WIDTH,)),
                             jnp.full((RET_QK_WIDTH,), RET_QK_DIM ** -0.5, F32)])[None, :]
    w_act = jnp.concatenate([wi[:, c_rg:], w_gate[0]], axis=1).astype(BF16)
    b_act = jnp.concatenate([jnp.zeros((RET_V_WIDTH,), F32), b_gate[0]])[None, :]
    swish_cols = jnp.concatenate([ones((RET_V_WIDTH,)), jnp.zeros((GATE_COLS,), F32)])
    m_act = jnp.stack([swish_cols, 1.0 - swish_cols])
    cos_t, sin_t = _rope_tables(seq)

    plain, xn = _proj_plain_call(x2, norm_mix_g[0][None, :], w_plain, s_plain, seq)
    rqk = _proj_rot_call(xn, w_rot, s_rot, cos_t, sin_t, seq)
    act = _proj_act_call(xn, w_act, b_act, m_act, seq)
    ysb = _sb_call(plain, seq)
    yret = _ret_call(rqk, plain, act, ret_norm_g[0][None, :], seq)

    wr = jnp.concatenate([w_group_router[0], w_expert_router[0]], axis=1).T
    wr = jnp.pad(wr, ((0, ROUTER_ROWS - wr.shape[0]), (0, 0)))
    wr_hi = wr.astype(BF16)
    wr_lo = (wr - wr_hi.astype(F32)).astype(BF16)
    br = jnp.concatenate([b_group_router[0], b_expert_router[0]])
    br = jnp.pad(br, (0, ROUTER_ROWS - br.shape[0]))[:, None]
    hext, cls8, cnt = _mix_call(ysb, yret, act, x2, w_sb_out[0].astype(BF16), w_ret_out[0].astype(BF16),
                           w_out[0].astype(BF16), norm_ffn_g[0][None, :], wr_hi, wr_lo, br)

    counts = cnt[:N_CLASSES, 0].astype(jnp.int32)
    dest, nvalid, ea, eb = _routing_plan(cls8[0], cls8[1], counts, MOE_TM)
    out = _moe_call(dest, nvalid, ea, eb, hext, w_exp_gate[0].astype(BF16), w_exp_up[0].astype(BF16),
                    w_exp_down[0].astype(BF16), norm_ffn_g[0][None, :], norm_final_g[None, :])
    return out.reshape(bsz, seq, d)
```

```python
import functools
import math

import numpy as np
import jax
import jax.numpy as jnp
from jax import lax
from jax.experimental import pallas as pl
from jax.experimental.pallas import tpu as pltpu

F32 = jnp.float32
BF16 = jnp.bfloat16

D_MODEL = 1024
SB_HEADS = 8
SB_HEAD_DIM = 64
SB_WIDTH = SB_HEADS * SB_HEAD_DIM
RET_HEADS = 8
RET_QK_DIM = 64
RET_V_DIM = 128
RET_QK_WIDTH = RET_HEADS * RET_QK_DIM
RET_V_WIDTH = RET_HEADS * RET_V_DIM
IN_COLS = 3 * SB_WIDTH + 2 * RET_QK_WIDTH + 2 * RET_V_WIDTH
GATE_COLS = 2 * D_MODEL
ALL_COLS = IN_COLS + GATE_COLS
CHUNK = 64
ROPE_BASE = 10000.0
N_GROUPS = 4
EXPERTS_PER_GROUP = 4
N_EXPERTS = N_GROUPS * EXPERTS_PER_GROUP
D_FF = 512
EPS = 1e-6

LANES = 128
PROJ_TN = 1280
GATE_BLK = 512
PROJ_ROWS = 256
SB_BLK = 256
SB_GROUP = 4
RET_BLK = 256
RET_UNROLL = 4
MIX_TM = 512
MOE_TM = 256
MOE_INVERT_UNROLL = 32
ROUTER_ROWS = 32
PAIRS_PER_GROUP = 6
N_CLASSES = N_GROUPS * PAIRS_PER_GROUP
INFO_COLS = LANES
HEXT_COLS = D_MODEL + INFO_COLS
VMEM_LIMIT = 56 * 1024 * 1024
LOG2E = math.log2(math.e)
SB_SKIP_LOG2 = 156.0

_PAIR_A = (0, 0, 0, 1, 1, 2)
_PAIR_B = (1, 2, 3, 2, 3, 3)


def _proj_plain_kernel(x_ref, g_ref, w_ref, s_ref, o_ref, xn_ref):
    @pl.when(pl.program_id(1) == 0)
    def _():
        x = x_ref[...]
        ms = jnp.mean(x * x, axis=-1, keepdims=True)
        xn_ref[...] = ((x * lax.rsqrt(ms + EPS)) * g_ref[...]).astype(BF16)

    for rows in _row_chunks(xn_ref.shape[0]):
        acc = jnp.dot(xn_ref[rows, :], w_ref[...], preferred_element_type=F32)
        o_ref[rows, :] = (acc * s_ref[...]).astype(BF16)


def _row_chunks(n_rows):
    return [slice(r, r + PROJ_ROWS) for r in range(0, n_rows, PROJ_ROWS)]


def _proj_rot_kernel(xn_ref, w_ref, cos_ref, sin_ref, o_ref):
    for rows in _row_chunks(xn_ref.shape[0]):
        acc = jnp.dot(xn_ref[rows, :], w_ref[...], preferred_element_type=F32)
        cos = cos_ref[rows, :]
        sin = sin_ref[rows, :]
        for p in range(w_ref.shape[1] // LANES):
            cols = slice(p * LANES, (p + 1) * LANES)
            seg = acc[:, cols]
            o_ref[rows, cols] = (seg * cos + pltpu.roll(seg, LANES // 2, 1) * sin).astype(BF16)


def _proj_swish_kernel(xn_ref, w_ref, o_ref):
    for rows in _row_chunks(xn_ref.shape[0]):
        half = jnp.dot(xn_ref[rows, :], w_ref[...], preferred_element_type=F32)
        o_ref[rows, :] = (half * (jnp.tanh(half) + 1.0)).astype(BF16)


def _proj_gate_kernel(xn_ref, w_ref, b_ref, o_ref):
    for rows in _row_chunks(xn_ref.shape[0]):
        half = jnp.dot(xn_ref[rows, :], w_ref[...], preferred_element_type=F32) + b_ref[...]
        o_ref[rows, :] = (0.5 * jnp.tanh(half) + 0.5).astype(BF16)


def _proj_params():
    return pltpu.CompilerParams(dimension_semantics=("arbitrary", "arbitrary"),
                                vmem_limit_bytes=VMEM_LIMIT)


def _proj_tile(n):
    tn = PROJ_TN
    while n % tn:
        tn -= LANES
    return tn


def _proj_plain_call(x2, g, w, scale, seq):
    t, n = x2.shape[0], w.shape[1]
    tn = _proj_tile(n)
    return pl.pallas_call(
        _proj_plain_kernel,
        grid=(t // seq, n // tn),
        in_specs=[
            pl.BlockSpec((seq, D_MODEL), lambda i, j: (i, 0)),
            pl.BlockSpec((1, D_MODEL), lambda i, j: (0, 0)),
            pl.BlockSpec((D_MODEL, tn), lambda i, j: (0, j)),
            pl.BlockSpec((1, tn), lambda i, j: (0, j)),
        ],
        out_specs=[
            pl.BlockSpec((seq, tn), lambda i, j: (i, j)),
            pl.BlockSpec((seq, D_MODEL), lambda i, j: (i, 0)),
        ],
        out_shape=[jax.ShapeDtypeStruct((t, n), BF16), jax.ShapeDtypeStruct((t, D_MODEL), BF16)],
        compiler_params=_proj_params(),
        name="proj_plain",
    )(x2, g, w, scale)


def _proj_rot_call(xn, w, cos_t, sin_t, seq):
    t, n = xn.shape[0], w.shape[1]
    tn = _proj_tile(n)
    return pl.pallas_call(
        _proj_rot_kernel,
        grid=(t // seq, n // tn),
        in_specs=[
            pl.BlockSpec((seq, D_MODEL), lambda i, j: (i, 0)),
            pl.BlockSpec((D_MODEL, tn), lambda i, j: (0, j)),
            pl.BlockSpec((seq, LANES), lambda i, j: (0, 0)),
            pl.BlockSpec((seq, LANES), lambda i, j: (0, 0)),
        ],
        out_specs=pl.BlockSpec((seq, tn), lambda i, j: (i, j)),
        out_shape=jax.ShapeDtypeStruct((t, n), BF16),
        compiler_params=_proj_params(),
        name="proj_rotary",
    )(xn, w, cos_t, sin_t)


def _proj_swish_call(xn, w_half, seq):
    t, n = xn.shape[0], w_half.shape[1]
    tn = _proj_tile(n)
    return pl.pallas_call(
        _proj_swish_kernel,
        grid=(t // seq, n // tn),
        in_specs=[
            pl.BlockSpec((seq, D_MODEL), lambda i, j: (i, 0)),
            pl.BlockSpec((D_MODEL, tn), lambda i, j: (0, j)),
        ],
        out_specs=pl.BlockSpec((seq, tn), lambda i, j: (i, j)),
        out_shape=jax.ShapeDtypeStruct((t, n), BF16),
        compiler_params=_proj_params(),
        name="proj_swish",
    )(xn, w_half)


def _proj_gate_call(xn, w_half, bias_half, seq):
    t, n = xn.shape[0], w_half.shape[1]
    tn = _proj_tile(n)
    return pl.pallas_call(
        _proj_gate_kernel,
        grid=(t // seq, n // tn),
        in_specs=[
            pl.BlockSpec((seq, D_MODEL), lambda i, j: (i, 0)),
            pl.BlockSpec((D_MODEL, tn), lambda i, j: (0, j)),
            pl.BlockSpec((1, tn), lambda i, j: (0, j)),
        ],
        out_specs=pl.BlockSpec((seq, tn), lambda i, j: (i, j)),
        out_shape=jax.ShapeDtypeStruct((t, n), BF16),
        compiler_params=_proj_params(),
        name="proj_gate",
    )(xn, w_half, bias_half)


def _rot_perm():
    half = RET_QK_DIM // 2
    order = [p * LANES + hh * RET_QK_DIM + part * half + d
             for p in range(RET_QK_WIDTH // LANES) for part in range(2) for hh in range(2)
             for d in range(half)]
    return np.asarray(order, np.int32)


def _sb_kernel(q_ref, k_ref, v_ref, o_ref, u_ref, acc_ref, car_ref):
    seq = q_ref.shape[0]
    nq = seq // SB_BLK
    row = lax.broadcasted_iota(jnp.int32, (SB_BLK, SB_BLK), 0)
    col = lax.broadcasted_iota(jnp.int32, (SB_BLK, SB_BLK), 1)
    u_ref[...] = (row >= col).astype(BF16)
    lane = lax.broadcasted_iota(jnp.int32, (1, LANES), 1)
    head_masks = (lane < SB_HEAD_DIM, lane >= SB_HEAD_DIM)

    heads = range(2)
    causal = col < row

    def rows_of(blk):
        return pl.ds(pl.multiple_of(blk * SB_BLK, SB_BLK), SB_BLK)

    def q_heads_of(qi):
        q_blk = q_ref[rows_of(qi), :]
        return tuple(jnp.where(m, q_blk, jnp.zeros_like(q_blk)) for m in head_masks)

    def sweep(jobs):
        u = u_ref[...]
        q_heads = [q_heads_of(qi) for _, qi, _, _ in jobs]
        z2 = [[[lax.dot_general(q_heads[j][h], k_ref[rows_of(kb), :], (((1,), (1,)), ((), ())),
                                preferred_element_type=F32) for h in heads]
               for kb, _ in steps] for j, (_, _, steps, _) in enumerate(jobs)]
        sinc = []
        for j, (_, _, steps, _) in enumerate(jobs):
            sinc.append([])
            for i, (_, diagonal) in enumerate(steps):
                nlk = [jnp.maximum(z, 0.0) + jnp.log2(1.0 + jnp.exp2(-jnp.abs(z))) for z in z2[j][i]]
                if diagonal:
                    nlk = [jnp.where(causal, a, 0.0) for a in nlk]
                sinc[j].append([jnp.dot(a.astype(BF16), u, preferred_element_type=F32) for a in nlk])
        for j, (slot, _, steps, fresh) in enumerate(jobs):
            car = [None, None] if fresh else [car_ref[slot, h] for h in heads]
            contrib = None
            for i, (kb, diagonal) in enumerate(steps):
                v_blk = v_ref[rows_of(kb), :]
                for h in heads:
                    e = z2[j][i][h] - sinc[j][i][h]
                    if car[h] is not None:
                        e = e - jnp.concatenate([car[h], car[h]], axis=1)
                    w = jnp.exp2(e)
                    if diagonal:
                        w = jnp.where(causal, w, 0.0)
                    vh = jnp.where(head_masks[h], v_blk, jnp.zeros_like(v_blk))
                    pv = jnp.dot(w.astype(BF16), vh, preferred_element_type=F32)
                    contrib = pv if contrib is None else contrib + pv
                    tot = jnp.broadcast_to(sinc[j][i][h][:, 0:1], (SB_BLK, LANES))
                    car[h] = tot if car[h] is None else car[h] + tot
            for h in heads:
                car_ref[slot, h] = car[h]
            if fresh:
                acc_ref[slot] = contrib
            else:
                acc_ref[slot] += contrib

    def finish(slot, qi, first_kb):
        if first_kb is not None:
            def more(c):
                kb, min_carry = c
                return (kb >= 0) & (min_carry < SB_SKIP_LOG2)

            def k_step(c):
                kb, _ = c
                sweep([(slot, qi, [(kb, False)], False)])
                return kb - 1, jnp.min(car_ref[slot])

            lax.while_loop(more, k_step, (first_kb, jnp.min(car_ref[slot])))
        o_ref[rows_of(qi), :] = acc_ref[slot].astype(BF16)

    def first_steps(qi):
        return [(qi, True)] + ([(qi - 1, False)] if qi > 0 else [])

    sweep([(s, s, first_steps(s), True) for s in range(SB_GROUP)])
    for s in range(SB_GROUP):
        finish(s, s, s - 2 if s >= 2 else None)

    def q_group(g, carry):
        q0 = g * SB_GROUP
        sweep([(s, q0 + s, [(q0 + s, True), (q0 + s - 1, False)], True) for s in range(SB_GROUP)])
        for s in range(SB_GROUP):
            finish(s, q0 + s, q0 + s - 2)
        return carry

    lax.fori_loop(1, nq // SB_GROUP, q_group, 0)


def _sb_call(proj, seq):
    t = proj.shape[0]
    n_pairs = SB_WIDTH // LANES
    return pl.pallas_call(
        _sb_kernel,
        grid=(t // seq, n_pairs),
        in_specs=[
            pl.BlockSpec((seq, LANES), lambda b, p: (b, p)),
            pl.BlockSpec((seq, LANES), lambda b, p: (b, n_pairs + p)),
            pl.BlockSpec((seq, LANES), lambda b, p: (b, 2 * n_pairs + p)),
        ],
        out_specs=pl.BlockSpec((seq, LANES), lambda b, p: (b, p)),
        out_shape=jax.ShapeDtypeStruct((t, SB_WIDTH), BF16),
        scratch_shapes=[
            pltpu.VMEM((SB_BLK, SB_BLK), BF16),
            pltpu.VMEM((SB_GROUP, SB_BLK, LANES), F32),
            pltpu.VMEM((SB_GROUP, 2, SB_BLK, LANES), F32),
        ],
        compiler_params=pltpu.CompilerParams(
            dimension_semantics=("arbitrary", "arbitrary"), vmem_limit_bytes=VMEM_LIMIT),
        name="stickbreak",
    )(proj, proj, proj)


def _ret_tables():
    h = np.arange(RET_HEADS, dtype=np.float64)
    log_gamma = np.log(1.0 - 2.0 ** (-5.0 - h))
    idx = np.arange(RET_BLK, dtype=np.float64)
    t, s = idx[:, None], idx[None, :]
    same = (t // CHUNK) == (s // CHUNK)
    earlier = (s // CHUNK) < (t // CHUNK)
    expo = np.where(same, np.abs(t - s), np.where(earlier, t - s, 0.0))
    dmat = np.exp(log_gamma[:, None, None] * expo) * (same | earlier)
    qdec = np.exp(log_gamma[:, None] * (idx + 1.0)[None, :])
    kdec = np.exp(log_gamma[:, None] * (RET_BLK - 1.0 - idx)[None, :])
    cdec = np.exp(log_gamma * RET_BLK)
    n_pairs = RET_HEADS // 2
    lane_head = (np.arange(LANES) % RET_QK_DIM) // (RET_QK_DIM // 2)
    col_head = np.arange(2 * RET_V_DIM) // RET_V_DIM
    pair_heads = np.arange(RET_HEADS).reshape(n_pairs, 2)
    qdec_pair = np.stack([qdec[pair_heads[p][lane_head]].T for p in range(n_pairs)])
    kdec_pair = np.stack([kdec[pair_heads[p][lane_head]].T for p in range(n_pairs)])
    cdec_pair = np.stack([cdec[pair_heads[p][col_head]][None, :] for p in range(n_pairs)])
    return (jnp.asarray(dmat.reshape((n_pairs, 2, RET_BLK, RET_BLK)), F32),
            jnp.asarray(qdec_pair, F32), jnp.asarray(kdec_pair, F32), jnp.asarray(cdec_pair, F32))


def _ret_kernel(q_ref, k_ref, v_ref, g_ref, gn_ref, dm_ref, qd_ref, kd_ref, cd_ref, o_ref, st_ref):
    seq = q_ref.shape[0]
    lane = lax.broadcasted_iota(jnp.int32, (1, LANES), 1)
    head0 = (lane % RET_QK_DIM) < (RET_QK_DIM // 2)
    head_masks = (head0, jnp.logical_not(head0))
    st_row = lax.broadcasted_iota(jnp.int32, st_ref.shape, 0)
    st_col = lax.broadcasted_iota(jnp.int32, st_ref.shape, 1)
    own_head = ((st_row % RET_QK_DIM) // (RET_QK_DIM // 2)) == (st_col // RET_V_DIM)
    heads = range(2)
    nt = (((1,), (1,)), ((), ()))
    tn = (((0,), (0,)), ((), ()))
    st_ref[...] = jnp.zeros_like(st_ref)

    def step(n, carry):
        subs = range(RET_UNROLL)
        rows = [pl.ds(pl.multiple_of((n * RET_UNROLL + u) * RET_BLK, RET_BLK), RET_BLK) for u in subs]
        col = [slice(h * RET_V_DIM, (h + 1) * RET_V_DIM) for h in heads]
        q_blk = [q_ref[r, :] for r in rows]
        k_blk = [k_ref[r, :] for r in rows]
        qm = [[jnp.where(head_masks[h], q_blk[u], jnp.zeros_like(q_blk[u])) for h in heads] for u in subs]
        v = [v_ref[r, :] for r in rows]
        scores = [[lax.dot_general(qm[u][h], k_blk[u], nt, preferred_element_type=F32)
                   for h in heads] for u in subs]
        kd = [(k_blk[u].astype(F32) * kd_ref[0]).astype(BF16) for u in subs]
        kv = [jnp.where(own_head, lax.dot_general(kd[u], v[u], tn, preferred_element_type=F32), 0.0)
              for u in subs]
        qd = [(q_blk[u].astype(F32) * qd_ref[0]).astype(BF16) for u in subs]
        p = [[(scores[u][h] * dm_ref[0, h]).astype(BF16) for h in heads] for u in subs]
        intra = [[jnp.dot(p[u][h], v[u][:, col[h]], preferred_element_type=F32) for h in heads]
                 for u in subs]
        state = st_ref[...]
        for u in subs:
            cross = jnp.dot(qd[u], state.astype(BF16), preferred_element_type=F32)
            state = state * cd_ref[0] + kv[u]
            for h in heads:
                y = intra[u][h] + cross[:, col[h]]
                ms = jnp.mean(y * y, axis=-1, keepdims=True)
                yn = (y * lax.rsqrt(ms + EPS)) * gn_ref[:, col[h]]
                o_ref[rows[u], col[h]] = (g_ref[rows[u], col[h]].astype(F32) * yn).astype(BF16)
        st_ref[...] = state
        return carry

    lax.fori_loop(0, seq // (RET_BLK * RET_UNROLL), step, 0)


def _ret_call(rqk, plain, swish, ret_norm_g, seq):
    t = rqk.shape[0]
    n_pairs = RET_HEADS // 2
    dmat, qdec, kdec, cdec = _ret_tables()
    k0 = RET_QK_WIDTH // LANES
    v0 = (3 * SB_WIDTH) // (2 * RET_V_DIM)
    return pl.pallas_call(
        _ret_kernel,
        grid=(t // seq, n_pairs),
        in_specs=[
            pl.BlockSpec((seq, LANES), lambda b, p: (b, p)),
            pl.BlockSpec((seq, LANES), lambda b, p: (b, k0 + p)),
            pl.BlockSpec((seq, 2 * RET_V_DIM), lambda b, p: (b, v0 + p)),
            pl.BlockSpec((seq, 2 * RET_V_DIM), lambda b, p: (b, p)),
            pl.BlockSpec((1, 2 * RET_V_DIM), lambda b, p: (0, p)),
            pl.BlockSpec((1, 2, RET_BLK, RET_BLK), lambda b, p: (p, 0, 0, 0)),
            pl.BlockSpec((1, RET_BLK, LANES), lambda b, p: (p, 0, 0)),
            pl.BlockSpec((1, RET_BLK, LANES), lambda b, p: (p, 0, 0)),
            pl.BlockSpec((1, 1, 2 * RET_V_DIM), lambda b, p: (p, 0, 0)),
        ],
        out_specs=pl.BlockSpec((seq, 2 * RET_V_DIM), lambda b, p: (b, p)),
        out_shape=jax.ShapeDtypeStruct((t, RET_V_WIDTH), BF16),
        scratch_shapes=[pltpu.VMEM((LANES, 2 * RET_V_DIM), F32)],
        compiler_params=pltpu.CompilerParams(
            dimension_semantics=("arbitrary", "arbitrary"), vmem_limit_bytes=VMEM_LIMIT),
        name="retention",
    )(rqk, rqk, plain, swish, ret_norm_g, dmat, qdec, kdec, cdec)


def _mix_kernel(ysb_ref, yret_ref, gs0_ref, gs1_ref, gr0_ref, gr1_ref, x_ref,
                wsb_ref, wret_ref, wout_ref, gffn_ref, wrh_ref, wrl_ref, br_ref, tri_ref,
                hext_ref, cls_ref, cnt_ref):
    tm = x_ref.shape[0]

    @pl.when(pl.program_id(0) == 0)
    def _():
        cnt_ref[...] = jnp.zeros_like(cnt_ref)

    a = jnp.dot(ysb_ref[...], wsb_ref[...], preferred_element_type=F32)
    b = jnp.dot(yret_ref[...], wret_ref[...], preferred_element_type=F32)
    g_sb = jnp.concatenate([gs0_ref[...], gs1_ref[...]], axis=1).astype(F32)
    g_ret = jnp.concatenate([gr0_ref[...], gr1_ref[...]], axis=1).astype(F32)
    mixed = (g_sb * a + g_ret * b).astype(BF16)
    h = x_ref[...] + jnp.dot(mixed, wout_ref[...], preferred_element_type=F32)
    hext_ref[:, 0:D_MODEL] = h

    ms = jnp.mean(h * h, axis=-1, keepdims=True)
    hn = (h * lax.rsqrt(ms + EPS)) * gffn_ref[...]
    hn_hi = hn.astype(BF16)
    hn_lo = (hn - hn_hi.astype(F32)).astype(BF16)
    nt = (((1,), (1,)), ((), ()))
    w_both = jnp.concatenate([wrh_ref[...], wrl_ref[...]], axis=0)
    by_hi = lax.dot_general(w_both, hn_hi, nt, preferred_element_type=F32)
    logits = (by_hi[0:ROUTER_ROWS] + by_hi[ROUTER_ROWS:2 * ROUTER_ROWS]
              + lax.dot_general(wrh_ref[...], hn_lo, nt, preferred_element_type=F32)
              + br_ref[...])

    def first_argmax(vals):
        m = functools.reduce(jnp.maximum, vals)
        idx = jnp.full(m.shape, len(vals) - 1, jnp.int32)
        for i in range(len(vals) - 2, -1, -1):
            idx = jnp.where(vals[i] >= m, i, idx)
        return m, idx

    gl = [logits[r:r + 1, :] for r in range(N_GROUPS)]
    gmax, gsel = first_argmax(gl)
    p_group = 1.0 / functools.reduce(lambda s, v: s + v, [jnp.exp(v - gmax) for v in gl])
    el = []
    for e in range(EXPERTS_PER_GROUP):
        v = logits[N_GROUPS + 3 * EXPERTS_PER_GROUP + e:N_GROUPS + 3 * EXPERTS_PER_GROUP + e + 1, :]
        for g in range(N_GROUPS - 2, -1, -1):
            r = N_GROUPS + g * EXPERTS_PER_GROUP + e
            v = jnp.where(gsel == g, logits[r:r + 1, :], v)
        el.append(v)
    m1, i1 = first_argmax(el)
    rest = [jnp.where(i1 == e, -jnp.inf, el[e]) for e in range(EXPERTS_PER_GROUP)]
    m2, i2 = first_argmax(rest)
    tt = jnp.exp(m2 - m1)
    gate1 = p_group * (1.0 / (1.0 + tt))
    gate2 = p_group * (tt / (1.0 + tt))
    lo_first = i1 < i2
    ea = jnp.minimum(i1, i2)
    eb = jnp.maximum(i1, i2)
    w_a = jnp.where(lo_first, gate1, gate2)
    w_b = jnp.where(lo_first, gate2, gate1)
    pid = jnp.where(ea == 0, eb - 1, jnp.where(ea == 1, eb + 1, PAIRS_PER_GROUP - 1))
    cls = gsel * PAIRS_PER_GROUP + pid
    info = jnp.concatenate([w_a, w_b, jnp.zeros((INFO_COLS - 2, tm), F32)], axis=0)
    hext_ref[:, D_MODEL:HEXT_COLS] = info.T

    class_row = lax.broadcasted_iota(jnp.int32, (ROUTER_ROWS, tm), 0)
    onehot = class_row == cls
    onehot_bf = onehot.astype(BF16)
    before = jnp.dot(onehot_bf, tri_ref[...], preferred_element_type=F32)
    cnt = cnt_ref[...]
    seen = before + jnp.concatenate([cnt] * (tm // LANES), axis=1)
    rank = jnp.sum(jnp.where(onehot, seen, 0.0), axis=0, keepdims=True)
    cnt_ref[...] = cnt + jnp.dot(onehot_bf, jnp.ones((tm, LANES), BF16), preferred_element_type=F32)
    cls_ref[...] = jnp.concatenate(
        [cls, rank.astype(jnp.int32), jnp.zeros((cls_ref.shape[0] - 2, tm), jnp.int32)], axis=0)


def _mix_call(ysb, yret, gates, x2, wsb, wret, wout, gffn, wr_hi, wr_lo, br):
    t = x2.shape[0]
    tm = MIX_TM
    gate0 = 0
    const = lambda i: (0, 0)
    idx = jnp.arange(tm, dtype=jnp.int32)
    tri = (idx[:, None] < idx[None, :]).astype(BF16)
    return pl.pallas_call(
        _mix_kernel,
        grid=(t // tm,),
        in_specs=[
            pl.BlockSpec((tm, SB_WIDTH), lambda i: (i, 0)),
            pl.BlockSpec((tm, RET_V_WIDTH), lambda i: (i, 0)),
            pl.BlockSpec((tm, GATE_BLK), lambda i: (i, gate0)),
            pl.BlockSpec((tm, GATE_BLK), lambda i: (i, gate0 + 1)),
            pl.BlockSpec((tm, GATE_BLK), lambda i: (i, gate0 + 2)),
            pl.BlockSpec((tm, GATE_BLK), lambda i: (i, gate0 + 3)),
            pl.BlockSpec((tm, D_MODEL), lambda i: (i, 0)),
            pl.BlockSpec((SB_WIDTH, D_MODEL), const),
            pl.BlockSpec((RET_V_WIDTH, D_MODEL), const),
            pl.BlockSpec((D_MODEL, D_MODEL), const),
            pl.BlockSpec((1, D_MODEL), const),
            pl.BlockSpec((ROUTER_ROWS, D_MODEL), const),
            pl.BlockSpec((ROUTER_ROWS, D_MODEL), const),
            pl.BlockSpec((ROUTER_ROWS, 1), const),
            pl.BlockSpec((tm, tm), const),
        ],
        out_specs=[
            pl.BlockSpec((tm, HEXT_COLS), lambda i: (i, 0)),
            pl.BlockSpec((8, tm), lambda i: (0, i)),
            pl.BlockSpec((ROUTER_ROWS, LANES), const),
        ],
        out_shape=[
            jax.ShapeDtypeStruct((t, HEXT_COLS), F32),
            jax.ShapeDtypeStruct((8, t), jnp.int32),
            jax.ShapeDtypeStruct((ROUTER_ROWS, LANES), F32),
        ],
        compiler_params=pltpu.CompilerParams(
            dimension_semantics=("arbitrary",), vmem_limit_bytes=VMEM_LIMIT),
        name="mix_router",
    )(ysb, yret, gates, gates, gates, gates, x2, wsb, wret, wout, gffn, wr_hi, wr_lo, br, tri)


def _moe_kernel(dest_ref, nvalid_ref, ea_ref, eb_ref,
                hext_ref, wga_ref, wua_ref, wda_ref, wgb_ref, wub_ref, wdb_ref,
                gffn_ref, gfin_ref, out_ref, hbuf, obuf, src, gsem, ssem):
    i = pl.program_id(0)
    n_tiles = pl.num_programs(0)
    tm = hbuf.shape[1]
    nv = nvalid_ref[i]
    nxt = jnp.minimum(i + 1, n_tiles - 1)
    nv_next = jnp.where(i + 1 < n_tiles, nvalid_ref[nxt], 0)
    has_next = nv_next > 0

    def gather_copy(tile, buf_slot, r):
        tok = src[tile * tm + r]
        return pltpu.make_async_copy(hext_ref.at[pl.ds(tok, 1), :],
                                     hbuf.at[buf_slot, pl.ds(r, 1), :], gsem.at[buf_slot])

    def scatter_copy(tile, buf_slot, r):
        tok = src[tile * tm + r]
        return pltpu.make_async_copy(obuf.at[buf_slot, pl.ds(r, 1), :],
                                     out_ref.at[pl.ds(tok, 1), :], ssem.at[buf_slot])

    def start_rows(copy_of_row, rows_valid):
        for r in range(tm):
            @pl.when(r < rows_valid)
            def _():
                copy_of_row(r).start(priority=r % 2)

    def wait_rows(full_copy, row_copy, rows_valid):
        @pl.when(rows_valid == tm)
        def _():
            full_copy.wait()

        @pl.when(rows_valid < tm)
        def _():
            for r in range(tm):
                @pl.when(r < rows_valid)
                def _():
                    row_copy(r).wait()

    def wait_gather(buf_slot, rows_valid):
        wait_rows(pltpu.make_async_copy(hext_ref.at[pl.ds(0, tm), :], hbuf.at[buf_slot], gsem.at[buf_slot]),
                  lambda r: pltpu.make_async_copy(hext_ref.at[pl.ds(0, 1), :],
                                                  hbuf.at[buf_slot, pl.ds(r, 1), :], gsem.at[buf_slot]),
                  rows_valid)

    def wait_scatter(buf_slot, rows_valid):
        wait_rows(pltpu.make_async_copy(obuf.at[buf_slot], out_ref.at[pl.ds(0, tm), :], ssem.at[buf_slot]),
                  lambda r: pltpu.make_async_copy(obuf.at[buf_slot, pl.ds(r, 1), :],
                                                  out_ref.at[pl.ds(0, 1), :], ssem.at[buf_slot]),
                  rows_valid)

    @pl.when(i == 0)
    def _():
        def invert(c, carry):
            for u in range(MOE_INVERT_UNROLL):
                t = c * MOE_INVERT_UNROLL + u
                src[dest_ref[t]] = t
            return carry

        lax.fori_loop(0, dest_ref.shape[0] // MOE_INVERT_UNROLL, invert, 0)
        hbuf[...] = jnp.zeros_like(hbuf)
        start_rows(functools.partial(gather_copy, 0, 0), nv)

    def tile_body(slot):
        @pl.when(i >= 2)
        def _():
            wait_scatter(slot, nvalid_ref[jnp.maximum(i - 2, 0)])

        wait_gather(slot, nv)
        start_rows(functools.partial(gather_copy, nxt, 1 - slot), nv_next)
        hrows = hbuf[slot]
        h = hrows[:, 0:D_MODEL]
        w_a = hrows[:, D_MODEL:D_MODEL + 1]
        w_b = hrows[:, D_MODEL + 1:D_MODEL + 2]
        ms = jnp.mean(h * h, axis=-1, keepdims=True)
        hn = ((h * lax.rsqrt(ms + EPS)) * gffn_ref[...]).astype(BF16)

        def expert(wg_ref, wu_ref, wd_ref):
            gate = jnp.dot(hn, wg_ref[0], preferred_element_type=F32)
            up = jnp.dot(hn, wu_ref[0], preferred_element_type=F32)
            hidden = ((gate * jax.nn.sigmoid(gate)) * up).astype(BF16)
            return jnp.dot(hidden, wd_ref[0], preferred_element_type=F32)

        y = w_a * expert(wga_ref, wua_ref, wda_ref) + w_b * expert(wgb_ref, wub_ref, wdb_ref)
        h2 = h + y
        ms2 = jnp.mean(h2 * h2, axis=-1, keepdims=True)
        obuf[slot] = (h2 * lax.rsqrt(ms2 + EPS)) * gfin_ref[...]
        start_rows(functools.partial(scatter_copy, i, slot), nv)

        @pl.when(jnp.logical_not(has_next))
        def _():
            @pl.when(i >= 1)
            def _():
                wait_scatter(1 - slot, nvalid_ref[jnp.maximum(i - 1, 0)])

            wait_scatter(slot, nv)

    for parity in range(2):
        pl.when((nv > 0) & (i % 2 == parity))(functools.partial(tile_body, parity))


def _moe_call(dest, nvalid, ea, eb, hext, wg, wu, wd, gffn, gfin):
    t = hext.shape[0]
    tm = MOE_TM
    n_tiles = nvalid.shape[0]
    assert t % MOE_INVERT_UNROLL == 0
    wa_map = lambda i, dest, nv, ea, eb: (ea[i], 0, 0)
    wb_map = lambda i, dest, nv, ea, eb: (eb[i], 0, 0)
    const = lambda i, dest, nv, ea, eb: (0, 0)
    return pl.pallas_call(
        _moe_kernel,
        grid_spec=pltpu.PrefetchScalarGridSpec(
            num_scalar_prefetch=4,
            grid=(n_tiles,),
            in_specs=[
                pl.BlockSpec(memory_space=pl.ANY),
                pl.BlockSpec((1, D_MODEL, D_FF), wa_map),
                pl.BlockSpec((1, D_MODEL, D_FF), wa_map),
                pl.BlockSpec((1, D_FF, D_MODEL), wa_map),
                pl.BlockSpec((1, D_MODEL, D_FF), wb_map),
                pl.BlockSpec((1, D_MODEL, D_FF), wb_map),
                pl.BlockSpec((1, D_FF, D_MODEL), wb_map),
                pl.BlockSpec((1, D_MODEL), const),
                pl.BlockSpec((1, D_MODEL), const),
            ],
            out_specs=pl.BlockSpec(memory_space=pl.ANY),
            scratch_shapes=[
                pltpu.VMEM((2, tm, HEXT_COLS), F32),
                pltpu.VMEM((2, tm, D_MODEL), F32),
                pltpu.SMEM((n_tiles * tm,), jnp.int32),
                pltpu.SemaphoreType.DMA((2,)),
                pltpu.SemaphoreType.DMA((2,)),
            ],
        ),
        out_shape=jax.ShapeDtypeStruct((t, D_MODEL), F32),
        compiler_params=pltpu.CompilerParams(
            dimension_semantics=("arbitrary",), vmem_limit_bytes=VMEM_LIMIT),
        name="experts",
    )(dest, nvalid, ea, eb, hext, wg, wu, wd, wg, wu, wd, gffn, gfin)


def _routing_plan(cls, rank, counts, tm):
    t = cls.shape[0]
    n_tiles = t // tm + N_CLASSES
    tiles_c = (counts + tm - 1) // tm
    tile_end = jnp.cumsum(tiles_c)
    tile_off = tile_end - tiles_c
    onehot = cls[:, None] == jnp.arange(N_CLASSES, dtype=jnp.int32)[None, :]
    dest = jnp.sum(jnp.where(onehot, tile_off[None, :], 0), axis=1) * tm + rank
    tile = jnp.arange(n_tiles, dtype=jnp.int32)
    used = tile < tile_end[-1]
    tile_cls = jnp.minimum(jnp.sum((tile[:, None] >= tile_end[None, :]).astype(jnp.int32), axis=1),
                           N_CLASSES - 1)
    cls_onehot = tile_cls[:, None] == jnp.arange(N_CLASSES, dtype=jnp.int32)[None, :]
    pick = lambda table: jnp.sum(jnp.where(cls_onehot, table[None, :], 0), axis=1)
    nvalid = jnp.where(used, jnp.clip(pick(counts) - (tile - pick(tile_off)) * tm, 0, tm), 0)
    classes = np.arange(N_CLASSES)
    ea_tab = jnp.asarray((classes // PAIRS_PER_GROUP) * EXPERTS_PER_GROUP
                         + np.asarray(_PAIR_A)[classes % PAIRS_PER_GROUP], jnp.int32)
    eb_tab = jnp.asarray((classes // PAIRS_PER_GROUP) * EXPERTS_PER_GROUP
                         + np.asarray(_PAIR_B)[classes % PAIRS_PER_GROUP], jnp.int32)
    last_used = jnp.sum(jnp.where(tile == tile_end[-1] - 1, tile_cls, 0))
    tile_cls = jnp.where(used, tile_cls, last_used)
    cls_onehot = tile_cls[:, None] == jnp.arange(N_CLASSES, dtype=jnp.int32)[None, :]
    return dest.astype(jnp.int32), nvalid.astype(jnp.int32), pick(ea_tab), pick(eb_tab)


def _rope_tables(seq):
    half = RET_QK_DIM // 2
    inv_freq = ROPE_BASE ** (-jnp.arange(half, dtype=F32) / half)
    ang = jnp.arange(seq, dtype=F32)[:, None] * inv_freq[None, :]
    cos, sin = jnp.cos(ang), jnp.sin(ang)
    cos_t = jnp.tile(cos, (1, LANES // half))
    sin_t = jnp.concatenate([-sin, -sin, sin, sin], axis=1)
    return cos_t, sin_t


def kernel(x, norm_mix_g, w_in, w_gate, b_gate, w_sb_out, w_ret_out, ret_norm_g, w_out,
           norm_ffn_g, w_group_router, b_group_router, w_expert_router, b_expert_router,
           w_exp_gate, w_exp_up, w_exp_down, norm_final_g):
    bsz, seq, d = x.shape
    t = bsz * seq
    assert d == D_MODEL and w_in.shape[0] == 1 and t % MIX_TM == 0
    assert seq % (SB_GROUP * SB_BLK) == 0 and seq % (RET_BLK * RET_UNROLL) == 0
    x2 = x.reshape(t, d)
    wi = w_in[0]
    c_rq = 3 * SB_WIDTH
    c_rk = c_rq + RET_QK_WIDTH
    c_rv = c_rk + RET_QK_WIDTH
    c_rg = c_rv + RET_V_WIDTH
    ones = functools.partial(jnp.ones, dtype=F32)
    w_plain = jnp.concatenate([wi[:, :c_rq], wi[:, c_rv:c_rg]], axis=1).astype(BF16)
    s_plain = jnp.concatenate([jnp.full((SB_WIDTH,), SB_HEAD_DIM ** -0.5 * LOG2E, F32),
                               ones((2 * SB_WIDTH + RET_V_WIDTH,))])[None, :]
    perm = _rot_perm()
    w_rot = jnp.concatenate([wi[:, c_rq:c_rk][:, perm],
                             (RET_QK_DIM ** -0.5) * wi[:, c_rk:c_rv][:, perm]], axis=1).astype(BF16)
    w_swish_half = (0.5 * wi[:, c_rg:]).astype(BF16)
    w_gate_half = (0.5 * w_gate[0]).astype(BF16)
    b_gate_half = (0.5 * b_gate[0])[None, :]
    cos_t, sin_t = _rope_tables(seq)

    plain, xn = _proj_plain_call(x2, norm_mix_g[0][None, :], w_plain, s_plain, seq)
    rqk = _proj_rot_call(xn, w_rot, cos_t, sin_t, seq)
    swish = _proj_swish_call(xn, w_swish_half, seq)
    gates = _proj_gate_call(xn, w_gate_half, b_gate_half, seq)
    ysb = _sb_call(plain, seq)
    yret = _ret_call(rqk, plain, swish, ret_norm_g[0][None, :], seq)

    wr = jnp.concatenate([w_group_router[0], w_expert_router[0]], axis=1).T
    wr = jnp.pad(wr, ((0, ROUTER_ROWS - wr.shape[0]), (0, 0)))
    wr_hi = wr.astype(BF16)
    wr_lo = (wr - wr_hi.astype(F32)).astype(BF16)
    br = jnp.concatenate([b_group_router[0], b_expert_router[0]])
    br = jnp.pad(br, (0, ROUTER_ROWS - br.shape[0]))[:, None]
    hext, cls8, cnt = _mix_call(ysb, yret, gates, x2, w_sb_out[0].astype(BF16), w_ret_out[0].astype(BF16),
                           w_out[0].astype(BF16), norm_ffn_g[0][None, :], wr_hi, wr_lo, br)

    counts = cnt[:N_CLASSES, 0].astype(jnp.int32)
    dest, nvalid, ea, eb = _routing_plan(cls8[0], cls8[1], counts, MOE_TM)
    out = _moe_call(dest, nvalid, ea, eb, hext, w_exp_gate[0].astype(BF16), w_exp_up[0].astype(BF16),
                    w_exp_down[0].astype(BF16), norm_ffn_g[0][None, :], norm_final_g[None, :])
    return out.reshape(bsz, seq, d)
```

```python
import functools
import math

import numpy as np
import jax
import jax.numpy as jnp
from jax import lax
from jax.experimental import pallas as pl
from jax.experimental.pallas import tpu as pltpu

F32 = jnp.float32
BF16 = jnp.bfloat16

D_MODEL = 1024
SB_HEADS = 8
SB_HEAD_DIM = 64
SB_WIDTH = SB_HEADS * SB_HEAD_DIM
RET_HEADS = 8
RET_QK_DIM = 64
RET_V_DIM = 128
RET_QK_WIDTH = RET_HEADS * RET_QK_DIM
RET_V_WIDTH = RET_HEADS * RET_V_DIM
IN_COLS = 3 * SB_WIDTH + 2 * RET_QK_WIDTH + 2 * RET_V_WIDTH
GATE_COLS = 2 * D_MODEL
ALL_COLS = IN_COLS + GATE_COLS
CHUNK = 64
ROPE_BASE = 10000.0
N_GROUPS = 4
EXPERTS_PER_GROUP = 4
N_EXPERTS = N_GROUPS * EXPERTS_PER_GROUP
D_FF = 512
EPS = 1e-6

LANES = 128
PROJ_TN = 1280
GATE_BLK = 512
PROJ_ROWS = 256
SB_BLK = 256
SB_GROUP = 4
RET_BLK = 256
RET_UNROLL = 4
MIX_TM = 512
MOE_TM = 256
MOE_INVERT_UNROLL = 32
ROUTER_ROWS = 32
PAIRS_PER_GROUP = 6
N_CLASSES = N_GROUPS * PAIRS_PER_GROUP
INFO_COLS = LANES
HEXT_COLS = D_MODEL + INFO_COLS
VMEM_LIMIT = 56 * 1024 * 1024
LOG2E = math.log2(math.e)
SB_SKIP_LOG2 = 156.0

_PAIR_A = (0, 0, 0, 1, 1, 2)
_PAIR_B = (1, 2, 3, 2, 3, 3)


def _proj_plain_kernel(x_ref, g_ref, w_ref, s_ref, o_ref, xn_ref):
    @pl.when(pl.program_id(1) == 0)
    def _():
        x = x_ref[...]
        ms = jnp.mean(x * x, axis=-1, keepdims=True)
        xn_ref[...] = ((x * lax.rsqrt(ms + EPS)) * g_ref[...]).astype(BF16)

    for rows in _row_chunks(xn_ref.shape[0]):
        acc = jnp.dot(xn_ref[rows, :], w_ref[...], preferred_element_type=F32)
        o_ref[rows, :] = (acc * s_ref[...]).astype(BF16)


def _row_chunks(n_rows):
    return [slice(r, r + PROJ_ROWS) for r in range(0, n_rows, PROJ_ROWS)]


def _proj_rot_kernel(xn_ref, w_ref, cos_ref, sin_ref, o_ref):
    for rows in _row_chunks(xn_ref.shape[0]):
        acc = jnp.dot(xn_ref[rows, :], w_ref[...], preferred_element_type=F32)
        cos = cos_ref[rows, :]
        sin = sin_ref[rows, :]
        for p in range(w_ref.shape[1] // LANES):
            cols = slice(p * LANES, (p + 1) * LANES)
            seg = acc[:, cols]
            o_ref[rows, cols] = (seg * cos + pltpu.roll(seg, LANES // 2, 1) * sin).astype(BF16)


def _proj_swish_kernel(xn_ref, w_ref, o_ref):
    for rows in _row_chunks(xn_ref.shape[0]):
        half = jnp.dot(xn_ref[rows, :], w_ref[...], preferred_element_type=F32)
        o_ref[rows, :] = (half * (jnp.tanh(half) + 1.0)).astype(BF16)


def _proj_gate_kernel(xn_ref, w_ref, b_ref, o_ref):
    for rows in _row_chunks(xn_ref.shape[0]):
        half = jnp.dot(xn_ref[rows, :], w_ref[...], preferred_element_type=F32) + b_ref[...]
        o_ref[rows, :] = (0.5 * jnp.tanh(half) + 0.5).astype(BF16)


def _proj_params():
    return pltpu.CompilerParams(dimension_semantics=("arbitrary", "arbitrary"),
                                vmem_limit_bytes=VMEM_LIMIT)


def _proj_tile(n):
    tn = PROJ_TN
    while n % tn:
        tn -= LANES
    return tn


def _proj_plain_call(x2, g, w, scale, seq):
    t, n = x2.shape[0], w.shape[1]
    tn = _proj_tile(n)
    return pl.pallas_call(
        _proj_plain_kernel,
        grid=(t // seq, n // tn),
        in_specs=[
            pl.BlockSpec((seq, D_MODEL), lambda i, j: (i, 0)),
            pl.BlockSpec((1, D_MODEL), lambda i, j: (0, 0)),
            pl.BlockSpec((D_MODEL, tn), lambda i, j: (0, j)),
            pl.BlockSpec((1, tn), lambda i, j: (0, j)),
        ],
        out_specs=[
            pl.BlockSpec((seq, tn), lambda i, j: (i, j)),
            pl.BlockSpec((seq, D_MODEL), lambda i, j: (i, 0)),
        ],
        out_shape=[jax.ShapeDtypeStruct((t, n), BF16), jax.ShapeDtypeStruct((t, D_MODEL), BF16)],
        compiler_params=_proj_params(),
        name="proj_plain",
    )(x2, g, w, scale)


def _proj_rot_call(xn, w, cos_t, sin_t, seq):
    t, n = xn.shape[0], w.shape[1]
    tn = _proj_tile(n)
    return pl.pallas_call(
        _proj_rot_kernel,
        grid=(t // seq, n // tn),
        in_specs=[
            pl.BlockSpec((seq, D_MODEL), lambda i, j: (i, 0)),
            pl.BlockSpec((D_MODEL, tn), lambda i, j: (0, j)),
            pl.BlockSpec((seq, LANES), lambda i, j: (0, 0)),
            pl.BlockSpec((seq, LANES), lambda i, j: (0, 0)),
        ],
        out_specs=pl.BlockSpec((seq, tn), lambda i, j: (i, j)),
        out_shape=jax.ShapeDtypeStruct((t, n), BF16),
        compiler_params=_proj_params(),
        name="proj_rotary",
    )(xn, w, cos_t, sin_t)


def _proj_swish_call(xn, w_half, seq):
    t, n = xn.shape[0], w_half.shape[1]
    tn = _proj_tile(n)
    return pl.pallas_call(
        _proj_swish_kernel,
        grid=(t // seq, n // tn),
        in_specs=[
            pl.BlockSpec((seq, D_MODEL), lambda i, j: (i, 0)),
            pl.BlockSpec((D_MODEL, tn), lambda i, j: (0, j)),
        ],
        out_specs=pl.BlockSpec((seq, tn), lambda i, j: (i, j)),
        out_shape=jax.ShapeDtypeStruct((t, n), BF16),
        compiler_params=_proj_params(),
        name="proj_swish",
    )(xn, w_half)


def _proj_gate_call(xn, w_half, bias_half, seq):
    t, n = xn.shape[0], w_half.shape[1]
    tn = _proj_tile(n)
    return pl.pallas_call(
        _proj_gate_kernel,
        grid=(t // seq, n // tn),
        in_specs=[
            pl.BlockSpec((seq, D_MODEL), lambda i, j: (i, 0)),
            pl.BlockSpec((D_MODEL, tn), lambda i, j: (0, j)),
            pl.BlockSpec((1, tn), lambda i, j: (0, j)),
        ],
        out_specs=pl.BlockSpec((seq, tn), lambda i, j: (i, j)),
        out_shape=jax.ShapeDtypeStruct((t, n), BF16),
        compiler_params=_proj_params(),
        name="proj_gate",
    )(xn, w_half, bias_half)


def _rot_perm():
    half = RET_QK_DIM // 2
    order = [p * LANES + hh * RET_QK_DIM + part * half + d
             for p in range(RET_QK_WIDTH // LANES) for part in range(2) for hh in range(2)
             for d in range(half)]
    return np.asarray(order, np.int32)


def _sb_kernel(q_ref, k_ref, v_ref, o_ref, u_ref, acc_ref, car_ref):
    seq = q_ref.shape[0]
    nq = seq // SB_BLK
    row = lax.broadcasted_iota(jnp.int32, (SB_BLK, SB_BLK), 0)
    col = lax.broadcasted_iota(jnp.int32, (SB_BLK, SB_BLK), 1)
    u_ref[...] = (row >= col).astype(BF16)
    lane = lax.broadcasted_iota(jnp.int32, (1, LANES), 1)
    head_masks = (lane < SB_HEAD_DIM, lane >= SB_HEAD_DIM)

    heads = range(2)
    causal = col < row

    def rows_of(blk):
        return pl.ds(pl.multiple_of(blk * SB_BLK, SB_BLK), SB_BLK)

    def q_heads_of(qi):
        q_blk = q_ref[rows_of(qi), :]
        return tuple(jnp.where(m, q_blk, jnp.zeros_like(q_blk)) for m in head_masks)

    def sweep(jobs):
        u = u_ref[...]
        q_heads = [q_heads_of(qi) for _, qi, _, _ in jobs]
        z2 = [[[lax.dot_general(q_heads[j][h], k_ref[rows_of(kb), :], (((1,), (1,)), ((), ())),
                                preferred_element_type=F32) for h in heads]
               for kb, _ in steps] for j, (_, _, steps, _) in enumerate(jobs)]
        sinc = []
        for j, (_, _, steps, _) in enumerate(jobs):
            sinc.append([])
            for i, (_, diagonal) in enumerate(steps):
                nlk = [jnp.maximum(z, 0.0) + jnp.log2(1.0 + jnp.exp2(-jnp.abs(z))) for z in z2[j][i]]
                if diagonal:
                    nlk = [jnp.where(causal, a, 0.0) for a in nlk]
                sinc[j].append([jnp.dot(a.astype(BF16), u, preferred_element_type=F32) for a in nlk])
        for j, (slot, _, steps, fresh) in enumerate(jobs):
            car = [None, None] if fresh else [car_ref[slot, h] for h in heads]
            contrib = None
            for i, (kb, diagonal) in enumerate(steps):
                v_blk = v_ref[rows_of(kb), :]
                for h in heads:
                    e = z2[j][i][h] - sinc[j][i][h]
                    if car[h] is not None:
                        e = e - jnp.concatenate([car[h], car[h]], axis=1)
                    w = jnp.exp2(e)
                    if diagonal:
                        w = jnp.where(causal, w, 0.0)
                    vh = jnp.where(head_masks[h], v_blk, jnp.zeros_like(v_blk))
                    pv = jnp.dot(w.astype(BF16), vh, preferred_element_type=F32)
                    contrib = pv if contrib is None else contrib + pv
                    tot = jnp.broadcast_to(sinc[j][i][h][:, 0:1], (SB_BLK, LANES))
                    car[h] = tot if car[h] is None else car[h] + tot
            for h in heads:
                car_ref[slot, h] = car[h]
            if fresh:
                acc_ref[slot] = contrib
            else:
                acc_ref[slot] += contrib

    def finish(slot, qi, first_kb):
        if first_kb is not None:
            def more(c):
                kb, min_carry = c
                return (kb >= 0) & (min_carry < SB_SKIP_LOG2)

            def k_step(c):
                kb, _ = c
                sweep([(slot, qi, [(kb, False)], False)])
                return kb - 1, jnp.min(car_ref[slot])

            lax.while_loop(more, k_step, (first_kb, jnp.min(car_ref[slot])))
        o_ref[rows_of(qi), :] = acc_ref[slot].astype(BF16)

    def first_steps(qi):
        return [(qi, True)] + ([(qi - 1, False)] if qi > 0 else [])

    sweep([(s, s, first_steps(s), True) for s in range(SB_GROUP)])
    for s in range(SB_GROUP):
        finish(s, s, s - 2 if s >= 2 else None)

    def q_group(g, carry):
        q0 = g * SB_GROUP
        sweep([(s, q0 + s, [(q0 + s, True), (q0 + s - 1, False)], True) for s in range(SB_GROUP)])
        for s in range(SB_GROUP):
            finish(s, q0 + s, q0 + s - 2)
        return carry

    lax.fori_loop(1, nq // SB_GROUP, q_group, 0)


def _sb_call(proj, seq):
    t = proj.shape[0]
    n_pairs = SB_WIDTH // LANES
    return pl.pallas_call(
        _sb_kernel,
        grid=(t // seq, n_pairs),
        in_specs=[
            pl.BlockSpec((seq, LANES), lambda b, p: (b, p)),
            pl.BlockSpec((seq, LANES), lambda b, p: (b, n_pairs + p)),
            pl.BlockSpec((seq, LANES), lambda b, p: (b, 2 * n_pairs + p)),
        ],
        out_specs=pl.BlockSpec((seq, LANES), lambda b, p: (b, p)),
        out_shape=jax.ShapeDtypeStruct((t, SB_WIDTH), BF16),
        scratch_shapes=[
            pltpu.VMEM((SB_BLK, SB_BLK), BF16),
            pltpu.VMEM((SB_GROUP, SB_BLK, LANES), F32),
            pltpu.VMEM((SB_GROUP, 2, SB_BLK, LANES), F32),
        ],
        compiler_params=pltpu.CompilerParams(
            dimension_semantics=("arbitrary", "arbitrary"), vmem_limit_bytes=VMEM_LIMIT),
        name="stickbreak",
    )(proj, proj, proj)


def _ret_tables():
    h = np.arange(RET_HEADS, dtype=np.float64)
    log_gamma = np.log(1.0 - 2.0 ** (-5.0 - h))
    idx = np.arange(RET_BLK, dtype=np.float64)
    t, s = idx[:, None], idx[None, :]
    same = (t // CHUNK) == (s // CHUNK)
    earlier = (s // CHUNK) < (t // CHUNK)
    expo = np.where(same, np.abs(t - s), np.where(earlier, t - s, 0.0))
    dmat = np.exp(log_gamma[:, None, None] * expo) * (same | earlier)
    qdec = np.exp(log_gamma[:, None] * (idx + 1.0)[None, :])
    kdec = np.exp(log_gamma[:, None] * (RET_BLK - 1.0 - idx)[None, :])
    cdec = np.exp(log_gamma * RET_BLK)
    n_pairs = RET_HEADS // 2
    lane_head = (np.arange(LANES) % RET_QK_DIM) // (RET_QK_DIM // 2)
    col_head = np.arange(2 * RET_V_DIM) // RET_V_DIM
    pair_heads = np.arange(RET_HEADS).reshape(n_pairs, 2)
    qdec_pair = np.stack([qdec[pair_heads[p][lane_head]].T for p in range(n_pairs)])
    kdec_pair = np.stack([kdec[pair_heads[p][lane_head]].T for p in range(n_pairs)])
    cdec_pair = np.stack([cdec[pair_heads[p][col_head]][None, :] for p in range(n_pairs)])
    return (jnp.asarray(dmat.reshape((n_pairs, 2, RET_BLK, RET_BLK)), F32),
            jnp.asarray(qdec_pair, F32), jnp.asarray(kdec_pair, F32), jnp.asarray(cdec_pair, F32))


def _ret_kernel(q_ref, k_ref, v_ref, g_ref, gn_ref, dm_ref, qd_ref, kd_ref, cd_ref, o_ref, st_ref):
    seq = q_ref.shape[0]
    lane = lax.broadcasted_iota(jnp.int32, (1, LANES), 1)
    head0 = (lane % RET_QK_DIM) < (RET_QK_DIM // 2)
    head_masks = (head0, jnp.logical_not(head0))
    st_row = lax.broadcasted_iota(jnp.int32, st_ref.shape, 0)
    st_col = lax.broadcasted_iota(jnp.int32, st_ref.shape, 1)
    own_head = ((st_row % RET_QK_DIM) // (RET_QK_DIM // 2)) == (st_col // RET_V_DIM)
    heads = range(2)
    nt = (((1,), (1,)), ((), ()))
    tn = (((0,), (0,)), ((), ()))
    st_ref[...] = jnp.zeros_like(st_ref)

    def step(n, carry):
        subs = range(RET_UNROLL)
        rows = [pl.ds(pl.multiple_of((n * RET_UNROLL + u) * RET_BLK, RET_BLK), RET_BLK) for u in subs]
        col = [slice(h * RET_V_DIM, (h + 1) * RET_V_DIM) for h in heads]
        q_blk = [q_ref[r, :] for r in rows]
        k_blk = [k_ref[r, :] for r in rows]
        qm = [[jnp.where(head_masks[h], q_blk[u], jnp.zeros_like(q_blk[u])) for h in heads] for u in subs]
        v = [v_ref[r, :] for r in rows]
        scores = [[lax.dot_general(qm[u][h], k_blk[u], nt, preferred_element_type=F32)
                   for h in heads] for u in subs]
        kd = [(k_blk[u].astype(F32) * kd_ref[0]).astype(BF16) for u in subs]
        kv = [jnp.where(own_head, lax.dot_general(kd[u], v[u], tn, preferred_element_type=F32), 0.0)
              for u in subs]
        qd = [(q_blk[u].astype(F32) * qd_ref[0]).astype(BF16) for u in subs]
        p = [[(scores[u][h] * dm_ref[0, h]).astype(BF16) for h in heads] for u in subs]
        intra = [[jnp.dot(p[u][h], v[u][:, col[h]], preferred_element_type=F32) for h in heads]
                 for u in subs]
        state = st_ref[...]
        for u in subs:
            cross = jnp.dot(qd[u], state.astype(BF16), preferred_element_type=F32)
            state = state * cd_ref[0] + kv[u]
            for h in heads:
                y = intra[u][h] + cross[:, col[h]]
                ms = jnp.mean(y * y, axis=-1, keepdims=True)
                yn = (y * lax.rsqrt(ms + EPS)) * gn_ref[:, col[h]]
                o_ref[rows[u], col[h]] = (g_ref[rows[u], col[h]].astype(F32) * yn).astype(BF16)
        st_ref[...] = state
        return carry

    lax.fori_loop(0, seq // (RET_BLK * RET_UNROLL), step, 0)


def _ret_call(rqk, plain, swish, ret_norm_g, seq):
    t = rqk.shape[0]
    n_pairs = RET_HEADS // 2
    dmat, qdec, kdec, cdec = _ret_tables()
    k0 = RET_QK_WIDTH // LANES
    v0 = (3 * SB_WIDTH) // (2 * RET_V_DIM)
    return pl.pallas_call(
        _ret_kernel,
        grid=(t // seq, n_pairs),
        in_specs=[
            pl.BlockSpec((seq, LANES), lambda b, p: (b, p)),
            pl.BlockSpec((seq, LANES), lambda b, p: (b, k0 + p)),
            pl.BlockSpec((seq, 2 * RET_V_DIM), lambda b, p: (b, v0 + p)),
            pl.BlockSpec((seq, 2 * RET_V_DIM), lambda b, p: (b, p)),
            pl.BlockSpec((1, 2 * RET_V_DIM), lambda b, p: (0, p)),
            pl.BlockSpec((1, 2, RET_BLK, RET_BLK), lambda b, p: (p, 0, 0, 0)),
            pl.BlockSpec((1, RET_BLK, LANES), lambda b, p: (p, 0, 0)),
            pl.BlockSpec((1, RET_BLK, LANES), lambda b, p: (p, 0, 0)),
            pl.BlockSpec((1, 1, 2 * RET_V_DIM), lambda b, p: (p, 0, 0)),
        ],
        out_specs=pl.BlockSpec((seq, 2 * RET_V_DIM), lambda b, p: (b, p)),
        out_shape=jax.ShapeDtypeStruct((t, RET_V_WIDTH), BF16),
        scratch_shapes=[pltpu.VMEM((LANES, 2 * RET_V_DIM), F32)],
        compiler_params=pltpu.CompilerParams(
            dimension_semantics=("arbitrary", "arbitrary"), vmem_limit_bytes=VMEM_LIMIT),
        name="retention",
    )(rqk, rqk, plain, swish, ret_norm_g, dmat, qdec, kdec, cdec)


def _mix_kernel(ysb_ref, yret_ref, gs0_ref, gs1_ref, gr0_ref, gr1_ref, x_ref,
                wsb_ref, wret_ref, wout_ref, gffn_ref, wrh_ref, wrl_ref, br_ref, tri_ref,
                hext_ref, cls_ref, cnt_ref):
    tm = x_ref.shape[0]

    @pl.when(pl.program_id(0) == 0)
    def _():
        cnt_ref[...] = jnp.zeros_like(cnt_ref)

    a = jnp.dot(ysb_ref[...], wsb_ref[...], preferred_element_type=F32)
    b = jnp.dot(yret_ref[...], wret_ref[...], preferred_element_type=F32)
    g_sb = jnp.concatenate([gs0_ref[...], gs1_ref[...]], axis=1).astype(F32)
    g_ret = jnp.concatenate([gr0_ref[...], gr1_ref[...]], axis=1).astype(F32)
    mixed = (g_sb * a + g_ret * b).astype(BF16)
    h = x_ref[...] + jnp.dot(mixed, wout_ref[...], preferred_element_type=F32)
    hext_ref[:, 0:D_MODEL] = h

    ms = jnp.mean(h * h, axis=-1, keepdims=True)
    hn = (h * lax.rsqrt(ms + EPS)) * gffn_ref[...]
    hn_hi = hn.astype(BF16)
    hn_lo = (hn - hn_hi.astype(F32)).astype(BF16)
    nt = (((1,), (1,)), ((), ()))
    w_both = jnp.concatenate([wrh_ref[...], wrl_ref[...]], axis=0)
    by_hi = lax.dot_general(w_both, hn_hi, nt, preferred_element_type=F32)
    logits = (by_hi[0:ROUTER_ROWS] + by_hi[ROUTER_ROWS:2 * ROUTER_ROWS]
              + lax.dot_general(wrh_ref[...], hn_lo, nt, preferred_element_type=F32)
              + br_ref[...])

    def first_argmax(vals):
        m = functools.reduce(jnp.maximum, vals)
        idx = jnp.full(m.shape, len(vals) - 1, jnp.int32)
        for i in range(len(vals) - 2, -1, -1):
            idx = jnp.where(vals[i] >= m, i, idx)
        return m, idx

    gl = [logits[r:r + 1, :] for r in range(N_GROUPS)]
    gmax, gsel = first_argmax(gl)
    p_group = 1.0 / functools.reduce(lambda s, v: s + v, [jnp.exp(v - gmax) for v in gl])
    el = []
    for e in range(EXPERTS_PER_GROUP):
        v = logits[N_GROUPS + 3 * EXPERTS_PER_GROUP + e:N_GROUPS + 3 * EXPERTS_PER_GROUP + e + 1, :]
        for g in range(N_GROUPS - 2, -1, -1):
            r = N_GROUPS + g * EXPERTS_PER_GROUP + e
            v = jnp.where(gsel == g, logits[r:r + 1, :], v)
        el.append(v)
    m1, i1 = first_argmax(el)
    rest = [jnp.where(i1 == e, -jnp.inf, el[e]) for e in range(EXPERTS_PER_GROUP)]
    m2, i2 = first_argmax(rest)
    tt = jnp.exp(m2 - m1)
    gate1 = p_group * (1.0 / (1.0 + tt))
    gate2 = p_group * (tt / (1.0 + tt))
    lo_first = i1 < i2
    ea = jnp.minimum(i1, i2)
    eb = jnp.maximum(i1, i2)
    w_a = jnp.where(lo_first, gate1, gate2)
    w_b = jnp.where(lo_first, gate2, gate1)
    pid = jnp.where(ea == 0, eb - 1, jnp.where(ea == 1, eb + 1, PAIRS_PER_GROUP - 1))
    cls = gsel * PAIRS_PER_GROUP + pid
    info = jnp.concatenate([w_a, w_b, jnp.zeros((INFO_COLS - 2, tm), F32)], axis=0)
    hext_ref[:, D_MODEL:HEXT_COLS] = info.T

    class_row = lax.broadcasted_iota(jnp.int32, (ROUTER_ROWS, tm), 0)
    onehot = class_row == cls
    onehot_bf = onehot.astype(BF16)
    before = jnp.dot(onehot_bf, tri_ref[...], preferred_element_type=F32)
    cnt = cnt_ref[...]
    seen = before + jnp.concatenate([cnt] * (tm // LANES), axis=1)
    rank = jnp.sum(jnp.where(onehot, seen, 0.0), axis=0, keepdims=True)
    cnt_ref[...] = cnt + jnp.dot(onehot_bf, jnp.ones((tm, LANES), BF16), preferred_element_type=F32)
    cls_ref[...] = jnp.concatenate(
        [cls, rank.astype(jnp.int32), jnp.zeros((cls_ref.shape[0] - 2, tm), jnp.int32)], axis=0)


def _mix_call(ysb, yret, gates, x2, wsb, wret, wout, gffn, wr_hi, wr_lo, br):
    t = x2.shape[0]
    tm = MIX_TM
    gate0 = 0
    const = lambda i: (0, 0)
    idx = jnp.arange(tm, dtype=jnp.int32)
    tri = (idx[:, None] < idx[None, :]).astype(BF16)
    return pl.pallas_call(
        _mix_kernel,
        grid=(t // tm,),
        in_specs=[
            pl.BlockSpec((tm, SB_WIDTH), lambda i: (i, 0)),
            pl.BlockSpec((tm, RET_V_WIDTH), lambda i: (i, 0)),
            pl.BlockSpec((tm, GATE_BLK), lambda i: (i, gate0)),
            pl.BlockSpec((tm, GATE_BLK), lambda i: (i, gate0 + 1)),
            pl.BlockSpec((tm, GATE_BLK), lambda i: (i, gate0 + 2)),
            pl.BlockSpec((tm, GATE_BLK), lambda i: (i, gate0 + 3)),
            pl.BlockSpec((tm, D_MODEL), lambda i: (i, 0)),
            pl.BlockSpec((SB_WIDTH, D_MODEL), const),
            pl.BlockSpec((RET_V_WIDTH, D_MODEL), const),
            pl.BlockSpec((D_MODEL, D_MODEL), const),
            pl.BlockSpec((1, D_MODEL), const),
            pl.BlockSpec((ROUTER_ROWS, D_MODEL), const),
            pl.BlockSpec((ROUTER_ROWS, D_MODEL), const),
            pl.BlockSpec((ROUTER_ROWS, 1), const),
            pl.BlockSpec((tm, tm), const),
        ],
        out_specs=[
            pl.BlockSpec((tm, HEXT_COLS), lambda i: (i, 0)),
            pl.BlockSpec((8, tm), lambda i: (0, i)),
            pl.BlockSpec((ROUTER_ROWS, LANES), const),
        ],
        out_shape=[
            jax.ShapeDtypeStruct((t, HEXT_COLS), F32),
            jax.ShapeDtypeStruct((8, t), jnp.int32),
            jax.ShapeDtypeStruct((ROUTER_ROWS, LANES), F32),
        ],
        compiler_params=pltpu.CompilerParams(
            dimension_semantics=("arbitrary",), vmem_limit_bytes=VMEM_LIMIT),
        name="mix_router",
    )(ysb, yret, gates, gates, gates, gates, x2, wsb, wret, wout, gffn, wr_hi, wr_lo, br, tri)


def _moe_kernel(dest_ref, nvalid_ref, ea_ref, eb_ref,
                hext_ref, wga_ref, wua_ref, wda_ref, wgb_ref, wub_ref, wdb_ref,
                gffn_ref, gfin_ref, out_ref, hbuf, obuf, src, gsem, ssem):
    i = pl.program_id(0)
    n_tiles = pl.num_programs(0)
    tm = hbuf.shape[1]
    nv = nvalid_ref[i]
    nxt = jnp.minimum(i + 1, n_tiles - 1)
    nv_next = jnp.where(i + 1 < n_tiles, nvalid_ref[nxt], 0)
    has_next = nv_next > 0

    def gather_copy(tile, buf_slot, r):
        tok = src[tile * tm + r]
        return pltpu.make_async_copy(hext_ref.at[pl.ds(tok, 1), :],
                                     hbuf.at[buf_slot, pl.ds(r, 1), :], gsem.at[buf_slot])

    def scatter_copy(tile, buf_slot, r):
        tok = src[tile * tm + r]
        return pltpu.make_async_copy(obuf.at[buf_slot, pl.ds(r, 1), :],
                                     out_ref.at[pl.ds(tok, 1), :], ssem.at[buf_slot])

    def start_rows(copy_of_row, rows_valid):
        for r in range(tm):
            if rows_valid is None:
                copy_of_row(r).start(priority=r % 2)
            else:
                @pl.when(r < rows_valid)
                def _():
                    copy_of_row(r).start(priority=r % 2)

    def wait_rows(full_copy, row_copy, rows_valid):
        @pl.when(rows_valid == tm)
        def _():
            full_copy.wait()

        @pl.when(rows_valid < tm)
        def _():
            for r in range(tm):
                @pl.when(r < rows_valid)
                def _():
                    row_copy(r).wait()

    def wait_gather(buf_slot, rows_valid):
        wait_rows(pltpu.make_async_copy(hext_ref.at[pl.ds(0, tm), :], hbuf.at[buf_slot], gsem.at[buf_slot]),
                  lambda r: pltpu.make_async_copy(hext_ref.at[pl.ds(0, 1), :],
                                                  hbuf.at[buf_slot, pl.ds(r, 1), :], gsem.at[buf_slot]),
                  rows_valid)

    def wait_scatter(buf_slot, rows_valid):
        wait_rows(pltpu.make_async_copy(obuf.at[buf_slot], out_ref.at[pl.ds(0, tm), :], ssem.at[buf_slot]),
                  lambda r: pltpu.make_async_copy(obuf.at[buf_slot, pl.ds(r, 1), :],
                                                  out_ref.at[pl.ds(0, 1), :], ssem.at[buf_slot]),
                  rows_valid)

    @pl.when(i == 0)
    def _():
        def invert(c, carry):
            for u in range(MOE_INVERT_UNROLL):
                t = c * MOE_INVERT_UNROLL + u
                src[dest_ref[t]] = t
            return carry

        lax.fori_loop(0, dest_ref.shape[0] // MOE_INVERT_UNROLL, invert, 0)
        hbuf[...] = jnp.zeros_like(hbuf)
        start_rows(functools.partial(gather_copy, 0, 0), nv)

    def tile_body(slot):
        @pl.when(i >= 2)
        def _():
            wait_scatter(slot, nvalid_ref[jnp.maximum(i - 2, 0)])

        wait_gather(slot, nv)
        both_full = (nv == tm) & (nv_next == tm)
        pl.when(both_full)(functools.partial(tile_main, slot, None, None))
        pl.when(jnp.logical_not(both_full))(functools.partial(tile_main, slot, nv, nv_next))

        @pl.when(jnp.logical_not(has_next))
        def _():
            @pl.when(i >= 1)
            def _():
                wait_scatter(1 - slot, nvalid_ref[jnp.maximum(i - 1, 0)])

            wait_scatter(slot, nv)

    def tile_main(slot, rows_now, rows_next):
        start_rows(functools.partial(gather_copy, nxt, 1 - slot), rows_next)
        hrows = hbuf[slot]
        h = hrows[:, 0:D_MODEL]
        w_a = hrows[:, D_MODEL:D_MODEL + 1]
        w_b = hrows[:, D_MODEL + 1:D_MODEL + 2]
        ms = jnp.mean(h * h, axis=-1, keepdims=True)
        hn = ((h * lax.rsqrt(ms + EPS)) * gffn_ref[...]).astype(BF16)

        def expert(wg_ref, wu_ref, wd_ref):
            gate = jnp.dot(hn, wg_ref[0], preferred_element_type=F32)
            up = jnp.dot(hn, wu_ref[0], preferred_element_type=F32)
            hidden = ((gate * jax.nn.sigmoid(gate)) * up).astype(BF16)
            return jnp.dot(hidden, wd_ref[0], preferred_element_type=F32)

        y = w_a * expert(wga_ref, wua_ref, wda_ref) + w_b * expert(wgb_ref, wub_ref, wdb_ref)
        h2 = h + y
        ms2 = jnp.mean(h2 * h2, axis=-1, keepdims=True)
        obuf[slot] = (h2 * lax.rsqrt(ms2 + EPS)) * gfin_ref[...]
        start_rows(functools.partial(scatter_copy, i, slot), rows_now)

    for parity in range(2):
        pl.when((nv > 0) & (i % 2 == parity))(functools.partial(tile_body, parity))


def _moe_call(dest, nvalid, ea, eb, hext, wg, wu, wd, gffn, gfin):
    t = hext.shape[0]
    tm = MOE_TM
    n_tiles = nvalid.shape[0]
    assert t % MOE_INVERT_UNROLL == 0
    wa_map = lambda i, dest, nv, ea, eb: (ea[i], 0, 0)
    wb_map = lambda i, dest, nv, ea, eb: (eb[i], 0, 0)
    const = lambda i, dest, nv, ea, eb: (0, 0)
    return pl.pallas_call(
        _moe_kernel,
        grid_spec=pltpu.PrefetchScalarGridSpec(
            num_scalar_prefetch=4,
            grid=(n_tiles,),
            in_specs=[
                pl.BlockSpec(memory_space=pl.ANY),
                pl.BlockSpec((1, D_MODEL, D_FF), wa_map),
                pl.BlockSpec((1, D_MODEL, D_FF), wa_map),
                pl.BlockSpec((1, D_FF, D_MODEL), wa_map),
                pl.BlockSpec((1, D_MODEL, D_FF), wb_map),
                pl.BlockSpec((1, D_MODEL, D_FF), wb_map),
                pl.BlockSpec((1, D_FF, D_MODEL), wb_map),
                pl.BlockSpec((1, D_MODEL), const),
                pl.BlockSpec((1, D_MODEL), const),
            ],
            out_specs=pl.BlockSpec(memory_space=pl.ANY),
            scratch_shapes=[
                pltpu.VMEM((2, tm, HEXT_COLS), F32),
                pltpu.VMEM((2, tm, D_MODEL), F32),
                pltpu.SMEM((n_tiles * tm,), jnp.int32),
                pltpu.SemaphoreType.DMA((2,)),
                pltpu.SemaphoreType.DMA((2,)),
            ],
        ),
        out_shape=jax.ShapeDtypeStruct((t, D_MODEL), F32),
        compiler_params=pltpu.CompilerParams(
            dimension_semantics=("arbitrary",), vmem_limit_bytes=VMEM_LIMIT),
        name="experts",
    )(dest, nvalid, ea, eb, hext, wg, wu, wd, wg, wu, wd, gffn, gfin)


def _routing_plan(cls, rank, counts, tm):
    t = cls.shape[0]
    n_tiles = t // tm + N_CLASSES
    tiles_c = (counts + tm - 1) // tm
    tile_end = jnp.cumsum(tiles_c)
    tile_off = tile_end - tiles_c
    onehot = cls[:, None] == jnp.arange(N_CLASSES, dtype=jnp.int32)[None, :]
    dest = jnp.sum(jnp.where(onehot, tile_off[None, :], 0), axis=1) * tm + rank
    tile = jnp.arange(n_tiles, dtype=jnp.int32)
    used = tile < tile_end[-1]
    tile_cls = jnp.minimum(jnp.sum((tile[:, None] >= tile_end[None, :]).astype(jnp.int32), axis=1),
                           N_CLASSES - 1)
    cls_onehot = tile_cls[:, None] == jnp.arange(N_CLASSES, dtype=jnp.int32)[None, :]
    pick = lambda table: jnp.sum(jnp.where(cls_onehot, table[None, :], 0), axis=1)
    nvalid = jnp.where(used, jnp.clip(pick(counts) - (tile - pick(tile_off)) * tm, 0, tm), 0)
    classes = np.arange(N_CLASSES)
    ea_tab = jnp.asarray((classes // PAIRS_PER_GROUP) * EXPERTS_PER_GROUP
                         + np.asarray(_PAIR_A)[classes % PAIRS_PER_GROUP], jnp.int32)
    eb_tab = jnp.asarray((classes // PAIRS_PER_GROUP) * EXPERTS_PER_GROUP
                         + np.asarray(_PAIR_B)[classes % PAIRS_PER_GROUP], jnp.int32)
    last_used = jnp.sum(jnp.where(tile == tile_end[-1] - 1, tile_cls, 0))
    tile_cls = jnp.where(used, tile_cls, last_used)
    cls_onehot = tile_cls[:, None] == jnp.arange(N_CLASSES, dtype=jnp.int32)[None, :]
    return dest.astype(jnp.int32), nvalid.astype(jnp.int32), pick(ea_tab), pick(eb_tab)


def _rope_tables(seq):
    half = RET_QK_DIM // 2
    inv_freq = ROPE_BASE ** (-jnp.arange(half, dtype=F32) / half)
    ang = jnp.arange(seq, dtype=F32)[:, None] * inv_freq[None, :]
    cos, sin = jnp.cos(ang), jnp.sin(ang)
    cos_t = jnp.tile(cos, (1, LANES // half))
    sin_t = jnp.concatenate([-sin, -sin, sin, sin], axis=1)
    return cos_t, sin_t


def kernel(x, norm_mix_g, w_in, w_gate, b_gate, w_sb_out, w_ret_out, ret_norm_g, w_out,
           norm_ffn_g, w_group_router, b_group_router, w_expert_router, b_expert_router,
           w_exp_gate, w_exp_up, w_exp_down, norm_final_g):
    bsz, seq, d = x.shape
    t = bsz * seq
    assert d == D_MODEL and w_in.shape[0] == 1 and t % MIX_TM == 0
    assert seq % (SB_GROUP * SB_BLK) == 0 and seq % (RET_BLK * RET_UNROLL) == 0
    x2 = x.reshape(t, d)
    wi = w_in[0]
    c_rq = 3 * SB_WIDTH
    c_rk = c_rq + RET_QK_WIDTH
    c_rv = c_rk + RET_QK_WIDTH
    c_rg = c_rv + RET_V_WIDTH
    ones = functools.partial(jnp.ones, dtype=F32)
    w_plain = jnp.concatenate([wi[:, :c_rq], wi[:, c_rv:c_rg]], axis=1).astype(BF16)
    s_plain = jnp.concatenate([jnp.full((SB_WIDTH,), SB_HEAD_DIM ** -0.5 * LOG2E, F32),
                               ones((2 * SB_WIDTH + RET_V_WIDTH,))])[None, :]
    perm = _rot_perm()
    w_rot = jnp.concatenate([wi[:, c_rq:c_rk][:, perm],
                             (RET_QK_DIM ** -0.5) * wi[:, c_rk:c_rv][:, perm]], axis=1).astype(BF16)
    w_swish_half = (0.5 * wi[:, c_rg:]).astype(BF16)
    w_gate_half = (0.5 * w_gate[0]).astype(BF16)
    b_gate_half = (0.5 * b_gate[0])[None, :]
    cos_t, sin_t = _rope_tables(seq)

    plain, xn = _proj_plain_call(x2, norm_mix_g[0][None, :], w_plain, s_plain, seq)
    rqk = _proj_rot_call(xn, w_rot, cos_t, sin_t, seq)
    swish = _proj_swish_call(xn, w_swish_half, seq)
    gates = _proj_gate_call(xn, w_gate_half, b_gate_half, seq)
    ysb = _sb_call(plain, seq)
    yret = _ret_call(rqk, plain, swish, ret_norm_g[0][None, :], seq)

    wr = jnp.concatenate([w_group_router[0], w_expert_router[0]], axis=1).T
    wr = jnp.pad(wr, ((0, ROUTER_ROWS - wr.shape[0]), (0, 0)))
    wr_hi = wr.astype(BF16)
    wr_lo = (wr - wr_hi.astype(F32)).astype(BF16)
    br = jnp.concatenate([b_group_router[0], b_expert_router[0]])
    br = jnp.pad(br, (0, ROUTER_ROWS - br.shape[0]))[:, None]
    hext, cls8, cnt = _mix_call(ysb, yret, gates, x2, w_sb_out[0].astype(BF16), w_ret_out[0].astype(BF16),
                           w_out[0].astype(BF16), norm_ffn_g[0][None, :], wr_hi, wr_lo, br)

    counts = cnt[:N_CLASSES, 0].astype(jnp.int32)
    dest, nvalid, ea, eb = _routing_plan(cls8[0], cls8[1], counts, MOE_TM)
    out = _moe_call(dest, nvalid, ea, eb, hext, w_exp_gate[0].astype(BF16), w_exp_up[0].astype(BF16),
                    w_exp_down[0].astype(BF16), norm_ffn_g[0][None, :], norm_final_g[None, :])
    return out.reshape(bsz, seq, d)
```

```python
import functools
import math

import numpy as np
import jax
import jax.numpy as jnp
from jax import lax
from jax.experimental import pallas as pl
from jax.experimental.pallas import tpu as pltpu

F32 = jnp.float32
BF16 = jnp.bfloat16

D_MODEL = 1024
SB_HEADS = 8
SB_HEAD_DIM = 64
SB_WIDTH = SB_HEADS * SB_HEAD_DIM
RET_HEADS = 8
RET_QK_DIM = 64
RET_V_DIM = 128
RET_QK_WIDTH = RET_HEADS * RET_QK_DIM
RET_V_WIDTH = RET_HEADS * RET_V_DIM
IN_COLS = 3 * SB_WIDTH + 2 * RET_QK_WIDTH + 2 * RET_V_WIDTH
GATE_COLS = 2 * D_MODEL
ALL_COLS = IN_COLS + GATE_COLS
CHUNK = 64
ROPE_BASE = 10000.0
N_GROUPS = 4
EXPERTS_PER_GROUP = 4
N_EXPERTS = N_GROUPS * EXPERTS_PER_GROUP
D_FF = 512
EPS = 1e-6

LANES = 128
PROJ_TN = 1280
GATE_BLK = 512
PROJ_ROWS = 256
SB_BLK = 256
SB_GROUP = 4
RET_BLK = 256
RET_UNROLL = 4
MIX_TM = 512
MOE_TM = 256
MOE_INVERT_UNROLL = 32
ROUTER_ROWS = 32
PAIRS_PER_GROUP = 6
N_CLASSES = N_GROUPS * PAIRS_PER_GROUP
INFO_COLS = LANES
HEXT_COLS = D_MODEL + INFO_COLS
VMEM_LIMIT = 56 * 1024 * 1024
LOG2E = math.log2(math.e)
SB_SKIP_LOG2 = 156.0

_PAIR_A = (0, 0, 0, 1, 1, 2)
_PAIR_B = (1, 2, 3, 2, 3, 3)


def _proj_plain_kernel(x_ref, g_ref, w_ref, s_ref, o_ref, xn_ref):
    def body(normalise):
        for rows in _row_chunks(xn_ref.shape[0]):
            if normalise:
                x = x_ref[rows, :]
                ms = jnp.mean(x * x, axis=-1, keepdims=True)
                xn_ref[rows, :] = ((x * lax.rsqrt(ms + EPS)) * g_ref[...]).astype(BF16)
            acc = jnp.dot(xn_ref[rows, :], w_ref[...], preferred_element_type=F32)
            o_ref[rows, :] = (acc * s_ref[...]).astype(BF16)

    first = pl.program_id(1) == 0
    pl.when(first)(functools.partial(body, True))
    pl.when(jnp.logical_not(first))(functools.partial(body, False))


def _row_chunks(n_rows):
    return [slice(r, r + PROJ_ROWS) for r in range(0, n_rows, PROJ_ROWS)]


def _proj_rot_kernel(xn_ref, w_ref, cos_ref, sin_ref, o_ref):
    for rows in _row_chunks(xn_ref.shape[0]):
        acc = jnp.dot(xn_ref[rows, :], w_ref[...], preferred_element_type=F32)
        cos = cos_ref[rows, :]
        sin = sin_ref[rows, :]
        for p in range(w_ref.shape[1] // LANES):
            cols = slice(p * LANES, (p + 1) * LANES)
            seg = acc[:, cols]
            o_ref[rows, cols] = (seg * cos + pltpu.roll(seg, LANES // 2, 1) * sin).astype(BF16)


def _proj_swish_kernel(xn_ref, w_ref, o_ref):
    for rows in _row_chunks(xn_ref.shape[0]):
        half = jnp.dot(xn_ref[rows, :], w_ref[...], preferred_element_type=F32)
        o_ref[rows, :] = (half * (jnp.tanh(half) + 1.0)).astype(BF16)


def _proj_gate_kernel(xn_ref, w_ref, b_ref, o_ref):
    for rows in _row_chunks(xn_ref.shape[0]):
        half = jnp.dot(xn_ref[rows, :], w_ref[...], preferred_element_type=F32) + b_ref[...]
        o_ref[rows, :] = (0.5 * jnp.tanh(half) + 0.5).astype(BF16)


def _proj_params():
    return pltpu.CompilerParams(dimension_semantics=("arbitrary", "arbitrary"),
                                vmem_limit_bytes=VMEM_LIMIT)


def _proj_tile(n):
    tn = PROJ_TN
    while n % tn:
        tn -= LANES
    return tn


def _proj_plain_call(x2, g, w, scale, seq):
    t, n = x2.shape[0], w.shape[1]
    tn = _proj_tile(n)
    return pl.pallas_call(
        _proj_plain_kernel,
        grid=(t // seq, n // tn),
        in_specs=[
            pl.BlockSpec((seq, D_MODEL), lambda i, j: (i, 0)),
            pl.BlockSpec((1, D_MODEL), lambda i, j: (0, 0)),
            pl.BlockSpec((D_MODEL, tn), lambda i, j: (0, j)),
            pl.BlockSpec((1, tn), lambda i, j: (0, j)),
        ],
        out_specs=[
            pl.BlockSpec((seq, tn), lambda i, j: (i, j)),
            pl.BlockSpec((seq, D_MODEL), lambda i, j: (i, 0)),
        ],
        out_shape=[jax.ShapeDtypeStruct((t, n), BF16), jax.ShapeDtypeStruct((t, D_MODEL), BF16)],
        compiler_params=_proj_params(),
        name="proj_plain",
    )(x2, g, w, scale)


def _proj_rot_call(xn, w, cos_t, sin_t, seq):
    t, n = xn.shape[0], w.shape[1]
    tn = _proj_tile(n)
    return pl.pallas_call(
        _proj_rot_kernel,
        grid=(t // seq, n // tn),
        in_specs=[
            pl.BlockSpec((seq, D_MODEL), lambda i, j: (i, 0)),
            pl.BlockSpec((D_MODEL, tn), lambda i, j: (0, j)),
            pl.BlockSpec((seq, LANES), lambda i, j: (0, 0)),
            pl.BlockSpec((seq, LANES), lambda i, j: (0, 0)),
        ],
        out_specs=pl.BlockSpec((seq, tn), lambda i, j: (i, j)),
        out_shape=jax.ShapeDtypeStruct((t, n), BF16),
        compiler_params=_proj_params(),
        name="proj_rotary",
    )(xn, w, cos_t, sin_t)


def _proj_swish_call(xn, w_half, seq):
    t, n = xn.shape[0], w_half.shape[1]
    tn = _proj_tile(n)
    return pl.pallas_call(
        _proj_swish_kernel,
        grid=(t // seq, n // tn),
        in_specs=[
            pl.BlockSpec((seq, D_MODEL), lambda i, j: (i, 0)),
            pl.BlockSpec((D_MODEL, tn), lambda i, j: (0, j)),
        ],
        out_specs=pl.BlockSpec((seq, tn), lambda i, j: (i, j)),
        out_shape=jax.ShapeDtypeStruct((t, n), BF16),
        compiler_params=_proj_params(),
        name="proj_swish",
    )(xn, w_half)


def _proj_gate_call(xn, w_half, bias_half, seq):
    t, n = xn.shape[0], w_half.shape[1]
    tn = _proj_tile(n)
    return pl.pallas_call(
        _proj_gate_kernel,
        grid=(t // seq, n // tn),
        in_specs=[
            pl.BlockSpec((seq, D_MODEL), lambda i, j: (i, 0)),
            pl.BlockSpec((D_MODEL, tn), lambda i, j: (0, j)),
            pl.BlockSpec((1, tn), lambda i, j: (0, j)),
        ],
        out_specs=pl.BlockSpec((seq, tn), lambda i, j: (i, j)),
        out_shape=jax.ShapeDtypeStruct((t, n), BF16),
        compiler_params=_proj_params(),
        name="proj_gate",
    )(xn, w_half, bias_half)


def _rot_perm():
    half = RET_QK_DIM // 2
    order = [p * LANES + hh * RET_QK_DIM + part * half + d
             for p in range(RET_QK_WIDTH // LANES) for part in range(2) for hh in range(2)
             for d in range(half)]
    return np.asarray(order, np.int32)


def _sb_kernel(q_ref, k_ref, v_ref, o_ref, u_ref, acc_ref, car_ref):
    seq = q_ref.shape[0]
    nq = seq // SB_BLK
    row = lax.broadcasted_iota(jnp.int32, (SB_BLK, SB_BLK), 0)
    col = lax.broadcasted_iota(jnp.int32, (SB_BLK, SB_BLK), 1)
    u_ref[...] = (row >= col).astype(BF16)
    lane = lax.broadcasted_iota(jnp.int32, (1, LANES), 1)
    head_masks = (lane < SB_HEAD_DIM, lane >= SB_HEAD_DIM)

    heads = range(2)
    causal = col < row

    def rows_of(blk):
        return pl.ds(pl.multiple_of(blk * SB_BLK, SB_BLK), SB_BLK)

    def q_heads_of(qi):
        q_blk = q_ref[rows_of(qi), :]
        return tuple(jnp.where(m, q_blk, jnp.zeros_like(q_blk)) for m in head_masks)

    def sweep(jobs):
        u = u_ref[...]
        q_heads = [q_heads_of(qi) for _, qi, _, _ in jobs]
        z2 = [[[lax.dot_general(q_heads[j][h], k_ref[rows_of(kb), :], (((1,), (1,)), ((), ())),
                                preferred_element_type=F32) for h in heads]
               for kb, _ in steps] for j, (_, _, steps, _) in enumerate(jobs)]
        sinc = []
        for j, (_, _, steps, _) in enumerate(jobs):
            sinc.append([])
            for i, (_, diagonal) in enumerate(steps):
                nlk = [jnp.maximum(z, 0.0) + jnp.log2(1.0 + jnp.exp2(-jnp.abs(z))) for z in z2[j][i]]
                if diagonal:
                    nlk = [jnp.where(causal, a, 0.0) for a in nlk]
                sinc[j].append([jnp.dot(a.astype(BF16), u, preferred_element_type=F32) for a in nlk])
        for j, (slot, _, steps, fresh) in enumerate(jobs):
            car = [None, None] if fresh else [car_ref[slot, h] for h in heads]
            contrib = None
            for i, (kb, diagonal) in enumerate(steps):
                v_blk = v_ref[rows_of(kb), :]
                for h in heads:
                    e = z2[j][i][h] - sinc[j][i][h]
                    if car[h] is not None:
                        e = e - jnp.concatenate([car[h], car[h]], axis=1)
                    w = jnp.exp2(e)
                    if diagonal:
                        w = jnp.where(causal, w, 0.0)
                    vh = jnp.where(head_masks[h], v_blk, jnp.zeros_like(v_blk))
                    pv = jnp.dot(w.astype(BF16), vh, preferred_element_type=F32)
                    contrib = pv if contrib is None else contrib + pv
                    tot = jnp.broadcast_to(sinc[j][i][h][:, 0:1], (SB_BLK, LANES))
                    car[h] = tot if car[h] is None else car[h] + tot
            for h in heads:
                car_ref[slot, h] = car[h]
            if fresh:
                acc_ref[slot] = contrib
            else:
                acc_ref[slot] += contrib

    def finish(jobs):
        pending = [job for job in jobs if job[2] is not None]
        if pending:
            lo, hi = pending[0][0], pending[-1][0] + 1

            @pl.when(jnp.min(car_ref[lo:hi]) < SB_SKIP_LOG2)
            def _():
                for slot, qi, first_kb in pending:
                    def more(c):
                        kb, min_carry = c
                        return (kb >= 0) & (min_carry < SB_SKIP_LOG2)

                    def k_step(c, slot=slot, qi=qi):
                        kb, _ = c
                        sweep([(slot, qi, [(kb, False)], False)])
                        return kb - 1, jnp.min(car_ref[slot])

                    lax.while_loop(more, k_step, (first_kb, jnp.min(car_ref[slot])))
        for slot, qi, _ in jobs:
            o_ref[rows_of(qi), :] = acc_ref[slot].astype(BF16)

    def first_steps(qi):
        return [(qi, True)] + ([(qi - 1, False)] if qi > 0 else [])

    sweep([(s, s, first_steps(s), True) for s in range(SB_GROUP)])
    finish([(s, s, s - 2 if s >= 2 else None) for s in range(SB_GROUP)])

    def q_group(g, carry):
        q0 = g * SB_GROUP
        sweep([(s, q0 + s, [(q0 + s, True), (q0 + s - 1, False)], True) for s in range(SB_GROUP)])
        finish([(s, q0 + s, q0 + s - 2) for s in range(SB_GROUP)])
        return carry

    lax.fori_loop(1, nq // SB_GROUP, q_group, 0)


def _sb_call(proj, seq):
    t = proj.shape[0]
    n_pairs = SB_WIDTH // LANES
    return pl.pallas_call(
        _sb_kernel,
        grid=(t // seq, n_pairs),
        in_specs=[
            pl.BlockSpec((seq, LANES), lambda b, p: (b, p)),
            pl.BlockSpec((seq, LANES), lambda b, p: (b, n_pairs + p)),
            pl.BlockSpec((seq, LANES), lambda b, p: (b, 2 * n_pairs + p)),
        ],
        out_specs=pl.BlockSpec((seq, LANES), lambda b, p: (b, p)),
        out_shape=jax.ShapeDtypeStruct((t, SB_WIDTH), BF16),
        scratch_shapes=[
            pltpu.VMEM((SB_BLK, SB_BLK), BF16),
            pltpu.VMEM((SB_GROUP, SB_BLK, LANES), F32),
            pltpu.VMEM((SB_GROUP, 2, SB_BLK, LANES), F32),
        ],
        compiler_params=pltpu.CompilerParams(
            dimension_semantics=("arbitrary", "arbitrary"), vmem_limit_bytes=VMEM_LIMIT),
        name="stickbreak",
    )(proj, proj, proj)


def _ret_tables():
    h = np.arange(RET_HEADS, dtype=np.float64)
    log_gamma = np.log(1.0 - 2.0 ** (-5.0 - h))
    idx = np.arange(RET_BLK, dtype=np.float64)
    t, s = idx[:, None], idx[None, :]
    same = (t // CHUNK) == (s // CHUNK)
    earlier = (s // CHUNK) < (t // CHUNK)
    expo = np.where(same, np.abs(t - s), np.where(earlier, t - s, 0.0))
    dmat = np.exp(log_gamma[:, None, None] * expo) * (same | earlier)
    qdec = np.exp(log_gamma[:, None] * (idx + 1.0)[None, :])
    kdec = np.exp(log_gamma[:, None] * (RET_BLK - 1.0 - idx)[None, :])
    cdec = np.exp(log_gamma * RET_BLK)
    n_pairs = RET_HEADS // 2
    lane_head = (np.arange(LANES) % RET_QK_DIM) // (RET_QK_DIM // 2)
    col_head = np.arange(2 * RET_V_DIM) // RET_V_DIM
    pair_heads = np.arange(RET_HEADS).reshape(n_pairs, 2)
    qdec_pair = np.stack([qdec[pair_heads[p][lane_head]].T for p in range(n_pairs)])
    kdec_pair = np.stack([kdec[pair_heads[p][lane_head]].T for p in range(n_pairs)])
    cdec_pair = np.stack([cdec[pair_heads[p][col_head]][None, :] for p in range(n_pairs)])
    return (jnp.asarray(dmat.reshape((n_pairs, 2, RET_BLK, RET_BLK)), F32),
            jnp.asarray(qdec_pair, F32), jnp.asarray(kdec_pair, F32), jnp.asarray(cdec_pair, F32))


def _ret_kernel(q_ref, k_ref, v_ref, g_ref, gn_ref, dm_ref, qd_ref, kd_ref, cd_ref, o_ref, st_ref):
    seq = q_ref.shape[0]
    lane = lax.broadcasted_iota(jnp.int32, (1, LANES), 1)
    head0 = (lane % RET_QK_DIM) < (RET_QK_DIM // 2)
    head_masks = (head0, jnp.logical_not(head0))
    st_row = lax.broadcasted_iota(jnp.int32, st_ref.shape, 0)
    st_col = lax.broadcasted_iota(jnp.int32, st_ref.shape, 1)
    own_head = ((st_row % RET_QK_DIM) // (RET_QK_DIM // 2)) == (st_col // RET_V_DIM)
    heads = range(2)
    nt = (((1,), (1,)), ((), ()))
    tn = (((0,), (0,)), ((), ()))
    st_ref[...] = jnp.zeros_like(st_ref)

    def step(n, carry):
        subs = range(RET_UNROLL)
        rows = [pl.ds(pl.multiple_of((n * RET_UNROLL + u) * RET_BLK, RET_BLK), RET_BLK) for u in subs]
        col = [slice(h * RET_V_DIM, (h + 1) * RET_V_DIM) for h in heads]
        q_blk = [q_ref[r, :] for r in rows]
        k_blk = [k_ref[r, :] for r in rows]
        qm = [[jnp.where(head_masks[h], q_blk[u], jnp.zeros_like(q_blk[u])) for h in heads] for u in subs]
        v = [v_ref[r, :] for r in rows]
        scores = [[lax.dot_general(qm[u][h], k_blk[u], nt, preferred_element_type=F32)
                   for h in heads] for u in subs]
        kd = [(k_blk[u].astype(F32) * kd_ref[0]).astype(BF16) for u in subs]
        kv = [jnp.where(own_head, lax.dot_general(kd[u], v[u], tn, preferred_element_type=F32), 0.0)
              for u in subs]
        qd = [(q_blk[u].astype(F32) * qd_ref[0]).astype(BF16) for u in subs]
        p = [[(scores[u][h] * dm_ref[0, h]).astype(BF16) for h in heads] for u in subs]
        intra = [[jnp.dot(p[u][h], v[u][:, col[h]], preferred_element_type=F32) for h in heads]
                 for u in subs]
        state = st_ref[...]
        for u in subs:
            cross = jnp.dot(qd[u], state.astype(BF16), preferred_element_type=F32)
            state = state * cd_ref[0] + kv[u]
            for h in heads:
                y = intra[u][h] + cross[:, col[h]]
                ms = jnp.mean(y * y, axis=-1, keepdims=True)
                yn = (y * lax.rsqrt(ms + EPS)) * gn_ref[:, col[h]]
                o_ref[rows[u], col[h]] = (g_ref[rows[u], col[h]].astype(F32) * yn).astype(BF16)
        st_ref[...] = state
        return carry

    lax.fori_loop(0, seq // (RET_BLK * RET_UNROLL), step, 0)


def _ret_call(rqk, plain, swish, ret_norm_g, seq):
    t = rqk.shape[0]
    n_pairs = RET_HEADS // 2
    dmat, qdec, kdec, cdec = _ret_tables()
    k0 = RET_QK_WIDTH // LANES
    v0 = (3 * SB_WIDTH) // (2 * RET_V_DIM)
    return pl.pallas_call(
        _ret_kernel,
        grid=(t // seq, n_pairs),
        in_specs=[
            pl.BlockSpec((seq, LANES), lambda b, p: (b, p)),
            pl.BlockSpec((seq, LANES), lambda b, p: (b, k0 + p)),
            pl.BlockSpec((seq, 2 * RET_V_DIM), lambda b, p: (b, v0 + p)),
            pl.BlockSpec((seq, 2 * RET_V_DIM), lambda b, p: (b, p)),
            pl.BlockSpec((1, 2 * RET_V_DIM), lambda b, p: (0, p)),
            pl.BlockSpec((1, 2, RET_BLK, RET_BLK), lambda b, p: (p, 0, 0, 0)),
            pl.BlockSpec((1, RET_BLK, LANES), lambda b, p: (p, 0, 0)),
            pl.BlockSpec((1, RET_BLK, LANES), lambda b, p: (p, 0, 0)),
            pl.BlockSpec((1, 1, 2 * RET_V_DIM), lambda b, p: (p, 0, 0)),
        ],
        out_specs=pl.BlockSpec((seq, 2 * RET_V_DIM), lambda b, p: (b, p)),
        out_shape=jax.ShapeDtypeStruct((t, RET_V_WIDTH), BF16),
        scratch_shapes=[pltpu.VMEM((LANES, 2 * RET_V_DIM), F32)],
        compiler_params=pltpu.CompilerParams(
            dimension_semantics=("arbitrary", "arbitrary"), vmem_limit_bytes=VMEM_LIMIT),
        name="retention",
    )(rqk, rqk, plain, swish, ret_norm_g, dmat, qdec, kdec, cdec)


def _mix_kernel(ysb_ref, yret_ref, gs0_ref, gs1_ref, gr0_ref, gr1_ref, x_ref,
                wsb_ref, wret_ref, wout_ref, gffn_ref, wrh_ref, wrl_ref, br_ref, tri_ref,
                hext_ref, cls_ref, cnt_ref):
    tm = x_ref.shape[0]

    @pl.when(pl.program_id(0) == 0)
    def _():
        cnt_ref[...] = jnp.zeros_like(cnt_ref)

    a = jnp.dot(ysb_ref[...], wsb_ref[...], preferred_element_type=F32)
    b = jnp.dot(yret_ref[...], wret_ref[...], preferred_element_type=F32)
    g_sb = jnp.concatenate([gs0_ref[...], gs1_ref[...]], axis=1).astype(F32)
    g_ret = jnp.concatenate([gr0_ref[...], gr1_ref[...]], axis=1).astype(F32)
    mixed = (g_sb * a + g_ret * b).astype(BF16)
    h = x_ref[...] + jnp.dot(mixed, wout_ref[...], preferred_element_type=F32)
    hext_ref[:, 0:D_MODEL] = h

    ms = jnp.mean(h * h, axis=-1, keepdims=True)
    hn = (h * lax.rsqrt(ms + EPS)) * gffn_ref[...]
    hn_hi = hn.astype(BF16)
    hn_lo = (hn - hn_hi.astype(F32)).astype(BF16)
    nt = (((1,), (1,)), ((), ()))
    w_both = jnp.concatenate([wrh_ref[...], wrl_ref[...]], axis=0)
    by_hi = lax.dot_general(w_both, hn_hi, nt, preferred_element_type=F32)
    logits = (by_hi[0:ROUTER_ROWS] + by_hi[ROUTER_ROWS:2 * ROUTER_ROWS]
              + lax.dot_general(wrh_ref[...], hn_lo, nt, preferred_element_type=F32)
              + br_ref[...])

    def first_argmax(vals):
        m = functools.reduce(jnp.maximum, vals)
        idx = jnp.full(m.shape, len(vals) - 1, jnp.int32)
        for i in range(len(vals) - 2, -1, -1):
            idx = jnp.where(vals[i] >= m, i, idx)
        return m, idx

    gl = [logits[r:r + 1, :] for r in range(N_GROUPS)]
    gmax, gsel = first_argmax(gl)
    p_group = 1.0 / functools.reduce(lambda s, v: s + v, [jnp.exp(v - gmax) for v in gl])
    el = []
    for e in range(EXPERTS_PER_GROUP):
        v = logits[N_GROUPS + 3 * EXPERTS_PER_GROUP + e:N_GROUPS + 3 * EXPERTS_PER_GROUP + e + 1, :]
        for g in range(N_GROUPS - 2, -1, -1):
            r = N_GROUPS + g * EXPERTS_PER_GROUP + e
            v = jnp.where(gsel == g, logits[r:r + 1, :], v)
        el.append(v)
    m1, i1 = first_argmax(el)
    rest = [jnp.where(i1 == e, -jnp.inf, el[e]) for e in range(EXPERTS_PER_GROUP)]
    m2, i2 = first_argmax(rest)
    tt = jnp.exp(m2 - m1)
    gate1 = p_group * (1.0 / (1.0 + tt))
    gate2 = p_group * (tt / (1.0 + tt))
    lo_first = i1 < i2
    ea = jnp.minimum(i1, i2)
    eb = jnp.maximum(i1, i2)
    w_a = jnp.where(lo_first, gate1, gate2)
    w_b = jnp.where(lo_first, gate2, gate1)
    pid = jnp.where(ea == 0, eb - 1, jnp.where(ea == 1, eb + 1, PAIRS_PER_GROUP - 1))
    cls = gsel * PAIRS_PER_GROUP + pid
    info = jnp.concatenate([w_a, w_b, jnp.zeros((INFO_COLS - 2, tm), F32)], axis=0)
    hext_ref[:, D_MODEL:HEXT_COLS] = info.T

    class_row = lax.broadcasted_iota(jnp.int32, (ROUTER_ROWS, tm), 0)
    onehot = class_row == cls
    onehot_bf = onehot.astype(BF16)
    before = jnp.dot(onehot_bf, tri_ref[...], preferred_element_type=F32)
    cnt = cnt_ref[...]
    seen = before + jnp.concatenate([cnt] * (tm // LANES), axis=1)
    rank = jnp.sum(jnp.where(onehot, seen, 0.0), axis=0, keepdims=True)
    cnt_ref[...] = cnt + jnp.dot(onehot_bf, jnp.ones((tm, LANES), BF16), preferred_element_type=F32)
    cls_ref[...] = jnp.concatenate(
        [cls, rank.astype(jnp.int32), jnp.zeros((cls_ref.shape[0] - 2, tm), jnp.int32)], axis=0)


def _mix_call(ysb, yret, gates, x2, wsb, wret, wout, gffn, wr_hi, wr_lo, br):
    t = x2.shape[0]
    tm = MIX_TM
    gate0 = 0
    const = lambda i: (0, 0)
    idx = jnp.arange(tm, dtype=jnp.int32)
    tri = (idx[:, None] < idx[None, :]).astype(BF16)
    return pl.pallas_call(
        _mix_kernel,
        grid=(t // tm,),
        in_specs=[
            pl.BlockSpec((tm, SB_WIDTH), lambda i: (i, 0)),
            pl.BlockSpec((tm, RET_V_WIDTH), lambda i: (i, 0)),
            pl.BlockSpec((tm, GATE_BLK), lambda i: (i, gate0)),
            pl.BlockSpec((tm, GATE_BLK), lambda i: (i, gate0 + 1)),
            pl.BlockSpec((tm, GATE_BLK), lambda i: (i, gate0 + 2)),
            pl.BlockSpec((tm, GATE_BLK), lambda i: (i, gate0 + 3)),
            pl.BlockSpec((tm, D_MODEL), lambda i: (i, 0)),
            pl.BlockSpec((SB_WIDTH, D_MODEL), const),
            pl.BlockSpec((RET_V_WIDTH, D_MODEL), const),
            pl.BlockSpec((D_MODEL, D_MODEL), const),
            pl.BlockSpec((1, D_MODEL), const),
            pl.BlockSpec((ROUTER_ROWS, D_MODEL), const),
            pl.BlockSpec((ROUTER_ROWS, D_MODEL), const),
            pl.BlockSpec((ROUTER_ROWS, 1), const),
            pl.BlockSpec((tm, tm), const),
        ],
        out_specs=[
            pl.BlockSpec((tm, HEXT_COLS), lambda i: (i, 0)),
            pl.BlockSpec((8, tm), lambda i: (0, i)),
            pl.BlockSpec((ROUTER_ROWS, LANES), const),
        ],
        out_shape=[
            jax.ShapeDtypeStruct((t, HEXT_COLS), F32),
            jax.ShapeDtypeStruct((8, t), jnp.int32),
            jax.ShapeDtypeStruct((ROUTER_ROWS, LANES), F32),
        ],
        compiler_params=pltpu.CompilerParams(
            dimension_semantics=("arbitrary",), vmem_limit_bytes=VMEM_LIMIT),
        name="mix_router",
    )(ysb, yret, gates, gates, gates, gates, x2, wsb, wret, wout, gffn, wr_hi, wr_lo, br, tri)


def _moe_kernel(dest_ref, nvalid_ref, ea_ref, eb_ref,
                hext_ref, wga_ref, wua_ref, wda_ref, wgb_ref, wub_ref, wdb_ref,
                gffn_ref, gfin_ref, out_ref, hbuf, obuf, src, gsem, ssem):
    i = pl.program_id(0)
    n_tiles = pl.num_programs(0)
    tm = hbuf.shape[1]
    nv = nvalid_ref[i]
    nxt = jnp.minimum(i + 1, n_tiles - 1)
    nv_next = jnp.where(i + 1 < n_tiles, nvalid_ref[nxt], 0)
    has_next = nv_next > 0

    def gather_copy(tile, buf_slot, r):
        tok = src[tile * tm + r]
        return pltpu.make_async_copy(hext_ref.at[pl.ds(tok, 1), :],
                                     hbuf.at[buf_slot, pl.ds(r, 1), :], gsem.at[buf_slot])

    def scatter_copy(tile, buf_slot, r):
        tok = src[tile * tm + r]
        return pltpu.make_async_copy(obuf.at[buf_slot, pl.ds(r, 1), :],
                                     out_ref.at[pl.ds(tok, 1), :], ssem.at[buf_slot])

    def start_rows(copy_of_row, rows_valid):
        for r in range(tm):
            if rows_valid is None:
                copy_of_row(r).start(priority=r % 2)
            else:
                @pl.when(r < rows_valid)
                def _():
                    copy_of_row(r).start(priority=r % 2)

    def wait_rows(full_copy, row_copy, rows_valid):
        @pl.when(rows_valid == tm)
        def _():
            full_copy.wait()

        @pl.when(rows_valid < tm)
        def _():
            for r in range(tm):
                @pl.when(r < rows_valid)
                def _():
                    row_copy(r).wait()

    def wait_gather(buf_slot, rows_valid):
        wait_rows(pltpu.make_async_copy(hext_ref.at[pl.ds(0, tm), :], hbuf.at[buf_slot], gsem.at[buf_slot]),
                  lambda r: pltpu.make_async_copy(hext_ref.at[pl.ds(0, 1), :],
                                                  hbuf.at[buf_slot, pl.ds(r, 1), :], gsem.at[buf_slot]),
                  rows_valid)

    def wait_scatter(buf_slot, rows_valid):
        wait_rows(pltpu.make_async_copy(obuf.at[buf_slot], out_ref.at[pl.ds(0, tm), :], ssem.at[buf_slot]),
                  lambda r: pltpu.make_async_copy(obuf.at[buf_slot, pl.ds(r, 1), :],
                                                  out_ref.at[pl.ds(0, 1), :], ssem.at[buf_slot]),
                  rows_valid)

    @pl.when(i == 0)
    def _():
        def invert(c, carry):
            for u in range(MOE_INVERT_UNROLL):
                t = c * MOE_INVERT_UNROLL + u
                src[dest_ref[t]] = t
            return carry

        lax.fori_loop(0, dest_ref.shape[0] // MOE_INVERT_UNROLL, invert, 0)
        hbuf[...] = jnp.zeros_like(hbuf)
        start_rows(functools.partial(gather_copy, 0, 0), nv)

    def tile_body(slot):
        @pl.when(i >= 2)
        def _():
            wait_scatter(slot, nvalid_ref[jnp.maximum(i - 2, 0)])

        wait_gather(slot, nv)
        both_full = (nv == tm) & (nv_next == tm)
        pl.when(both_full)(functools.partial(tile_main, slot, None, None))
        pl.when(jnp.logical_not(both_full))(functools.partial(tile_main, slot, nv, nv_next))

        @pl.when(jnp.logical_not(has_next))
        def _():
            @pl.when(i >= 1)
            def _():
                wait_scatter(1 - slot, nvalid_ref[jnp.maximum(i - 1, 0)])

            wait_scatter(slot, nv)

    def tile_main(slot, rows_now, rows_next):
        start_rows(functools.partial(gather_copy, nxt, 1 - slot), rows_next)
        hrows = hbuf[slot]
        h = hrows[:, 0:D_MODEL]
        w_a = hrows[:, D_MODEL:D_MODEL + 1]
        w_b = hrows[:, D_MODEL + 1:D_MODEL + 2]
        ms = jnp.mean(h * h, axis=-1, keepdims=True)
        hn = ((h * lax.rsqrt(ms + EPS)) * gffn_ref[...]).astype(BF16)

        def expert(wg_ref, wu_ref, wd_ref):
            half = jnp.dot(hn, wg_ref[0], preferred_element_type=F32)
            up = jnp.dot(hn, wu_ref[0], preferred_element_type=F32)
            hidden = ((half * (jnp.tanh(half) + 1.0)) * up).astype(BF16)
            return jnp.dot(hidden, wd_ref[0], preferred_element_type=F32)

        y = w_a * expert(wga_ref, wua_ref, wda_ref) + w_b * expert(wgb_ref, wub_ref, wdb_ref)
        h2 = h + y
        ms2 = jnp.mean(h2 * h2, axis=-1, keepdims=True)
        obuf[slot] = (h2 * lax.rsqrt(ms2 + EPS)) * gfin_ref[...]
        start_rows(functools.partial(scatter_copy, i, slot), rows_now)

    for parity in range(2):
        pl.when((nv > 0) & (i % 2 == parity))(functools.partial(tile_body, parity))


def _moe_call(dest, nvalid, ea, eb, hext, wg, wu, wd, gffn, gfin):
    t = hext.shape[0]
    tm = MOE_TM
    n_tiles = nvalid.shape[0]
    assert t % MOE_INVERT_UNROLL == 0
    wa_map = lambda i, dest, nv, ea, eb: (ea[i], 0, 0)
    wb_map = lambda i, dest, nv, ea, eb: (eb[i], 0, 0)
    const = lambda i, dest, nv, ea, eb: (0, 0)
    return pl.pallas_call(
        _moe_kernel,
        grid_spec=pltpu.PrefetchScalarGridSpec(
            num_scalar_prefetch=4,
            grid=(n_tiles,),
            in_specs=[
                pl.BlockSpec(memory_space=pl.ANY),
                pl.BlockSpec((1, D_MODEL, D_FF), wa_map),
                pl.BlockSpec((1, D_MODEL, D_FF), wa_map),
                pl.BlockSpec((1, D_FF, D_MODEL), wa_map),
                pl.BlockSpec((1, D_MODEL, D_FF), wb_map),
                pl.BlockSpec((1, D_MODEL, D_FF), wb_map),
                pl.BlockSpec((1, D_FF, D_MODEL), wb_map),
                pl.BlockSpec((1, D_MODEL), const),
                pl.BlockSpec((1, D_MODEL), const),
            ],
            out_specs=pl.BlockSpec(memory_space=pl.ANY),
            scratch_shapes=[
                pltpu.VMEM((2, tm, HEXT_COLS), F32),
                pltpu.VMEM((2, tm, D_MODEL), F32),
                pltpu.SMEM((n_tiles * tm,), jnp.int32),
                pltpu.SemaphoreType.DMA((2,)),
                pltpu.SemaphoreType.DMA((2,)),
            ],
        ),
        out_shape=jax.ShapeDtypeStruct((t, D_MODEL), F32),
        compiler_params=pltpu.CompilerParams(
            dimension_semantics=("arbitrary",), vmem_limit_bytes=VMEM_LIMIT),
        name="experts",
    )(dest, nvalid, ea, eb, hext, wg, wu, wd, wg, wu, wd, gffn, gfin)


def _routing_plan(cls, rank, counts, tm):
    t = cls.shape[0]
    n_tiles = t // tm + N_CLASSES
    tiles_c = (counts + tm - 1) // tm
    tile_end = jnp.cumsum(tiles_c)
    tile_off = tile_end - tiles_c
    onehot = cls[:, None] == jnp.arange(N_CLASSES, dtype=jnp.int32)[None, :]
    dest = jnp.sum(jnp.where(onehot, tile_off[None, :], 0), axis=1) * tm + rank
    tile = jnp.arange(n_tiles, dtype=jnp.int32)
    used = tile < tile_end[-1]
    tile_cls = jnp.minimum(jnp.sum((tile[:, None] >= tile_end[None, :]).astype(jnp.int32), axis=1),
                           N_CLASSES - 1)
    cls_onehot = tile_cls[:, None] == jnp.arange(N_CLASSES, dtype=jnp.int32)[None, :]
    pick = lambda table: jnp.sum(jnp.where(cls_onehot, table[None, :], 0), axis=1)
    nvalid = jnp.where(used, jnp.clip(pick(counts) - (tile - pick(tile_off)) * tm, 0, tm), 0)
    classes = np.arange(N_CLASSES)
    ea_tab = jnp.asarray((classes // PAIRS_PER_GROUP) * EXPERTS_PER_GROUP
                         + np.asarray(_PAIR_A)[classes % PAIRS_PER_GROUP], jnp.int32)
    eb_tab = jnp.asarray((classes // PAIRS_PER_GROUP) * EXPERTS_PER_GROUP
                         + np.asarray(_PAIR_B)[classes % PAIRS_PER_GROUP], jnp.int32)
    last_used = jnp.sum(jnp.where(tile == tile_end[-1] - 1, tile_cls, 0))
    tile_cls = jnp.where(used, tile_cls, last_used)
    cls_onehot = tile_cls[:, None] == jnp.arange(N_CLASSES, dtype=jnp.int32)[None, :]
    return dest.astype(jnp.int32), nvalid.astype(jnp.int32), pick(ea_tab), pick(eb_tab)


def _rope_tables(seq):
    half = RET_QK_DIM // 2
    inv_freq = ROPE_BASE ** (-jnp.arange(half, dtype=F32) / half)
    ang = jnp.arange(seq, dtype=F32)[:, None] * inv_freq[None, :]
    cos, sin = jnp.cos(ang), jnp.sin(ang)
    cos_t = jnp.tile(cos, (1, LANES // half))
    sin_t = jnp.concatenate([-sin, -sin, sin, sin], axis=1)
    return cos_t, sin_t


def kernel(x, norm_mix_g, w_in, w_gate, b_gate, w_sb_out, w_ret_out, ret_norm_g, w_out,
           norm_ffn_g, w_group_router, b_group_router, w_expert_router, b_expert_router,
           w_exp_gate, w_exp_up, w_exp_down, norm_final_g):
    bsz, seq, d = x.shape
    t = bsz * seq
    assert d == D_MODEL and w_in.shape[0] == 1 and t % MIX_TM == 0
    assert seq % (SB_GROUP * SB_BLK) == 0 and seq % (RET_BLK * RET_UNROLL) == 0
    x2 = x.reshape(t, d)
    wi = w_in[0]
    c_rq = 3 * SB_WIDTH
    c_rk = c_rq + RET_QK_WIDTH
    c_rv = c_rk + RET_QK_WIDTH
    c_rg = c_rv + RET_V_WIDTH
    ones = functools.partial(jnp.ones, dtype=F32)
    w_plain = jnp.concatenate([wi[:, :c_rq], wi[:, c_rv:c_rg]], axis=1).astype(BF16)
    s_plain = jnp.concatenate([jnp.full((SB_WIDTH,), SB_HEAD_DIM ** -0.5 * LOG2E, F32),
                               ones((2 * SB_WIDTH + RET_V_WIDTH,))])[None, :]
    perm = _rot_perm()
    w_rot = jnp.concatenate([wi[:, c_rq:c_rk][:, perm],
                             (RET_QK_DIM ** -0.5) * wi[:, c_rk:c_rv][:, perm]], axis=1).astype(BF16)
    w_swish_half = (0.5 * wi[:, c_rg:]).astype(BF16)
    w_gate_half = (0.5 * w_gate[0]).astype(BF16)
    b_gate_half = (0.5 * b_gate[0])[None, :]
    cos_t, sin_t = _rope_tables(seq)

    plain, xn = _proj_plain_call(x2, norm_mix_g[0][None, :], w_plain, s_plain, seq)
    rqk = _proj_rot_call(xn, w_rot, cos_t, sin_t, seq)
    swish = _proj_swish_call(xn, w_swish_half, seq)
    gates = _proj_gate_call(xn, w_gate_half, b_gate_half, seq)
    ysb = _sb_call(plain, seq)
    yret = _ret_call(rqk, plain, swish, ret_norm_g[0][None, :], seq)

    wr = jnp.concatenate([w_group_router[0], w_expert_router[0]], axis=1).T
    wr = jnp.pad(wr, ((0, ROUTER_ROWS - wr.shape[0]), (0, 0)))
    wr_hi = wr.astype(BF16)
    wr_lo = (wr - wr_hi.astype(F32)).astype(BF16)
    br = jnp.concatenate([b_group_router[0], b_expert_router[0]])
    br = jnp.pad(br, (0, ROUTER_ROWS - br.shape[0]))[:, None]
    hext, cls8, cnt = _mix_call(ysb, yret, gates, x2, w_sb_out[0].astype(BF16), w_ret_out[0].astype(BF16),
                           w_out[0].astype(BF16), norm_ffn_g[0][None, :], wr_hi, wr_lo, br)

    counts = cnt[:N_CLASSES, 0].astype(jnp.int32)
    dest, nvalid, ea, eb = _routing_plan(cls8[0], cls8[1], counts, MOE_TM)
    out = _moe_call(dest, nvalid, ea, eb, hext, (0.5 * w_exp_gate[0]).astype(BF16), w_exp_up[0].astype(BF16),
                    w_exp_down[0].astype(BF16), norm_ffn_g[0][None, :], norm_final_g[None, :])
    return out.reshape(bsz, seq, d)
```

```python
import functools
import math

import numpy as np
import jax
import jax.numpy as jnp
from jax import lax
from jax.experimental import pallas as pl
from jax.experimental.pallas import tpu as pltpu

F32 = jnp.float32
BF16 = jnp.bfloat16

D_MODEL = 1024
SB_HEADS = 8
SB_HEAD_DIM = 64
SB_WIDTH = SB_HEADS * SB_HEAD_DIM
RET_HEADS = 8
RET_QK_DIM = 64
RET_V_DIM = 128
RET_QK_WIDTH = RET_HEADS * RET_QK_DIM
RET_V_WIDTH = RET_HEADS * RET_V_DIM
IN_COLS = 3 * SB_WIDTH + 2 * RET_QK_WIDTH + 2 * RET_V_WIDTH
GATE_COLS = 2 * D_MODEL
ALL_COLS = IN_COLS + GATE_COLS
CHUNK = 64
ROPE_BASE = 10000.0
N_GROUPS = 4
EXPERTS_PER_GROUP = 4
N_EXPERTS = N_GROUPS * EXPERTS_PER_GROUP
D_FF = 512
EPS = 1e-6

LANES = 128
PROJ_TN = 1280
GATE_BLK = 512
PROJ_ROWS = 256
SB_BLK = 256
SB_GROUP = 4
RET_BLK = 256
RET_UNROLL = 8
MIX_TM = 512
MOE_TM = 256
MOE_INVERT_UNROLL = 32
ROUTER_ROWS = 32
PAIRS_PER_GROUP = 6
N_CLASSES = N_GROUPS * PAIRS_PER_GROUP
INFO_COLS = LANES
HEXT_COLS = D_MODEL + INFO_COLS
VMEM_LIMIT = 56 * 1024 * 1024
LOG2E = math.log2(math.e)
SB_SKIP_LOG2 = 156.0

_PAIR_A = (0, 0, 0, 1, 1, 2)
_PAIR_B = (1, 2, 3, 2, 3, 3)


def _proj_plain_kernel(x_ref, g_ref, w_ref, s_ref, o_ref, xn_ref):
    def body(normalise):
        for rows in _row_chunks(xn_ref.shape[0]):
            if normalise:
                x = x_ref[rows, :]
                ms = jnp.mean(x * x, axis=-1, keepdims=True)
                xn_ref[rows, :] = ((x * lax.rsqrt(ms + EPS)) * g_ref[...]).astype(BF16)
            acc = jnp.dot(xn_ref[rows, :], w_ref[...], preferred_element_type=F32)
            o_ref[rows, :] = (acc * s_ref[...]).astype(BF16)

    first = pl.program_id(1) == 0
    pl.when(first)(functools.partial(body, True))
    pl.when(jnp.logical_not(first))(functools.partial(body, False))


def _row_chunks(n_rows):
    return [slice(r, r + PROJ_ROWS) for r in range(0, n_rows, PROJ_ROWS)]


def _proj_rot_kernel(xn_ref, w_ref, cos_ref, sin_ref, o_ref):
    for rows in _row_chunks(xn_ref.shape[0]):
        acc = jnp.dot(xn_ref[rows, :], w_ref[...], preferred_element_type=F32)
        cos = cos_ref[rows, :]
        sin = sin_ref[rows, :]
        for p in range(w_ref.shape[1] // LANES):
            cols = slice(p * LANES, (p + 1) * LANES)
            seg = acc[:, cols]
            o_ref[rows, cols] = (seg * cos + pltpu.roll(seg, LANES // 2, 1) * sin).astype(BF16)


def _proj_swish_kernel(xn_ref, w_ref, o_ref):
    for rows in _row_chunks(xn_ref.shape[0]):
        half = jnp.dot(xn_ref[rows, :], w_ref[...], preferred_element_type=F32)
        o_ref[rows, :] = (half * (jnp.tanh(half) + 1.0)).astype(BF16)


def _proj_gate_kernel(xn_ref, w_ref, b_ref, o_ref):
    for rows in _row_chunks(xn_ref.shape[0]):
        half = jnp.dot(xn_ref[rows, :], w_ref[...], preferred_element_type=F32) + b_ref[...]
        o_ref[rows, :] = (0.5 * jnp.tanh(half) + 0.5).astype(BF16)


def _proj_params():
    return pltpu.CompilerParams(dimension_semantics=("arbitrary", "arbitrary"),
                                vmem_limit_bytes=VMEM_LIMIT)


def _proj_tile(n):
    tn = PROJ_TN
    while n % tn:
        tn -= LANES
    return tn


def _proj_plain_call(x2, g, w, scale, seq):
    t, n = x2.shape[0], w.shape[1]
    tn = _proj_tile(n)
    return pl.pallas_call(
        _proj_plain_kernel,
        grid=(t // seq, n // tn),
        in_specs=[
            pl.BlockSpec((seq, D_MODEL), lambda i, j: (i, 0)),
            pl.BlockSpec((1, D_MODEL), lambda i, j: (0, 0)),
            pl.BlockSpec((D_MODEL, tn), lambda i, j: (0, j)),
            pl.BlockSpec((1, tn), lambda i, j: (0, j)),
        ],
        out_specs=[
            pl.BlockSpec((seq, tn), lambda i, j: (i, j)),
            pl.BlockSpec((seq, D_MODEL), lambda i, j: (i, 0)),
        ],
        out_shape=[jax.ShapeDtypeStruct((t, n), BF16), jax.ShapeDtypeStruct((t, D_MODEL), BF16)],
        compiler_params=_proj_params(),
        name="proj_plain",
    )(x2, g, w, scale)


def _proj_rot_call(xn, w, cos_t, sin_t, seq):
    t, n = xn.shape[0], w.shape[1]
    tn = _proj_tile(n)
    return pl.pallas_call(
        _proj_rot_kernel,
        grid=(t // seq, n // tn),
        in_specs=[
            pl.BlockSpec((seq, D_MODEL), lambda i, j: (i, 0)),
            pl.BlockSpec((D_MODEL, tn), lambda i, j: (0, j)),
            pl.BlockSpec((seq, LANES), lambda i, j: (0, 0)),
            pl.BlockSpec((seq, LANES), lambda i, j: (0, 0)),
        ],
        out_specs=pl.BlockSpec((seq, tn), lambda i, j: (i, j)),
        out_shape=jax.ShapeDtypeStruct((t, n), BF16),
        compiler_params=_proj_params(),
        name="proj_rotary",
    )(xn, w, cos_t, sin_t)


def _proj_swish_call(xn, w_half, seq):
    t, n = xn.shape[0], w_half.shape[1]
    tn = _proj_tile(n)
    return pl.pallas_call(
        _proj_swish_kernel,
        grid=(t // seq, n // tn),
        in_specs=[
            pl.BlockSpec((seq, D_MODEL), lambda i, j: (i, 0)),
            pl.BlockSpec((D_MODEL, tn), lambda i, j: (0, j)),
        ],
        out_specs=pl.BlockSpec((seq, tn), lambda i, j: (i, j)),
        out_shape=jax.ShapeDtypeStruct((t, n), BF16),
        compiler_params=_proj_params(),
        name="proj_swish",
    )(xn, w_half)


def _proj_gate_call(xn, w_half, bias_half, seq):
    t, n = xn.shape[0], w_half.shape[1]
    tn = _proj_tile(n)
    return pl.pallas_call(
        _proj_gate_kernel,
        grid=(t // seq, n // tn),
        in_specs=[
            pl.BlockSpec((seq, D_MODEL), lambda i, j: (i, 0)),
            pl.BlockSpec((D_MODEL, tn), lambda i, j: (0, j)),
            pl.BlockSpec((1, tn), lambda i, j: (0, j)),
        ],
        out_specs=pl.BlockSpec((seq, tn), lambda i, j: (i, j)),
        out_shape=jax.ShapeDtypeStruct((t, n), BF16),
        compiler_params=_proj_params(),
        name="proj_gate",
    )(xn, w_half, bias_half)


def _rot_perm():
    half = RET_QK_DIM // 2
    order = [p * LANES + hh * RET_QK_DIM + part * half + d
             for p in range(RET_QK_WIDTH // LANES) for part in range(2) for hh in range(2)
             for d in range(half)]
    return np.asarray(order, np.int32)


def _sb_kernel(q_ref, k_ref, v_ref, o_ref, u_ref, acc_ref, car_ref):
    seq = q_ref.shape[0]
    nq = seq // SB_BLK
    row = lax.broadcasted_iota(jnp.int32, (SB_BLK, SB_BLK), 0)
    col = lax.broadcasted_iota(jnp.int32, (SB_BLK, SB_BLK), 1)
    u_ref[...] = (row >= col).astype(BF16)
    lane = lax.broadcasted_iota(jnp.int32, (1, LANES), 1)
    head_masks = (lane < SB_HEAD_DIM, lane >= SB_HEAD_DIM)

    heads = range(2)
    causal = col < row

    def rows_of(blk):
        return pl.ds(pl.multiple_of(blk * SB_BLK, SB_BLK), SB_BLK)

    def q_heads_of(qi):
        q_blk = q_ref[rows_of(qi), :]
        return tuple(jnp.where(m, q_blk, jnp.zeros_like(q_blk)) for m in head_masks)

    def sweep(jobs):
        u = u_ref[...]
        q_heads = [q_heads_of(qi) for _, qi, _, _ in jobs]
        z2 = [[[lax.dot_general(q_heads[j][h], k_ref[rows_of(kb), :], (((1,), (1,)), ((), ())),
                                preferred_element_type=F32) for h in heads]
               for kb, _ in steps] for j, (_, _, steps, _) in enumerate(jobs)]
        sinc = []
        for j, (_, _, steps, _) in enumerate(jobs):
            sinc.append([])
            for i, (_, diagonal) in enumerate(steps):
                nlk = [jnp.maximum(z, 0.0) + jnp.log2(1.0 + jnp.exp2(-jnp.abs(z))) for z in z2[j][i]]
                if diagonal:
                    nlk = [jnp.where(causal, a, 0.0) for a in nlk]
                sinc[j].append([jnp.dot(a.astype(BF16), u, preferred_element_type=F32) for a in nlk])
        for j, (slot, _, steps, fresh) in enumerate(jobs):
            car = [None, None] if fresh else [car_ref[slot, h] for h in heads]
            contrib = None
            for i, (kb, diagonal) in enumerate(steps):
                v_blk = v_ref[rows_of(kb), :]
                for h in heads:
                    e = z2[j][i][h] - sinc[j][i][h]
                    if car[h] is not None:
                        e = e - jnp.concatenate([car[h], car[h]], axis=1)
                    w = jnp.exp2(e)
                    if diagonal:
                        w = jnp.where(causal, w, 0.0)
                    vh = jnp.where(head_masks[h], v_blk, jnp.zeros_like(v_blk))
                    pv = jnp.dot(w.astype(BF16), vh, preferred_element_type=F32)
                    contrib = pv if contrib is None else contrib + pv
                    tot = jnp.broadcast_to(sinc[j][i][h][:, 0:1], (SB_BLK, LANES))
                    car[h] = tot if car[h] is None else car[h] + tot
            for h in heads:
                car_ref[slot, h] = car[h]
            if fresh:
                acc_ref[slot] = contrib
            else:
                acc_ref[slot] += contrib

    def finish(jobs):
        pending = [job for job in jobs if job[2] is not None]
        if pending:
            lo, hi = pending[0][0], pending[-1][0] + 1

            @pl.when(jnp.min(car_ref[lo:hi]) < SB_SKIP_LOG2)
            def _():
                for slot, qi, first_kb in pending:
                    def more(c):
                        kb, min_carry = c
                        return (kb >= 0) & (min_carry < SB_SKIP_LOG2)

                    def k_step(c, slot=slot, qi=qi):
                        kb, _ = c
                        sweep([(slot, qi, [(kb, False)], False)])
                        return kb - 1, jnp.min(car_ref[slot])

                    lax.while_loop(more, k_step, (first_kb, jnp.min(car_ref[slot])))
        for slot, qi, _ in jobs:
            o_ref[rows_of(qi), :] = acc_ref[slot].astype(BF16)

    def first_steps(qi):
        return [(qi, True)] + ([(qi - 1, False)] if qi > 0 else [])

    sweep([(s, s, first_steps(s), True) for s in range(SB_GROUP)])
    finish([(s, s, s - 2 if s >= 2 else None) for s in range(SB_GROUP)])

    def q_group(g, carry):
        q0 = g * SB_GROUP
        sweep([(s, q0 + s, [(q0 + s, True), (q0 + s - 1, False)], True) for s in range(SB_GROUP)])
        finish([(s, q0 + s, q0 + s - 2) for s in range(SB_GROUP)])
        return carry

    lax.fori_loop(1, nq // SB_GROUP, q_group, 0)


def _sb_call(proj, seq):
    t = proj.shape[0]
    n_pairs = SB_WIDTH // LANES
    return pl.pallas_call(
        _sb_kernel,
        grid=(t // seq, n_pairs),
        in_specs=[
            pl.BlockSpec((seq, LANES), lambda b, p: (b, p)),
            pl.BlockSpec((seq, LANES), lambda b, p: (b, n_pairs + p)),
            pl.BlockSpec((seq, LANES), lambda b, p: (b, 2 * n_pairs + p)),
        ],
        out_specs=pl.BlockSpec((seq, LANES), lambda b, p: (b, p)),
        out_shape=jax.ShapeDtypeStruct((t, SB_WIDTH), BF16),
        scratch_shapes=[
            pltpu.VMEM((SB_BLK, SB_BLK), BF16),
            pltpu.VMEM((SB_GROUP, SB_BLK, LANES), F32),
            pltpu.VMEM((SB_GROUP, 2, SB_BLK, LANES), F32),
        ],
        compiler_params=pltpu.CompilerParams(
            dimension_semantics=("arbitrary", "arbitrary"), vmem_limit_bytes=VMEM_LIMIT),
        name="stickbreak",
    )(proj, proj, proj)


def _ret_tables():
    h = np.arange(RET_HEADS, dtype=np.float64)
    log_gamma = np.log(1.0 - 2.0 ** (-5.0 - h))
    idx = np.arange(RET_BLK, dtype=np.float64)
    t, s = idx[:, None], idx[None, :]
    same = (t // CHUNK) == (s // CHUNK)
    earlier = (s // CHUNK) < (t // CHUNK)
    expo = np.where(same, np.abs(t - s), np.where(earlier, t - s, 0.0))
    dmat = np.exp(log_gamma[:, None, None] * expo) * (same | earlier)
    qdec = np.exp(log_gamma[:, None] * (idx + 1.0)[None, :])
    kdec = np.exp(log_gamma[:, None] * (RET_BLK - 1.0 - idx)[None, :])
    cdec = np.exp(log_gamma * RET_BLK)
    n_pairs = RET_HEADS // 2
    lane_head = (np.arange(LANES) % RET_QK_DIM) // (RET_QK_DIM // 2)
    col_head = np.arange(2 * RET_V_DIM) // RET_V_DIM
    pair_heads = np.arange(RET_HEADS).reshape(n_pairs, 2)
    qdec_pair = np.stack([qdec[pair_heads[p][lane_head]].T for p in range(n_pairs)])
    kdec_pair = np.stack([kdec[pair_heads[p][lane_head]].T for p in range(n_pairs)])
    cdec_pair = np.stack([cdec[pair_heads[p][col_head]][None, :] for p in range(n_pairs)])
    return (jnp.asarray(dmat.reshape((n_pairs, 2, RET_BLK, RET_BLK)), F32),
            jnp.asarray(qdec_pair, F32), jnp.asarray(kdec_pair, F32), jnp.asarray(cdec_pair, F32))


def _ret_kernel(q_ref, k_ref, v_ref, g_ref, gn_ref, dm_ref, qd_ref, kd_ref, cd_ref, o_ref, st_ref):
    seq = q_ref.shape[0]
    lane = lax.broadcasted_iota(jnp.int32, (1, LANES), 1)
    head0 = (lane % RET_QK_DIM) < (RET_QK_DIM // 2)
    head_masks = (head0, jnp.logical_not(head0))
    st_row = lax.broadcasted_iota(jnp.int32, st_ref.shape, 0)
    st_col = lax.broadcasted_iota(jnp.int32, st_ref.shape, 1)
    own_head = ((st_row % RET_QK_DIM) // (RET_QK_DIM // 2)) == (st_col // RET_V_DIM)
    heads = range(2)
    nt = (((1,), (1,)), ((), ()))
    tn = (((0,), (0,)), ((), ()))
    st_ref[...] = jnp.zeros_like(st_ref)

    def step(n, carry):
        subs = range(RET_UNROLL)
        rows = [pl.ds(pl.multiple_of((n * RET_UNROLL + u) * RET_BLK, RET_BLK), RET_BLK) for u in subs]
        col = [slice(h * RET_V_DIM, (h + 1) * RET_V_DIM) for h in heads]
        q_blk = [q_ref[r, :] for r in rows]
        k_blk = [k_ref[r, :] for r in rows]
        qm = [[jnp.where(head_masks[h], q_blk[u], jnp.zeros_like(q_blk[u])) for h in heads] for u in subs]
        v = [v_ref[r, :] for r in rows]
        scores = [[lax.dot_general(qm[u][h], k_blk[u], nt, preferred_element_type=F32)
                   for h in heads] for u in subs]
        kd = [(k_blk[u].astype(F32) * kd_ref[0]).astype(BF16) for u in subs]
        kv = [jnp.where(own_head, lax.dot_general(kd[u], v[u], tn, preferred_element_type=F32), 0.0)
              for u in subs]
        qd = [(q_blk[u].astype(F32) * qd_ref[0]).astype(BF16) for u in subs]
        p = [[(scores[u][h] * dm_ref[0, h]).astype(BF16) for h in heads] for u in subs]
        intra = [[jnp.dot(p[u][h], v[u][:, col[h]], preferred_element_type=F32) for h in heads]
                 for u in subs]
        state = st_ref[...]
        for u in subs:
            cross = jnp.dot(qd[u], state.astype(BF16), preferred_element_type=F32)
            state = state * cd_ref[0] + kv[u]
            for h in heads:
                y = intra[u][h] + cross[:, col[h]]
                ms = jnp.mean(y * y, axis=-1, keepdims=True)
                yn = (y * lax.rsqrt(ms + EPS)) * gn_ref[:, col[h]]
                o_ref[rows[u], col[h]] = (g_ref[rows[u], col[h]].astype(F32) * yn).astype(BF16)
        st_ref[...] = state
        return carry

    lax.fori_loop(0, seq // (RET_BLK * RET_UNROLL), step, 0)


def _ret_call(rqk, plain, swish, ret_norm_g, seq):
    t = rqk.shape[0]
    n_pairs = RET_HEADS // 2
    dmat, qdec, kdec, cdec = _ret_tables()
    k0 = RET_QK_WIDTH // LANES
    v0 = (3 * SB_WIDTH) // (2 * RET_V_DIM)
    return pl.pallas_call(
        _ret_kernel,
        grid=(t // seq, n_pairs),
        in_specs=[
            pl.BlockSpec((seq, LANES), lambda b, p: (b, p)),
            pl.BlockSpec((seq, LANES), lambda b, p: (b, k0 + p)),
            pl.BlockSpec((seq, 2 * RET_V_DIM), lambda b, p: (b, v0 + p)),
            pl.BlockSpec((seq, 2 * RET_V_DIM), lambda b, p: (b, p)),
            pl.BlockSpec((1, 2 * RET_V_DIM), lambda b, p: (0, p)),
            pl.BlockSpec((1, 2, RET_BLK, RET_BLK), lambda b, p: (p, 0, 0, 0)),
            pl.BlockSpec((1, RET_BLK, LANES), lambda b, p: (p, 0, 0)),
            pl.BlockSpec((1, RET_BLK, LANES), lambda b, p: (p, 0, 0)),
            pl.BlockSpec((1, 1, 2 * RET_V_DIM), lambda b, p: (p, 0, 0)),
        ],
        out_specs=pl.BlockSpec((seq, 2 * RET_V_DIM), lambda b, p: (b, p)),
        out_shape=jax.ShapeDtypeStruct((t, RET_V_WIDTH), BF16),
        scratch_shapes=[pltpu.VMEM((LANES, 2 * RET_V_DIM), F32)],
        compiler_params=pltpu.CompilerParams(
            dimension_semantics=("arbitrary", "arbitrary"), vmem_limit_bytes=VMEM_LIMIT),
        name="retention",
    )(rqk, rqk, plain, swish, ret_norm_g, dmat, qdec, kdec, cdec)


def _mix_kernel(ysb_ref, yret_ref, gs0_ref, gs1_ref, gr0_ref, gr1_ref, x_ref,
                wsb_ref, wret_ref, wout_ref, gffn_ref, wrh_ref, wrl_ref, br_ref, tri_ref,
                hext_ref, cls_ref, cnt_ref):
    tm = x_ref.shape[0]

    @pl.when(pl.program_id(0) == 0)
    def _():
        cnt_ref[...] = jnp.zeros_like(cnt_ref)

    a = jnp.dot(ysb_ref[...], wsb_ref[...], preferred_element_type=F32)
    b = jnp.dot(yret_ref[...], wret_ref[...], preferred_element_type=F32)
    g_sb = jnp.concatenate([gs0_ref[...], gs1_ref[...]], axis=1).astype(F32)
    g_ret = jnp.concatenate([gr0_ref[...], gr1_ref[...]], axis=1).astype(F32)
    mixed = (g_sb * a + g_ret * b).astype(BF16)
    h = x_ref[...] + jnp.dot(mixed, wout_ref[...], preferred_element_type=F32)
    hext_ref[:, 0:D_MODEL] = h

    ms = jnp.mean(h * h, axis=-1, keepdims=True)
    hn = (h * lax.rsqrt(ms + EPS)) * gffn_ref[...]
    hn_hi = hn.astype(BF16)
    hn_lo = (hn - hn_hi.astype(F32)).astype(BF16)
    nt = (((1,), (1,)), ((), ()))
    w_both = jnp.concatenate([wrh_ref[...], wrl_ref[...]], axis=0)
    by_hi = lax.dot_general(w_both, hn_hi, nt, preferred_element_type=F32)
    logits = (by_hi[0:ROUTER_ROWS] + by_hi[ROUTER_ROWS:2 * ROUTER_ROWS]
              + lax.dot_general(wrh_ref[...], hn_lo, nt, preferred_element_type=F32)
              + br_ref[...])

    def first_argmax(vals):
        m = functools.reduce(jnp.maximum, vals)
        idx = jnp.full(m.shape, len(vals) - 1, jnp.int32)
        for i in range(len(vals) - 2, -1, -1):
            idx = jnp.where(vals[i] >= m, i, idx)
        return m, idx

    gl = [logits[r:r + 1, :] for r in range(N_GROUPS)]
    gmax, gsel = first_argmax(gl)
    p_group = 1.0 / functools.reduce(lambda s, v: s + v, [jnp.exp(v - gmax) for v in gl])
    el = []
    for e in range(EXPERTS_PER_GROUP):
        v = logits[N_GROUPS + 3 * EXPERTS_PER_GROUP + e:N_GROUPS + 3 * EXPERTS_PER_GROUP + e + 1, :]
        for g in range(N_GROUPS - 2, -1, -1):
            r = N_GROUPS + g * EXPERTS_PER_GROUP + e
            v = jnp.where(gsel == g, logits[r:r + 1, :], v)
        el.append(v)
    m1, i1 = first_argmax(el)
    rest = [jnp.where(i1 == e, -jnp.inf, el[e]) for e in range(EXPERTS_PER_GROUP)]
    m2, i2 = first_argmax(rest)
    tt = jnp.exp(m2 - m1)
    gate1 = p_group * (1.0 / (1.0 + tt))
    gate2 = p_group * (tt / (1.0 + tt))
    lo_first = i1 < i2
    ea = jnp.minimum(i1, i2)
    eb = jnp.maximum(i1, i2)
    w_a = jnp.where(lo_first, gate1, gate2)
    w_b = jnp.where(lo_first, gate2, gate1)
    pid = jnp.where(ea == 0, eb - 1, jnp.where(ea == 1, eb + 1, PAIRS_PER_GROUP - 1))
    cls = gsel * PAIRS_PER_GROUP + pid
    info = jnp.concatenate([w_a, w_b, jnp.zeros((INFO_COLS - 2, tm), F32)], axis=0)
    hext_ref[:, D_MODEL:HEXT_COLS] = info.T

    class_row = lax.broadcasted_iota(jnp.int32, (ROUTER_ROWS, tm), 0)
    onehot = class_row == cls
    onehot_bf = onehot.astype(BF16)
    before = jnp.dot(onehot_bf, tri_ref[...], preferred_element_type=F32)
    cnt = cnt_ref[...]
    seen = before + jnp.concatenate([cnt] * (tm // LANES), axis=1)
    rank = jnp.sum(jnp.where(onehot, seen, 0.0), axis=0, keepdims=True)
    cnt_ref[...] = cnt + jnp.dot(onehot_bf, jnp.ones((tm, LANES), BF16), preferred_element_type=F32)
    cls_ref[...] = jnp.concatenate(
        [cls, rank.astype(jnp.int32), jnp.zeros((cls_ref.shape[0] - 2, tm), jnp.int32)], axis=0)


def _mix_call(ysb, yret, gates, x2, wsb, wret, wout, gffn, wr_hi, wr_lo, br):
    t = x2.shape[0]
    tm = MIX_TM
    gate0 = 0
    const = lambda i: (0, 0)
    idx = jnp.arange(tm, dtype=jnp.int32)
    tri = (idx[:, None] < idx[None, :]).astype(BF16)
    return pl.pallas_call(
        _mix_kernel,
        grid=(t // tm,),
        in_specs=[
            pl.BlockSpec((tm, SB_WIDTH), lambda i: (i, 0)),
            pl.BlockSpec((tm, RET_V_WIDTH), lambda i: (i, 0)),
            pl.BlockSpec((tm, GATE_BLK), lambda i: (i, gate0)),
            pl.BlockSpec((tm, GATE_BLK), lambda i: (i, gate0 + 1)),
            pl.BlockSpec((tm, GATE_BLK), lambda i: (i, gate0 + 2)),
            pl.BlockSpec((tm, GATE_BLK), lambda i: (i, gate0 + 3)),
            pl.BlockSpec((tm, D_MODEL), lambda i: (i, 0)),
            pl.BlockSpec((SB_WIDTH, D_MODEL), const),
            pl.BlockSpec((RET_V_WIDTH, D_MODEL), const),
            pl.BlockSpec((D_MODEL, D_MODEL), const),
            pl.BlockSpec((1, D_MODEL), const),
            pl.BlockSpec((ROUTER_ROWS, D_MODEL), const),
            pl.BlockSpec((ROUTER_ROWS, D_MODEL), const),
            pl.BlockSpec((ROUTER_ROWS, 1), const),
            pl.BlockSpec((tm, tm), const),
        ],
        out_specs=[
            pl.BlockSpec((tm, HEXT_COLS), lambda i: (i, 0)),
            pl.BlockSpec((8, tm), lambda i: (0, i)),
            pl.BlockSpec((ROUTER_ROWS, LANES), const),
        ],
        out_shape=[
            jax.ShapeDtypeStruct((t, HEXT_COLS), F32),
            jax.ShapeDtypeStruct((8, t), jnp.int32),
            jax.ShapeDtypeStruct((ROUTER_ROWS, LANES), F32),
        ],
        compiler_params=pltpu.CompilerParams(
            dimension_semantics=("arbitrary",), vmem_limit_bytes=VMEM_LIMIT),
        name="mix_router",
    )(ysb, yret, gates, gates, gates, gates, x2, wsb, wret, wout, gffn, wr_hi, wr_lo, br, tri)


def _moe_kernel(dest_ref, nvalid_ref, ea_ref, eb_ref,
                hext_ref, wga_ref, wua_ref, wda_ref, wgb_ref, wub_ref, wdb_ref,
                gffn_ref, gfin_ref, out_ref, hbuf, obuf, src, gsem, ssem):
    i = pl.program_id(0)
    n_tiles = pl.num_programs(0)
    tm = hbuf.shape[1]
    nv = nvalid_ref[i]
    nxt = jnp.minimum(i + 1, n_tiles - 1)
    nv_next = jnp.where(i + 1 < n_tiles, nvalid_ref[nxt], 0)
    has_next = nv_next > 0

    def gather_copy(tile, buf_slot, r):
        tok = src[tile * tm + r]
        return pltpu.make_async_copy(hext_ref.at[pl.ds(tok, 1), :],
                                     hbuf.at[buf_slot, pl.ds(r, 1), :], gsem.at[buf_slot])

    def scatter_copy(tile, buf_slot, r):
        tok = src[tile * tm + r]
        return pltpu.make_async_copy(obuf.at[buf_slot, pl.ds(r, 1), :],
                                     out_ref.at[pl.ds(tok, 1), :], ssem.at[buf_slot])

    def start_rows(copy_of_row, rows_valid):
        for r in range(tm):
            if rows_valid is None:
                copy_of_row(r).start(priority=r % 2)
            else:
                @pl.when(r < rows_valid)
                def _():
                    copy_of_row(r).start(priority=r % 2)

    def wait_rows(full_copy, row_copy, rows_valid):
        @pl.when(rows_valid == tm)
        def _():
            full_copy.wait()

        @pl.when(rows_valid < tm)
        def _():
            for r in range(tm):
                @pl.when(r < rows_valid)
                def _():
                    row_copy(r).wait()

    def wait_gather(buf_slot, rows_valid):
        wait_rows(pltpu.make_async_copy(hext_ref.at[pl.ds(0, tm), :], hbuf.at[buf_slot], gsem.at[buf_slot]),
                  lambda r: pltpu.make_async_copy(hext_ref.at[pl.ds(0, 1), :],
                                                  hbuf.at[buf_slot, pl.ds(r, 1), :], gsem.at[buf_slot]),
                  rows_valid)

    def wait_scatter(buf_slot, rows_valid):
        wait_rows(pltpu.make_async_copy(obuf.at[buf_slot], out_ref.at[pl.ds(0, tm), :], ssem.at[buf_slot]),
                  lambda r: pltpu.make_async_copy(obuf.at[buf_slot, pl.ds(r, 1), :],
                                                  out_ref.at[pl.ds(0, 1), :], ssem.at[buf_slot]),
                  rows_valid)

    @pl.when(i == 0)
    def _():
        def invert(c, carry):
            for u in range(MOE_INVERT_UNROLL):
                t = c * MOE_INVERT_UNROLL + u
                src[dest_ref[t]] = t
            return carry

        lax.fori_loop(0, dest_ref.shape[0] // MOE_INVERT_UNROLL, invert, 0)
        hbuf[...] = jnp.zeros_like(hbuf)
        start_rows(functools.partial(gather_copy, 0, 0), nv)

    def tile_body(slot):
        @pl.when(i >= 2)
        def _():
            wait_scatter(slot, nvalid_ref[jnp.maximum(i - 2, 0)])

        wait_gather(slot, nv)
        both_full = (nv == tm) & (nv_next == tm)
        pl.when(both_full)(functools.partial(tile_main, slot, None, None))
        pl.when(jnp.logical_not(both_full))(functools.partial(tile_main, slot, nv, nv_next))

        @pl.when(jnp.logical_not(has_next))
        def _():
            @pl.when(i >= 1)
            def _():
                wait_scatter(1 - slot, nvalid_ref[jnp.maximum(i - 1, 0)])

            wait_scatter(slot, nv)

    def tile_main(slot, rows_now, rows_next):
        start_rows(functools.partial(gather_copy, nxt, 1 - slot), rows_next)
        hrows = hbuf[slot]
        h = hrows[:, 0:D_MODEL]
        w_a = hrows[:, D_MODEL:D_MODEL + 1]
        w_b = hrows[:, D_MODEL + 1:D_MODEL + 2]
        ms = jnp.mean(h * h, axis=-1, keepdims=True)
        hn = ((h * lax.rsqrt(ms + EPS)) * gffn_ref[...]).astype(BF16)

        def expert(wg_ref, wu_ref, wd_ref):
            half = jnp.dot(hn, wg_ref[0], preferred_element_type=F32)
            up = jnp.dot(hn, wu_ref[0], preferred_element_type=F32)
            hidden = ((half * (jnp.tanh(half) + 1.0)) * up).astype(BF16)
            return jnp.dot(hidden, wd_ref[0], preferred_element_type=F32)

        y = w_a * expert(wga_ref, wua_ref, wda_ref) + w_b * expert(wgb_ref, wub_ref, wdb_ref)
        h2 = h + y
        ms2 = jnp.mean(h2 * h2, axis=-1, keepdims=True)
        obuf[slot] = (h2 * lax.rsqrt(ms2 + EPS)) * gfin_ref[...]
        start_rows(functools.partial(scatter_copy, i, slot), rows_now)

    for parity in range(2):
        pl.when((nv > 0) & (i % 2 == parity))(functools.partial(tile_body, parity))


def _moe_call(dest, nvalid, ea, eb, hext, wg, wu, wd, gffn, gfin):
    t = hext.shape[0]
    tm = MOE_TM
    n_tiles = nvalid.shape[0]
    assert t % MOE_INVERT_UNROLL == 0
    wa_map = lambda i, dest, nv, ea, eb: (ea[i], 0, 0)
    wb_map = lambda i, dest, nv, ea, eb: (eb[i], 0, 0)
    const = lambda i, dest, nv, ea, eb: (0, 0)
    return pl.pallas_call(
        _moe_kernel,
        grid_spec=pltpu.PrefetchScalarGridSpec(
            num_scalar_prefetch=4,
            grid=(n_tiles,),
            in_specs=[
                pl.BlockSpec(memory_space=pl.ANY),
                pl.BlockSpec((1, D_MODEL, D_FF), wa_map),
                pl.BlockSpec((1, D_MODEL, D_FF), wa_map),
                pl.BlockSpec((1, D_FF, D_MODEL), wa_map),
                pl.BlockSpec((1, D_MODEL, D_FF), wb_map),
                pl.BlockSpec((1, D_MODEL, D_FF), wb_map),
                pl.BlockSpec((1, D_FF, D_MODEL), wb_map),
                pl.BlockSpec((1, D_MODEL), const),
                pl.BlockSpec((1, D_MODEL), const),
            ],
            out_specs=pl.BlockSpec(memory_space=pl.ANY),
            scratch_shapes=[
                pltpu.VMEM((2, tm, HEXT_COLS), F32),
                pltpu.VMEM((2, tm, D_MODEL), F32),
                pltpu.SMEM((n_tiles * tm,), jnp.int32),
                pltpu.SemaphoreType.DMA((2,)),
                pltpu.SemaphoreType.DMA((2,)),
            ],
        ),
        out_shape=jax.ShapeDtypeStruct((t, D_MODEL), F32),
        compiler_params=pltpu.CompilerParams(
            dimension_semantics=("arbitrary",), vmem_limit_bytes=VMEM_LIMIT),
        name="experts",
    )(dest, nvalid, ea, eb, hext, wg, wu, wd, wg, wu, wd, gffn, gfin)


def _routing_plan(cls, rank, counts, tm):
    t = cls.shape[0]
    n_tiles = t // tm + N_CLASSES
    tiles_c = (counts + tm - 1) // tm
    tile_end = jnp.cumsum(tiles_c)
    tile_off = tile_end - tiles_c
    onehot = cls[:, None] == jnp.arange(N_CLASSES, dtype=jnp.int32)[None, :]
    dest = jnp.sum(jnp.where(onehot, tile_off[None, :], 0), axis=1) * tm + rank
    tile = jnp.arange(n_tiles, dtype=jnp.int32)
    used = tile < tile_end[-1]
    tile_cls = jnp.minimum(jnp.sum((tile[:, None] >= tile_end[None, :]).astype(jnp.int32), axis=1),
                           N_CLASSES - 1)
    cls_onehot = tile_cls[:, None] == jnp.arange(N_CLASSES, dtype=jnp.int32)[None, :]
    pick = lambda table: jnp.sum(jnp.where(cls_onehot, table[None, :], 0), axis=1)
    nvalid = jnp.where(used, jnp.clip(pick(counts) - (tile - pick(tile_off)) * tm, 0, tm), 0)
    classes = np.arange(N_CLASSES)
    ea_tab = jnp.asarray((classes // PAIRS_PER_GROUP) * EXPERTS_PER_GROUP
                         + np.asarray(_PAIR_A)[classes % PAIRS_PER_GROUP], jnp.int32)
    eb_tab = jnp.asarray((classes // PAIRS_PER_GROUP) * EXPERTS_PER_GROUP
                         + np.asarray(_PAIR_B)[classes % PAIRS_PER_GROUP], jnp.int32)
    last_used = jnp.sum(jnp.where(tile == tile_end[-1] - 1, tile_cls, 0))
    tile_cls = jnp.where(used, tile_cls, last_used)
    cls_onehot = tile_cls[:, None] == jnp.arange(N_CLASSES, dtype=jnp.int32)[None, :]
    return dest.astype(jnp.int32), nvalid.astype(jnp.int32), pick(ea_tab), pick(eb_tab)


def _rope_tables(seq):
    half = RET_QK_DIM // 2
    inv_freq = ROPE_BASE ** (-jnp.arange(half, dtype=F32) / half)
    ang = jnp.arange(seq, dtype=F32)[:, None] * inv_freq[None, :]
    cos, sin = jnp.cos(ang), jnp.sin(ang)
    cos_t = jnp.tile(cos, (1, LANES // half))
    sin_t = jnp.concatenate([-sin, -sin, sin, sin], axis=1)
    return cos_t, sin_t


def kernel(x, norm_mix_g, w_in, w_gate, b_gate, w_sb_out, w_ret_out, ret_norm_g, w_out,
           norm_ffn_g, w_group_router, b_group_router, w_expert_router, b_expert_router,
           w_exp_gate, w_exp_up, w_exp_down, norm_final_g):
    bsz, seq, d = x.shape
    t = bsz * seq
    assert d == D_MODEL and w_in.shape[0] == 1 and t % MIX_TM == 0
    assert seq % (SB_GROUP * SB_BLK) == 0 and seq % (RET_BLK * RET_UNROLL) == 0
    x2 = x.reshape(t, d)
    wi = w_in[0]
    c_rq = 3 * SB_WIDTH
    c_rk = c_rq + RET_QK_WIDTH
    c_rv = c_rk + RET_QK_WIDTH
    c_rg = c_rv + RET_V_WIDTH
    ones = functools.partial(jnp.ones, dtype=F32)
    w_plain = jnp.concatenate([wi[:, :c_rq], wi[:, c_rv:c_rg]], axis=1).astype(BF16)
    s_plain = jnp.concatenate([jnp.full((SB_WIDTH,), SB_HEAD_DIM ** -0.5 * LOG2E, F32),
                               ones((2 * SB_WIDTH + RET_V_WIDTH,))])[None, :]
    perm = _rot_perm()
    w_rot = jnp.concatenate([wi[:, c_rq:c_rk][:, perm],
                             (RET_QK_DIM ** -0.5) * wi[:, c_rk:c_rv][:, perm]], axis=1).astype(BF16)
    w_swish_half = (0.5 * wi[:, c_rg:]).astype(BF16)
    w_gate_half = (0.5 * w_gate[0]).astype(BF16)
    b_gate_half = (0.5 * b_gate[0])[None, :]
    cos_t, sin_t = _rope_tables(seq)

    plain, xn = _proj_plain_call(x2, norm_mix_g[0][None, :], w_plain, s_plain, seq)
    rqk = _proj_rot_call(xn, w_rot, cos_t, sin_t, seq)
    swish = _proj_swish_call(xn, w_swish_half, seq)
    gates = _proj_gate_call(xn, w_gate_half, b_gate_half, seq)
    ysb = _sb_call(plain, seq)
    yret = _ret_call(rqk, plain, swish, ret_norm_g[0][None, :], seq)

    wr = jnp.concatenate([w_group_router[0], w_expert_router[0]], axis=1).T
    wr = jnp.pad(wr, ((0, ROUTER_ROWS - wr.shape[0]), (0, 0)))
    wr_hi = wr.astype(BF16)
    wr_lo = (wr - wr_hi.astype(F32)).astype(BF16)
    br = jnp.concatenate([b_group_router[0], b_expert_router[0]])
    br = jnp.pad(br, (0, ROUTER_ROWS - br.shape[0]))[:, None]
    hext, cls8, cnt = _mix_call(ysb, yret, gates, x2, w_sb_out[0].astype(BF16), w_ret_out[0].astype(BF16),
                           w_out[0].astype(BF16), norm_ffn_g[0][None, :], wr_hi, wr_lo, br)

    counts = cnt[:N_CLASSES, 0].astype(jnp.int32)
    dest, nvalid, ea, eb = _routing_plan(cls8[0], cls8[1], counts, MOE_TM)
    out = _moe_call(dest, nvalid, ea, eb, hext, (0.5 * w_exp_gate[0]).astype(BF16), w_exp_up[0].astype(BF16),
                    w_exp_down[0].astype(BF16), norm_ffn_g[0][None, :], norm_final_g[None, :])
    return out.reshape(bsz, seq, d)
```

```python
import functools
import math

import numpy as np
import jax
import jax.numpy as jnp
from jax import lax
from jax.experimental import pallas as pl
from jax.experimental.pallas import tpu as pltpu

F32 = jnp.float32
BF16 = jnp.bfloat16

D_MODEL = 1024
SB_HEADS = 8
SB_HEAD_DIM = 64
SB_WIDTH = SB_HEADS * SB_HEAD_DIM
RET_HEADS = 8
RET_QK_DIM = 64
RET_V_DIM = 128
RET_QK_WIDTH = RET_HEADS * RET_QK_DIM
RET_V_WIDTH = RET_HEADS * RET_V_DIM
IN_COLS = 3 * SB_WIDTH + 2 * RET_QK_WIDTH + 2 * RET_V_WIDTH
GATE_COLS = 2 * D_MODEL
ALL_COLS = IN_COLS + GATE_COLS
CHUNK = 64
ROPE_BASE = 10000.0
N_GROUPS = 4
EXPERTS_PER_GROUP = 4
N_EXPERTS = N_GROUPS * EXPERTS_PER_GROUP
D_FF = 512
EPS = 1e-6

LANES = 128
PROJ_TN = 1280
GATE_BLK = 512
PROJ_ROWS = 256
SB_BLK = 256
SB_GROUP = 8
RET_BLK = 256
RET_UNROLL = 8
MIX_TM = 512
MOE_TM = 256
MOE_INVERT_UNROLL = 32
ROUTER_ROWS = 32
PAIRS_PER_GROUP = 6
N_CLASSES = N_GROUPS * PAIRS_PER_GROUP
INFO_COLS = LANES
HEXT_COLS = D_MODEL + INFO_COLS
VMEM_LIMIT = 56 * 1024 * 1024
LOG2E = math.log2(math.e)
SB_SKIP_LOG2 = 156.0

_PAIR_A = (0, 0, 0, 1, 1, 2)
_PAIR_B = (1, 2, 3, 2, 3, 3)


def _proj_plain_kernel(x_ref, g_ref, w_ref, s_ref, o_ref, xn_ref):
    def body(normalise):
        for rows in _row_chunks(xn_ref.shape[0]):
            if normalise:
                x = x_ref[rows, :]
                ms = jnp.mean(x * x, axis=-1, keepdims=True)
                xn_ref[rows, :] = ((x * lax.rsqrt(ms + EPS)) * g_ref[...]).astype(BF16)
            acc = jnp.dot(xn_ref[rows, :], w_ref[...], preferred_element_type=F32)
            o_ref[rows, :] = (acc * s_ref[...]).astype(BF16)

    first = pl.program_id(1) == 0
    pl.when(first)(functools.partial(body, True))
    pl.when(jnp.logical_not(first))(functools.partial(body, False))


def _row_chunks(n_rows):
    return [slice(r, r + PROJ_ROWS) for r in range(0, n_rows, PROJ_ROWS)]


def _proj_rot_kernel(xn_ref, w_ref, cos_ref, sin_ref, o_ref):
    for rows in _row_chunks(xn_ref.shape[0]):
        acc = jnp.dot(xn_ref[rows, :], w_ref[...], preferred_element_type=F32)
        cos = cos_ref[rows, :]
        sin = sin_ref[rows, :]
        for p in range(w_ref.shape[1] // LANES):
            cols = slice(p * LANES, (p + 1) * LANES)
            seg = acc[:, cols]
            o_ref[rows, cols] = (seg * cos + pltpu.roll(seg, LANES // 2, 1) * sin).astype(BF16)


def _proj_swish_kernel(xn_ref, w_ref, o_ref):
    for rows in _row_chunks(xn_ref.shape[0]):
        half = jnp.dot(xn_ref[rows, :], w_ref[...], preferred_element_type=F32)
        o_ref[rows, :] = (half * (jnp.tanh(half) + 1.0)).astype(BF16)


def _proj_gate_kernel(xn_ref, w_ref, b_ref, o_ref):
    for rows in _row_chunks(xn_ref.shape[0]):
        half = jnp.dot(xn_ref[rows, :], w_ref[...], preferred_element_type=F32) + b_ref[...]
        o_ref[rows, :] = (0.5 * jnp.tanh(half) + 0.5).astype(BF16)


def _proj_params():
    return pltpu.CompilerParams(dimension_semantics=("arbitrary", "arbitrary"),
                                vmem_limit_bytes=VMEM_LIMIT)


def _proj_tile(n):
    tn = PROJ_TN
    while n % tn:
        tn -= LANES
    return tn


def _proj_plain_call(x2, g, w, scale, seq):
    t, n = x2.shape[0], w.shape[1]
    tn = _proj_tile(n)
    return pl.pallas_call(
        _proj_plain_kernel,
        grid=(t // seq, n // tn),
        in_specs=[
            pl.BlockSpec((seq, D_MODEL), lambda i, j: (i, 0)),
            pl.BlockSpec((1, D_MODEL), lambda i, j: (0, 0)),
            pl.BlockSpec((D_MODEL, tn), lambda i, j: (0, j)),
            pl.BlockSpec((1, tn), lambda i, j: (0, j)),
        ],
        out_specs=[
            pl.BlockSpec((seq, tn), lambda i, j: (i, j)),
            pl.BlockSpec((seq, D_MODEL), lambda i, j: (i, 0)),
        ],
        out_shape=[jax.ShapeDtypeStruct((t, n), BF16), jax.ShapeDtypeStruct((t, D_MODEL), BF16)],
        compiler_params=_proj_params(),
        name="proj_plain",
    )(x2, g, w, scale)


def _proj_rot_call(xn, w, cos_t, sin_t, seq):
    t, n = xn.shape[0], w.shape[1]
    tn = _proj_tile(n)
    return pl.pallas_call(
        _proj_rot_kernel,
        grid=(t // seq, n // tn),
        in_specs=[
            pl.BlockSpec((seq, D_MODEL), lambda i, j: (i, 0)),
            pl.BlockSpec((D_MODEL, tn), lambda i, j: (0, j)),
            pl.BlockSpec((seq, LANES), lambda i, j: (0, 0)),
            pl.BlockSpec((seq, LANES), lambda i, j: (0, 0)),
        ],
        out_specs=pl.BlockSpec((seq, tn), lambda i, j: (i, j)),
        out_shape=jax.ShapeDtypeStruct((t, n), BF16),
        compiler_params=_proj_params(),
        name="proj_rotary",
    )(xn, w, cos_t, sin_t)


def _proj_swish_call(xn, w_half, seq):
    t, n = xn.shape[0], w_half.shape[1]
    tn = _proj_tile(n)
    return pl.pallas_call(
        _proj_swish_kernel,
        grid=(t // seq, n // tn),
        in_specs=[
            pl.BlockSpec((seq, D_MODEL), lambda i, j: (i, 0)),
            pl.BlockSpec((D_MODEL, tn), lambda i, j: (0, j)),
        ],
        out_specs=pl.BlockSpec((seq, tn), lambda i, j: (i, j)),
        out_shape=jax.ShapeDtypeStruct((t, n), BF16),
        compiler_params=_proj_params(),
        name="proj_swish",
    )(xn, w_half)


def _proj_gate_call(xn, w_half, bias_half, seq):
    t, n = xn.shape[0], w_half.shape[1]
    tn = _proj_tile(n)
    return pl.pallas_call(
        _proj_gate_kernel,
        grid=(t // seq, n // tn),
        in_specs=[
            pl.BlockSpec((seq, D_MODEL), lambda i, j: (i, 0)),
            pl.BlockSpec((D_MODEL, tn), lambda i, j: (0, j)),
            pl.BlockSpec((1, tn), lambda i, j: (0, j)),
        ],
        out_specs=pl.BlockSpec((seq, tn), lambda i, j: (i, j)),
        out_shape=jax.ShapeDtypeStruct((t, n), BF16),
        compiler_params=_proj_params(),
        name="proj_gate",
    )(xn, w_half, bias_half)


def _rot_perm():
    half = RET_QK_DIM // 2
    order = [p * LANES + hh * RET_QK_DIM + part * half + d
             for p in range(RET_QK_WIDTH // LANES) for part in range(2) for hh in range(2)
             for d in range(half)]
    return np.asarray(order, np.int32)


def _sb_kernel(q_ref, k_ref, v_ref, o_ref, u_ref, acc_ref, car_ref):
    seq = q_ref.shape[0]
    nq = seq // SB_BLK
    row = lax.broadcasted_iota(jnp.int32, (SB_BLK, SB_BLK), 0)
    col = lax.broadcasted_iota(jnp.int32, (SB_BLK, SB_BLK), 1)
    u_ref[...] = (row >= col).astype(BF16)
    lane = lax.broadcasted_iota(jnp.int32, (1, LANES), 1)
    head_masks = (lane < SB_HEAD_DIM, lane >= SB_HEAD_DIM)

    heads = range(2)
    causal = col < row

    def rows_of(blk):
        return pl.ds(pl.multiple_of(blk * SB_BLK, SB_BLK), SB_BLK)

    def q_heads_of(qi):
        q_blk = q_ref[rows_of(qi), :]
        return tuple(jnp.where(m, q_blk, jnp.zeros_like(q_blk)) for m in head_masks)

    def sweep(jobs):
        u = u_ref[...]
        q_heads = [q_heads_of(qi) for _, qi, _, _ in jobs]
        z2 = [[[lax.dot_general(q_heads[j][h], k_ref[rows_of(kb), :], (((1,), (1,)), ((), ())),
                                preferred_element_type=F32) for h in heads]
               for kb, _ in steps] for j, (_, _, steps, _) in enumerate(jobs)]
        sinc = []
        for j, (_, _, steps, _) in enumerate(jobs):
            sinc.append([])
            for i, (_, diagonal) in enumerate(steps):
                nlk = [jnp.maximum(z, 0.0) + jnp.log2(1.0 + jnp.exp2(-jnp.abs(z))) for z in z2[j][i]]
                if diagonal:
                    nlk = [jnp.where(causal, a, 0.0) for a in nlk]
                sinc[j].append([jnp.dot(a.astype(BF16), u, preferred_element_type=F32) for a in nlk])
        for j, (slot, _, steps, fresh) in enumerate(jobs):
            car = [None, None] if fresh else [car_ref[slot, h] for h in heads]
            contrib = None
            for i, (kb, diagonal) in enumerate(steps):
                v_blk = v_ref[rows_of(kb), :]
                for h in heads:
                    e = z2[j][i][h] - sinc[j][i][h]
                    if car[h] is not None:
                        e = e - jnp.concatenate([car[h], car[h]], axis=1)
                    w = jnp.exp2(e)
                    if diagonal:
                        w = jnp.where(causal, w, 0.0)
                    vh = jnp.where(head_masks[h], v_blk, jnp.zeros_like(v_blk))
                    pv = jnp.dot(w.astype(BF16), vh, preferred_element_type=F32)
                    contrib = pv if contrib is None else contrib + pv
                    tot = jnp.broadcast_to(sinc[j][i][h][:, 0:1], (SB_BLK, LANES))
                    car[h] = tot if car[h] is None else car[h] + tot
            for h in heads:
                car_ref[slot, h] = car[h]
            if fresh:
                acc_ref[slot] = contrib
            else:
                acc_ref[slot] += contrib

    def finish(jobs):
        pending = [job for job in jobs if job[2] is not None]
        if pending:
            lo, hi = pending[0][0], pending[-1][0] + 1

            @pl.when(jnp.min(car_ref[lo:hi]) < SB_SKIP_LOG2)
            def _():
                for slot, qi, first_kb in pending:
                    def more(c):
                        kb, min_carry = c
                        return (kb >= 0) & (min_carry < SB_SKIP_LOG2)

                    def k_step(c, slot=slot, qi=qi):
                        kb, _ = c
                        sweep([(slot, qi, [(kb, False)], False)])
                        return kb - 1, jnp.min(car_ref[slot])

                    lax.while_loop(more, k_step, (first_kb, jnp.min(car_ref[slot])))
        for slot, qi, _ in jobs:
            o_ref[rows_of(qi), :] = acc_ref[slot].astype(BF16)

    def first_steps(qi):
        return [(qi, True)] + ([(qi - 1, False)] if qi > 0 else [])

    sweep([(s, s, first_steps(s), True) for s in range(SB_GROUP)])
    finish([(s, s, s - 2 if s >= 2 else None) for s in range(SB_GROUP)])

    def q_group(g, carry):
        q0 = g * SB_GROUP
        sweep([(s, q0 + s, [(q0 + s, True), (q0 + s - 1, False)], True) for s in range(SB_GROUP)])
        finish([(s, q0 + s, q0 + s - 2) for s in range(SB_GROUP)])
        return carry

    lax.fori_loop(1, nq // SB_GROUP, q_group, 0)


def _sb_call(proj, seq):
    t = proj.shape[0]
    assert seq % (SB_GROUP * SB_BLK) == 0
    n_pairs = SB_WIDTH // LANES
    return pl.pallas_call(
        _sb_kernel,
        grid=(t // seq, n_pairs),
        in_specs=[
            pl.BlockSpec((seq, LANES), lambda b, p: (b, p)),
            pl.BlockSpec((seq, LANES), lambda b, p: (b, n_pairs + p)),
            pl.BlockSpec((seq, LANES), lambda b, p: (b, 2 * n_pairs + p)),
        ],
        out_specs=pl.BlockSpec((seq, LANES), lambda b, p: (b, p)),
        out_shape=jax.ShapeDtypeStruct((t, SB_WIDTH), BF16),
        scratch_shapes=[
            pltpu.VMEM((SB_BLK, SB_BLK), BF16),
            pltpu.VMEM((SB_GROUP, SB_BLK, LANES), F32),
            pltpu.VMEM((SB_GROUP, 2, SB_BLK, LANES), F32),
        ],
        compiler_params=pltpu.CompilerParams(
            dimension_semantics=("arbitrary", "arbitrary"), vmem_limit_bytes=VMEM_LIMIT),
        name="stickbreak",
    )(proj, proj, proj)


def _ret_tables():
    h = np.arange(RET_HEADS, dtype=np.float64)
    log_gamma = np.log(1.0 - 2.0 ** (-5.0 - h))
    idx = np.arange(RET_BLK, dtype=np.float64)
    t, s = idx[:, None], idx[None, :]
    same = (t // CHUNK) == (s // CHUNK)
    earlier = (s // CHUNK) < (t // CHUNK)
    expo = np.where(same, np.abs(t - s), np.where(earlier, t - s, 0.0))
    dmat = np.exp(log_gamma[:, None, None] * expo) * (same | earlier)
    qdec = np.exp(log_gamma[:, None] * (idx + 1.0)[None, :])
    kdec = np.exp(log_gamma[:, None] * (RET_BLK - 1.0 - idx)[None, :])
    cdec = np.exp(log_gamma * RET_BLK)
    n_pairs = RET_HEADS // 2
    lane_head = (np.arange(LANES) % RET_QK_DIM) // (RET_QK_DIM // 2)
    col_head = np.arange(2 * RET_V_DIM) // RET_V_DIM
    pair_heads = np.arange(RET_HEADS).reshape(n_pairs, 2)
    qdec_pair = np.stack([qdec[pair_heads[p][lane_head]].T for p in range(n_pairs)])
    kdec_pair = np.stack([kdec[pair_heads[p][lane_head]].T for p in range(n_pairs)])
    cdec_pair = np.stack([cdec[pair_heads[p][col_head]][None, :] for p in range(n_pairs)])
    return (jnp.asarray(dmat.reshape((n_pairs, 2, RET_BLK, RET_BLK)), F32),
            jnp.asarray(qdec_pair, F32), jnp.asarray(kdec_pair, F32), jnp.asarray(cdec_pair, F32))


def _ret_kernel(q_ref, k_ref, v_ref, g_ref, gn_ref, dm_ref, qd_ref, kd_ref, cd_ref, o_ref, st_ref):
    seq = q_ref.shape[0]
    lane = lax.broadcasted_iota(jnp.int32, (1, LANES), 1)
    head0 = (lane % RET_QK_DIM) < (RET_QK_DIM // 2)
    head_masks = (head0, jnp.logical_not(head0))
    st_row = lax.broadcasted_iota(jnp.int32, st_ref.shape, 0)
    st_col = lax.broadcasted_iota(jnp.int32, st_ref.shape, 1)
    own_head = ((st_row % RET_QK_DIM) // (RET_QK_DIM // 2)) == (st_col // RET_V_DIM)
    heads = range(2)
    nt = (((1,), (1,)), ((), ()))
    tn = (((0,), (0,)), ((), ()))
    st_ref[...] = jnp.zeros_like(st_ref)

    def step(n, carry):
        subs = range(RET_UNROLL)
        rows = [pl.ds(pl.multiple_of((n * RET_UNROLL + u) * RET_BLK, RET_BLK), RET_BLK) for u in subs]
        col = [slice(h * RET_V_DIM, (h + 1) * RET_V_DIM) for h in heads]
        q_blk = [q_ref[r, :] for r in rows]
        k_blk = [k_ref[r, :] for r in rows]
        qm = [[jnp.where(head_masks[h], q_blk[u], jnp.zeros_like(q_blk[u])) for h in heads] for u in subs]
        v = [v_ref[r, :] for r in rows]
        scores = [[lax.dot_general(qm[u][h], k_blk[u], nt, preferred_element_type=F32)
                   for h in heads] for u in subs]
        kd = [(k_blk[u].astype(F32) * kd_ref[0]).astype(BF16) for u in subs]
        kv = [jnp.where(own_head, lax.dot_general(kd[u], v[u], tn, preferred_element_type=F32), 0.0)
              for u in subs]
        qd = [(q_blk[u].astype(F32) * qd_ref[0]).astype(BF16) for u in subs]
        p = [[(scores[u][h] * dm_ref[0, h]).astype(BF16) for h in heads] for u in subs]
        intra = [[jnp.dot(p[u][h], v[u][:, col[h]], preferred_element_type=F32) for h in heads]
                 for u in subs]
        state = st_ref[...]
        for u in subs:
            cross = jnp.dot(qd[u], state.astype(BF16), preferred_element_type=F32)
            state = state * cd_ref[0] + kv[u]
            for h in heads:
                y = intra[u][h] + cross[:, col[h]]
                ms = jnp.mean(y * y, axis=-1, keepdims=True)
                yn = (y * lax.rsqrt(ms + EPS)) * gn_ref[:, col[h]]
                o_ref[rows[u], col[h]] = (g_ref[rows[u], col[h]].astype(F32) * yn).astype(BF16)
        st_ref[...] = state
        return carry

    lax.fori_loop(0, seq // (RET_BLK * RET_UNROLL), step, 0)


def _ret_call(rqk, plain, swish, ret_norm_g, seq):
    t = rqk.shape[0]
    n_pairs = RET_HEADS // 2
    dmat, qdec, kdec, cdec = _ret_tables()
    k0 = RET_QK_WIDTH // LANES
    v0 = (3 * SB_WIDTH) // (2 * RET_V_DIM)
    return pl.pallas_call(
        _ret_kernel,
        grid=(t // seq, n_pairs),
        in_specs=[
            pl.BlockSpec((seq, LANES), lambda b, p: (b, p)),
            pl.BlockSpec((seq, LANES), lambda b, p: (b, k0 + p)),
            pl.BlockSpec((seq, 2 * RET_V_DIM), lambda b, p: (b, v0 + p)),
            pl.BlockSpec((seq, 2 * RET_V_DIM), lambda b, p: (b, p)),
            pl.BlockSpec((1, 2 * RET_V_DIM), lambda b, p: (0, p)),
            pl.BlockSpec((1, 2, RET_BLK, RET_BLK), lambda b, p: (p, 0, 0, 0)),
            pl.BlockSpec((1, RET_BLK, LANES), lambda b, p: (p, 0, 0)),
            pl.BlockSpec((1, RET_BLK, LANES), lambda b, p: (p, 0, 0)),
            pl.BlockSpec((1, 1, 2 * RET_V_DIM), lambda b, p: (p, 0, 0)),
        ],
        out_specs=pl.BlockSpec((seq, 2 * RET_V_DIM), lambda b, p: (b, p)),
        out_shape=jax.ShapeDtypeStruct((t, RET_V_WIDTH), BF16),
        scratch_shapes=[pltpu.VMEM((LANES, 2 * RET_V_DIM), F32)],
        compiler_params=pltpu.CompilerParams(
            dimension_semantics=("arbitrary", "arbitrary"), vmem_limit_bytes=VMEM_LIMIT),
        name="retention",
    )(rqk, rqk, plain, swish, ret_norm_g, dmat, qdec, kdec, cdec)


def _mix_kernel(ysb_ref, yret_ref, gs0_ref, gs1_ref, gr0_ref, gr1_ref, x_ref,
                wsb_ref, wret_ref, wout_ref, gffn_ref, wrh_ref, wrl_ref, br_ref, tri_ref,
                hext_ref, cls_ref, cnt_ref):
    tm = x_ref.shape[0]

    @pl.when(pl.program_id(0) == 0)
    def _():
        cnt_ref[...] = jnp.zeros_like(cnt_ref)

    a = jnp.dot(ysb_ref[...], wsb_ref[...], preferred_element_type=F32)
    b = jnp.dot(yret_ref[...], wret_ref[...], preferred_element_type=F32)
    g_sb = jnp.concatenate([gs0_ref[...], gs1_ref[...]], axis=1).astype(F32)
    g_ret = jnp.concatenate([gr0_ref[...], gr1_ref[...]], axis=1).astype(F32)
    mixed = (g_sb * a + g_ret * b).astype(BF16)
    h = x_ref[...] + jnp.dot(mixed, wout_ref[...], preferred_element_type=F32)
    hext_ref[:, 0:D_MODEL] = h

    ms = jnp.mean(h * h, axis=-1, keepdims=True)
    hn = (h * lax.rsqrt(ms + EPS)) * gffn_ref[...]
    hn_hi = hn.astype(BF16)
    hn_lo = (hn - hn_hi.astype(F32)).astype(BF16)
    nt = (((1,), (1,)), ((), ()))
    w_both = jnp.concatenate([wrh_ref[...], wrl_ref[...]], axis=0)
    by_hi = lax.dot_general(w_both, hn_hi, nt, preferred_element_type=F32)
    logits = (by_hi[0:ROUTER_ROWS] + by_hi[ROUTER_ROWS:2 * ROUTER_ROWS]
              + lax.dot_general(wrh_ref[...], hn_lo, nt, preferred_element_type=F32)
              + br_ref[...])

    def first_argmax(vals):
        m = functools.reduce(jnp.maximum, vals)
        idx = jnp.full(m.shape, len(vals) - 1, jnp.int32)
        for i in range(len(vals) - 2, -1, -1):
            idx = jnp.where(vals[i] >= m, i, idx)
        return m, idx

    gl = [logits[r:r + 1, :] for r in range(N_GROUPS)]
    gmax, gsel = first_argmax(gl)
    p_group = 1.0 / functools.reduce(lambda s, v: s + v, [jnp.exp(v - gmax) for v in gl])
    el = []
    for e in range(EXPERTS_PER_GROUP):
        v = logits[N_GROUPS + 3 * EXPERTS_PER_GROUP + e:N_GROUPS + 3 * EXPERTS_PER_GROUP + e + 1, :]
        for g in range(N_GROUPS - 2, -1, -1):
            r = N_GROUPS + g * EXPERTS_PER_GROUP + e
            v = jnp.where(gsel == g, logits[r:r + 1, :], v)
        el.append(v)
    m1, i1 = first_argmax(el)
    rest = [jnp.where(i1 == e, -jnp.inf, el[e]) for e in range(EXPERTS_PER_GROUP)]
    m2, i2 = first_argmax(rest)
    tt = jnp.exp(m2 - m1)
    gate1 = p_group * (1.0 / (1.0 + tt))
    gate2 = p_group * (tt / (1.0 + tt))
    lo_first = i1 < i2
    ea = jnp.minimum(i1, i2)
    eb = jnp.maximum(i1, i2)
    w_a = jnp.where(lo_first, gate1, gate2)
    w_b = jnp.where(lo_first, gate2, gate1)
    pid = jnp.where(ea == 0, eb - 1, jnp.where(ea == 1, eb + 1, PAIRS_PER_GROUP - 1))
    cls = gsel * PAIRS_PER_GROUP + pid
    info = jnp.concatenate([w_a, w_b, jnp.zeros((INFO_COLS - 2, tm), F32)], axis=0)
    hext_ref[:, D_MODEL:HEXT_COLS] = info.T

    class_row = lax.broadcasted_iota(jnp.int32, (ROUTER_ROWS, tm), 0)
    onehot = class_row == cls
    onehot_bf = onehot.astype(BF16)
    before = jnp.dot(onehot_bf, tri_ref[...], preferred_element_type=F32)
    cnt = cnt_ref[...]
    seen = before + jnp.concatenate([cnt] * (tm // LANES), axis=1)
    rank = jnp.sum(jnp.where(onehot, seen, 0.0), axis=0, keepdims=True)
    cnt_ref[...] = cnt + jnp.dot(onehot_bf, jnp.ones((tm, LANES), BF16), preferred_element_type=F32)
    cls_ref[...] = jnp.concatenate(
        [cls, rank.astype(jnp.int32), jnp.zeros((cls_ref.shape[0] - 2, tm), jnp.int32)], axis=0)


def _mix_call(ysb, yret, gates, x2, wsb, wret, wout, gffn, wr_hi, wr_lo, br):
    t = x2.shape[0]
    tm = MIX_TM
    gate0 = 0
    const = lambda i: (0, 0)
    idx = jnp.arange(tm, dtype=jnp.int32)
    tri = (idx[:, None] < idx[None, :]).astype(BF16)
    return pl.pallas_call(
        _mix_kernel,
        grid=(t // tm,),
        in_specs=[
            pl.BlockSpec((tm, SB_WIDTH), lambda i: (i, 0)),
            pl.BlockSpec((tm, RET_V_WIDTH), lambda i: (i, 0)),
            pl.BlockSpec((tm, GATE_BLK), lambda i: (i, gate0)),
            pl.BlockSpec((tm, GATE_BLK), lambda i: (i, gate0 + 1)),
            pl.BlockSpec((tm, GATE_BLK), lambda i: (i, gate0 + 2)),
            pl.BlockSpec((tm, GATE_BLK), lambda i: (i, gate0 + 3)),
            pl.BlockSpec((tm, D_MODEL), lambda i: (i, 0)),
            pl.BlockSpec((SB_WIDTH, D_MODEL), const),
            pl.BlockSpec((RET_V_WIDTH, D_MODEL), const),
            pl.BlockSpec((D_MODEL, D_MODEL), const),
            pl.BlockSpec((1, D_MODEL), const),
            pl.BlockSpec((ROUTER_ROWS, D_MODEL), const),
            pl.BlockSpec((ROUTER_ROWS, D_MODEL), const),
            pl.BlockSpec((ROUTER_ROWS, 1), const),
            pl.BlockSpec((tm, tm), const),
        ],
        out_specs=[
            pl.BlockSpec((tm, HEXT_COLS), lambda i: (i, 0)),
            pl.BlockSpec((8, tm), lambda i: (0, i)),
            pl.BlockSpec((ROUTER_ROWS, LANES), const),
        ],
        out_shape=[
            jax.ShapeDtypeStruct((t, HEXT_COLS), F32),
            jax.ShapeDtypeStruct((8, t), jnp.int32),
            jax.ShapeDtypeStruct((ROUTER_ROWS, LANES), F32),
        ],
        compiler_params=pltpu.CompilerParams(
            dimension_semantics=("arbitrary",), vmem_limit_bytes=VMEM_LIMIT),
        name="mix_router",
    )(ysb, yret, gates, gates, gates, gates, x2, wsb, wret, wout, gffn, wr_hi, wr_lo, br, tri)


def _moe_kernel(dest_ref, nvalid_ref, ea_ref, eb_ref,
                hext_ref, wga_ref, wua_ref, wda_ref, wgb_ref, wub_ref, wdb_ref,
                gffn_ref, gfin_ref, out_ref, hbuf, obuf, src, gsem, ssem):
    i = pl.program_id(0)
    n_tiles = pl.num_programs(0)
    tm = hbuf.shape[1]
    nv = nvalid_ref[i]
    nxt = jnp.minimum(i + 1, n_tiles - 1)
    nv_next = jnp.where(i + 1 < n_tiles, nvalid_ref[nxt], 0)
    has_next = nv_next > 0

    def gather_copy(tile, buf_slot, r):
        tok = src[tile * tm + r]
        return pltpu.make_async_copy(hext_ref.at[pl.ds(tok, 1), :],
                                     hbuf.at[buf_slot, pl.ds(r, 1), :], gsem.at[buf_slot])

    def scatter_copy(tile, buf_slot, r):
        tok = src[tile * tm + r]
        return pltpu.make_async_copy(obuf.at[buf_slot, pl.ds(r, 1), :],
                                     out_ref.at[pl.ds(tok, 1), :], ssem.at[buf_slot])

    def start_rows(copy_of_row, rows_valid):
        for r in range(tm):
            if rows_valid is None:
                copy_of_row(r).start(priority=r % 2)
            else:
                @pl.when(r < rows_valid)
                def _():
                    copy_of_row(r).start(priority=r % 2)

    def wait_rows(full_copy, row_copy, rows_valid):
        @pl.when(rows_valid == tm)
        def _():
            full_copy.wait()

        @pl.when(rows_valid < tm)
        def _():
            for r in range(tm):
                @pl.when(r < rows_valid)
                def _():
                    row_copy(r).wait()

    def wait_gather(buf_slot, rows_valid):
        wait_rows(pltpu.make_async_copy(hext_ref.at[pl.ds(0, tm), :], hbuf.at[buf_slot], gsem.at[buf_slot]),
                  lambda r: pltpu.make_async_copy(hext_ref.at[pl.ds(0, 1), :],
                                                  hbuf.at[buf_slot, pl.ds(r, 1), :], gsem.at[buf_slot]),
                  rows_valid)

    def wait_scatter(buf_slot, rows_valid):
        wait_rows(pltpu.make_async_copy(obuf.at[buf_slot], out_ref.at[pl.ds(0, tm), :], ssem.at[buf_slot]),
                  lambda r: pltpu.make_async_copy(obuf.at[buf_slot, pl.ds(r, 1), :],
                                                  out_ref.at[pl.ds(0, 1), :], ssem.at[buf_slot]),
                  rows_valid)

    @pl.when(i == 0)
    def _():
        def invert(c, carry):
            for u in range(MOE_INVERT_UNROLL):
                t = c * MOE_INVERT_UNROLL + u
                src[dest_ref[t]] = t
            return carry

        lax.fori_loop(0, dest_ref.shape[0] // MOE_INVERT_UNROLL, invert, 0)
        hbuf[...] = jnp.zeros_like(hbuf)
        start_rows(functools.partial(gather_copy, 0, 0), nv)

    def tile_body(slot):
        @pl.when(i >= 2)
        def _():
            wait_scatter(slot, nvalid_ref[jnp.maximum(i - 2, 0)])

        wait_gather(slot, nv)
        both_full = (nv == tm) & (nv_next == tm)
        pl.when(both_full)(functools.partial(tile_main, slot, None, None))
        pl.when(jnp.logical_not(both_full))(functools.partial(tile_main, slot, nv, nv_next))

        @pl.when(jnp.logical_not(has_next))
        def _():
            @pl.when(i >= 1)
            def _():
                wait_scatter(1 - slot, nvalid_ref[jnp.maximum(i - 1, 0)])

            wait_scatter(slot, nv)

    def tile_main(slot, rows_now, rows_next):
        start_rows(functools.partial(gather_copy, nxt, 1 - slot), rows_next)
        hrows = hbuf[slot]
        h = hrows[:, 0:D_MODEL]
        w_a = hrows[:, D_MODEL:D_MODEL + 1]
        w_b = hrows[:, D_MODEL + 1:D_MODEL + 2]
        ms = jnp.mean(h * h, axis=-1, keepdims=True)
        hn = ((h * lax.rsqrt(ms + EPS)) * gffn_ref[...]).astype(BF16)

        def expert(wg_ref, wu_ref, wd_ref):
            half = jnp.dot(hn, wg_ref[0], preferred_element_type=F32)
            up = jnp.dot(hn, wu_ref[0], preferred_element_type=F32)
            hidden = ((half * (jnp.tanh(half) + 1.0)) * up).astype(BF16)
            return jnp.dot(hidden, wd_ref[0], preferred_element_type=F32)

        y = w_a * expert(wga_ref, wua_ref, wda_ref) + w_b * expert(wgb_ref, wub_ref, wdb_ref)
        h2 = h + y
        ms2 = jnp.mean(h2 * h2, axis=-1, keepdims=True)
        obuf[slot] = (h2 * lax.rsqrt(ms2 + EPS)) * gfin_ref[...]
        start_rows(functools.partial(scatter_copy, i, slot), rows_now)

    for parity in range(2):
        pl.when((nv > 0) & (i % 2 == parity))(functools.partial(tile_body, parity))


def _moe_call(dest, nvalid, ea, eb, hext, wg, wu, wd, gffn, gfin):
    t = hext.shape[0]
    tm = MOE_TM
    n_tiles = nvalid.shape[0]
    assert t % MOE_INVERT_UNROLL == 0
    wa_map = lambda i, dest, nv, ea, eb: (ea[i], 0, 0)
    wb_map = lambda i, dest, nv, ea, eb: (eb[i], 0, 0)
    const = lambda i, dest, nv, ea, eb: (0, 0)
    return pl.pallas_call(
        _moe_kernel,
        grid_spec=pltpu.PrefetchScalarGridSpec(
            num_scalar_prefetch=4,
            grid=(n_tiles,),
            in_specs=[
                pl.BlockSpec(memory_space=pl.ANY),
                pl.BlockSpec((1, D_MODEL, D_FF), wa_map),
                pl.BlockSpec((1, D_MODEL, D_FF), wa_map),
                pl.BlockSpec((1, D_FF, D_MODEL), wa_map),
                pl.BlockSpec((1, D_MODEL, D_FF), wb_map),
                pl.BlockSpec((1, D_MODEL, D_FF), wb_map),
                pl.BlockSpec((1, D_FF, D_MODEL), wb_map),
                pl.BlockSpec((1, D_MODEL), const),
                pl.BlockSpec((1, D_MODEL), const),
            ],
            out_specs=pl.BlockSpec(memory_space=pl.ANY),
            scratch_shapes=[
                pltpu.VMEM((2, tm, HEXT_COLS), F32),
                pltpu.VMEM((2, tm, D_MODEL), F32),
                pltpu.SMEM((n_tiles * tm,), jnp.int32),
                pltpu.SemaphoreType.DMA((2,)),
                pltpu.SemaphoreType.DMA((2,)),
            ],
        ),
        out_shape=jax.ShapeDtypeStruct((t, D_MODEL), F32),
        compiler_params=pltpu.CompilerParams(
            dimension_semantics=("arbitrary",), vmem_limit_bytes=VMEM_LIMIT),
        name="experts",
    )(dest, nvalid, ea, eb, hext, wg, wu, wd, wg, wu, wd, gffn, gfin)


def _routing_plan(cls, rank, counts, tm):
    t = cls.shape[0]
    n_tiles = t // tm + N_CLASSES
    tiles_c = (counts + tm - 1) // tm
    tile_end = jnp.cumsum(tiles_c)
    tile_off = tile_end - tiles_c
    onehot = cls[:, None] == jnp.arange(N_CLASSES, dtype=jnp.int32)[None, :]
    dest = jnp.sum(jnp.where(onehot, tile_off[None, :], 0), axis=1) * tm + rank
    tile = jnp.arange(n_tiles, dtype=jnp.int32)
    used = tile < tile_end[-1]
    tile_cls = jnp.minimum(jnp.sum((tile[:, None] >= tile_end[None, :]).astype(jnp.int32), axis=1),
                           N_CLASSES - 1)
    cls_onehot = tile_cls[:, None] == jnp.arange(N_CLASSES, dtype=jnp.int32)[None, :]
    pick = lambda table: jnp.sum(jnp.where(cls_onehot, table[None, :], 0), axis=1)
    nvalid = jnp.where(used, jnp.clip(pick(counts) - (tile - pick(tile_off)) * tm, 0, tm), 0)
    classes = np.arange(N_CLASSES)
    ea_tab = jnp.asarray((classes // PAIRS_PER_GROUP) * EXPERTS_PER_GROUP
                         + np.asarray(_PAIR_A)[classes % PAIRS_PER_GROUP], jnp.int32)
    eb_tab = jnp.asarray((classes // PAIRS_PER_GROUP) * EXPERTS_PER_GROUP
                         + np.asarray(_PAIR_B)[classes % PAIRS_PER_GROUP], jnp.int32)
    last_used = jnp.sum(jnp.where(tile == tile_end[-1] - 1, tile_cls, 0))
    tile_cls = jnp.where(used, tile_cls, last_used)
    cls_onehot = tile_cls[:, None] == jnp.arange(N_CLASSES, dtype=jnp.int32)[None, :]
    return dest.astype(jnp.int32), nvalid.astype(jnp.int32), pick(ea_tab), pick(eb_tab)


def _rope_tables(seq):
    half = RET_QK_DIM // 2
    inv_freq = ROPE_BASE ** (-jnp.arange(half, dtype=F32) / half)
    ang = jnp.arange(seq, dtype=F32)[:, None] * inv_freq[None, :]
    cos, sin = jnp.cos(ang), jnp.sin(ang)
    cos_t = jnp.tile(cos, (1, LANES // half))
    sin_t = jnp.concatenate([-sin, -sin, sin, sin], axis=1)
    return cos_t, sin_t


def kernel(x, norm_mix_g, w_in, w_gate, b_gate, w_sb_out, w_ret_out, ret_norm_g, w_out,
           norm_ffn_g, w_group_router, b_group_router, w_expert_router, b_expert_router,
           w_exp_gate, w_exp_up, w_exp_down, norm_final_g):
    bsz, seq, d = x.shape
    t = bsz * seq
    assert d == D_MODEL and w_in.shape[0] == 1 and t % MIX_TM == 0
    assert seq % (SB_GROUP * SB_BLK) == 0 and seq % (RET_BLK * RET_UNROLL) == 0
    x2 = x.reshape(t, d)
    wi = w_in[0]
    c_rq = 3 * SB_WIDTH
    c_rk = c_rq + RET_QK_WIDTH
    c_rv = c_rk + RET_QK_WIDTH
    c_rg = c_rv + RET_V_WIDTH
    ones = functools.partial(jnp.ones, dtype=F32)
    w_plain = jnp.concatenate([wi[:, :c_rq], wi[:, c_rv:c_rg]], axis=1).astype(BF16)
    s_plain = jnp.concatenate([jnp.full((SB_WIDTH,), SB_HEAD_DIM ** -0.5 * LOG2E, F32),
                               ones((2 * SB_WIDTH + RET_V_WIDTH,))])[None, :]
    perm = _rot_perm()
    w_rot = jnp.concatenate([wi[:, c_rq:c_rk][:, perm],
                             (RET_QK_DIM ** -0.5) * wi[:, c_rk:c_rv][:, perm]], axis=1).astype(BF16)
    w_swish_half = (0.5 * wi[:, c_rg:]).astype(BF16)
    w_gate_half = (0.5 * w_gate[0]).astype(BF16)
    b_gate_half = (0.5 * b_gate[0])[None, :]
    cos_t, sin_t = _rope_tables(seq)

    plain, xn = _proj_plain_call(x2, norm_mix_g[0][None, :], w_plain, s_plain, seq)
    rqk = _proj_rot_call(xn, w_rot, cos_t, sin_t, seq)
    swish = _proj_swish_call(xn, w_swish_half, seq)
    gates = _proj_gate_call(xn, w_gate_half, b_gate_half, seq)
    ysb = _sb_call(plain, seq)
    yret = _ret_call(rqk, plain, swish, ret_norm_g[0][None, :], seq)

    wr = jnp.concatenate([w_group_router[0], w_expert_router[0]], axis=1).T
    wr = jnp.pad(wr, ((0, ROUTER_ROWS - wr.shape[0]), (0, 0)))
    wr_hi = wr.astype(BF16)
    wr_lo = (wr - wr_hi.astype(F32)).astype(BF16)
    br = jnp.concatenate([b_group_router[0], b_expert_router[0]])
    br = jnp.pad(br, (0, ROUTER_ROWS - br.shape[0]))[:, None]
    hext, cls8, cnt = _mix_call(ysb, yret, gates, x2, w_sb_out[0].astype(BF16), w_ret_out[0].astype(BF16),
                           w_out[0].astype(BF16), norm_ffn_g[0][None, :], wr_hi, wr_lo, br)

    counts = cnt[:N_CLASSES, 0].astype(jnp.int32)
    dest, nvalid, ea, eb = _routing_plan(cls8[0], cls8[1], counts, MOE_TM)
    out = _moe_call(dest, nvalid, ea, eb, hext, (0.5 * w_exp_gate[0]).astype(BF16), w_exp_up[0].astype(BF16),
                    w_exp_down[0].astype(BF16), norm_ffn_g[0][None, :], norm_final_g[None, :])
    return out.reshape(bsz, seq, d)
```

```python
import functools
import math

import numpy as np
import jax
import jax.numpy as jnp
from jax import lax
from jax.experimental import pallas as pl
from jax.experimental.pallas import tpu as pltpu

F32 = jnp.float32
BF16 = jnp.bfloat16

D_MODEL = 1024
SB_HEADS = 8
SB_HEAD_DIM = 64
SB_WIDTH = SB_HEADS * SB_HEAD_DIM
RET_HEADS = 8
RET_QK_DIM = 64
RET_V_DIM = 128
RET_QK_WIDTH = RET_HEADS * RET_QK_DIM
RET_V_WIDTH = RET_HEADS * RET_V_DIM
CHUNK = 64
ROPE_BASE = 10000.0
N_GROUPS = 4
EXPERTS_PER_GROUP = 4
D_FF = 512
EPS = 1e-6

LANES = 128
PROJ_TN = 1280
GATE_BLK = 512
PROJ_ROWS = 256
SB_BLK = 256
SB_GROUP = 8
RET_BLK = 256
RET_UNROLL = 8
MIX_TM = 512
MOE_TM = 256
MOE_INVERT_UNROLL = 32
ROUTER_ROWS = 32
PAIRS_PER_GROUP = 6
N_CLASSES = N_GROUPS * PAIRS_PER_GROUP
INFO_COLS = LANES
HEXT_COLS = D_MODEL + INFO_COLS
VMEM_LIMIT = 56 * 1024 * 1024
LOG2E = math.log2(math.e)
SB_SKIP_LOG2 = 156.0

_PAIR_A = (0, 0, 0, 1, 1, 2)
_PAIR_B = (1, 2, 3, 2, 3, 3)


def _proj_plain_kernel(x_ref, g_ref, w_ref, s_ref, o_ref, xn_ref):
    def body(normalise):
        for rows in _row_chunks(xn_ref.shape[0]):
            if normalise:
                x = x_ref[rows, :]
                ms = jnp.mean(x * x, axis=-1, keepdims=True)
                xn_ref[rows, :] = ((x * lax.rsqrt(ms + EPS)) * g_ref[...]).astype(BF16)
            acc = jnp.dot(xn_ref[rows, :], w_ref[...], preferred_element_type=F32)
            o_ref[rows, :] = (acc * s_ref[...]).astype(BF16)

    first = pl.program_id(1) == 0
    pl.when(first)(functools.partial(body, True))
    pl.when(jnp.logical_not(first))(functools.partial(body, False))


def _row_chunks(n_rows):
    return [slice(r, r + PROJ_ROWS) for r in range(0, n_rows, PROJ_ROWS)]


def _proj_rot_kernel(xn_ref, w_ref, cos_ref, sin_ref, o_ref):
    for rows in _row_chunks(xn_ref.shape[0]):
        acc = jnp.dot(xn_ref[rows, :], w_ref[...], preferred_element_type=F32)
        cos = cos_ref[rows, :]
        sin = sin_ref[rows, :]
        for p in range(w_ref.shape[1] // LANES):
            cols = slice(p * LANES, (p + 1) * LANES)
            seg = acc[:, cols]
            o_ref[rows, cols] = (seg * cos + pltpu.roll(seg, LANES // 2, 1) * sin).astype(BF16)


def _proj_swish_kernel(xn_ref, w_ref, o_ref):
    for rows in _row_chunks(xn_ref.shape[0]):
        half = jnp.dot(xn_ref[rows, :], w_ref[...], preferred_element_type=F32)
        o_ref[rows, :] = (half * (jnp.tanh(half) + 1.0)).astype(BF16)


def _proj_gate_kernel(xn_ref, w_ref, b_ref, o_ref):
    for rows in _row_chunks(xn_ref.shape[0]):
        half = jnp.dot(xn_ref[rows, :], w_ref[...], preferred_element_type=F32) + b_ref[...]
        o_ref[rows, :] = (0.5 * jnp.tanh(half) + 0.5).astype(BF16)


def _proj_params():
    return pltpu.CompilerParams(dimension_semantics=("arbitrary", "arbitrary"),
                                vmem_limit_bytes=VMEM_LIMIT)


def _proj_tile(n):
    tn = PROJ_TN
    while n % tn:
        tn -= LANES
    return tn


def _proj_plain_call(x2, g, w, scale, seq):
    t, n = x2.shape[0], w.shape[1]
    tn = _proj_tile(n)
    return pl.pallas_call(
        _proj_plain_kernel,
        grid=(t // seq, n // tn),
        in_specs=[
            pl.BlockSpec((seq, D_MODEL), lambda i, j: (i, 0)),
            pl.BlockSpec((1, D_MODEL), lambda i, j: (0, 0)),
            pl.BlockSpec((D_MODEL, tn), lambda i, j: (0, j)),
            pl.BlockSpec((1, tn), lambda i, j: (0, j)),
        ],
        out_specs=[
            pl.BlockSpec((seq, tn), lambda i, j: (i, j)),
            pl.BlockSpec((seq, D_MODEL), lambda i, j: (i, 0)),
        ],
        out_shape=[jax.ShapeDtypeStruct((t, n), BF16), jax.ShapeDtypeStruct((t, D_MODEL), BF16)],
        compiler_params=_proj_params(),
        name="proj_plain",
    )(x2, g, w, scale)


def _proj_rot_call(xn, w, cos_t, sin_t, seq):
    t, n = xn.shape[0], w.shape[1]
    tn = _proj_tile(n)
    return pl.pallas_call(
        _proj_rot_kernel,
        grid=(t // seq, n // tn),
        in_specs=[
            pl.BlockSpec((seq, D_MODEL), lambda i, j: (i, 0)),
            pl.BlockSpec((D_MODEL, tn), lambda i, j: (0, j)),
            pl.BlockSpec((seq, LANES), lambda i, j: (0, 0)),
            pl.BlockSpec((seq, LANES), lambda i, j: (0, 0)),
        ],
        out_specs=pl.BlockSpec((seq, tn), lambda i, j: (i, j)),
        out_shape=jax.ShapeDtypeStruct((t, n), BF16),
        compiler_params=_proj_params(),
        name="proj_rotary",
    )(xn, w, cos_t, sin_t)


def _proj_swish_call(xn, w_half, seq):
    t, n = xn.shape[0], w_half.shape[1]
    tn = _proj_tile(n)
    return pl.pallas_call(
        _proj_swish_kernel,
        grid=(t // seq, n // tn),
        in_specs=[
            pl.BlockSpec((seq, D_MODEL), lambda i, j: (i, 0)),
            pl.BlockSpec((D_MODEL, tn), lambda i, j: (0, j)),
        ],
        out_specs=pl.BlockSpec((seq, tn), lambda i, j: (i, j)),
        out_shape=jax.ShapeDtypeStruct((t, n), BF16),
        compiler_params=_proj_params(),
        name="proj_swish",
    )(xn, w_half)


def _proj_gate_call(xn, w_half, bias_half, seq):
    t, n = xn.shape[0], w_half.shape[1]
    tn = _proj_tile(n)
    return pl.pallas_call(
        _proj_gate_kernel,
        grid=(t // seq, n // tn),
        in_specs=[
            pl.BlockSpec((seq, D_MODEL), lambda i, j: (i, 0)),
            pl.BlockSpec((D_MODEL, tn), lambda i, j: (0, j)),
            pl.BlockSpec((1, tn), lambda i, j: (0, j)),
        ],
        out_specs=pl.BlockSpec((seq, tn), lambda i, j: (i, j)),
        out_shape=jax.ShapeDtypeStruct((t, n), BF16),
        compiler_params=_proj_params(),
        name="proj_gate",
    )(xn, w_half, bias_half)


def _rot_perm():
    half = RET_QK_DIM // 2
    order = [p * LANES + hh * RET_QK_DIM + part * half + d
             for p in range(RET_QK_WIDTH // LANES) for part in range(2) for hh in range(2)
             for d in range(half)]
    return np.asarray(order, np.int32)


def _sb_kernel(q_ref, k_ref, v_ref, o_ref, u_ref, acc_ref, car_ref):
    seq = q_ref.shape[0]
    nq = seq // SB_BLK
    row = lax.broadcasted_iota(jnp.int32, (SB_BLK, SB_BLK), 0)
    col = lax.broadcasted_iota(jnp.int32, (SB_BLK, SB_BLK), 1)
    u_ref[...] = (row >= col).astype(BF16)
    lane = lax.broadcasted_iota(jnp.int32, (1, LANES), 1)
    head_masks = (lane < SB_HEAD_DIM, lane >= SB_HEAD_DIM)

    heads = range(2)
    causal = col < row

    def rows_of(blk):
        return pl.ds(pl.multiple_of(blk * SB_BLK, SB_BLK), SB_BLK)

    def q_heads_of(qi):
        q_blk = q_ref[rows_of(qi), :]
        return tuple(jnp.where(m, q_blk, jnp.zeros_like(q_blk)) for m in head_masks)

    def sweep(jobs):
        u = u_ref[...]
        q_heads = [q_heads_of(qi) for _, qi, _, _ in jobs]
        z2 = [[[lax.dot_general(q_heads[j][h], k_ref[rows_of(kb), :], (((1,), (1,)), ((), ())),
                                preferred_element_type=F32) for h in heads]
               for kb, _ in steps] for j, (_, _, steps, _) in enumerate(jobs)]
        sinc = []
        for j, (_, _, steps, _) in enumerate(jobs):
            sinc.append([])
            for i, (_, diagonal) in enumerate(steps):
                nlk = [jnp.maximum(z, 0.0) + jnp.log2(1.0 + jnp.exp2(-jnp.abs(z))) for z in z2[j][i]]
                if diagonal:
                    nlk = [jnp.where(causal, a, 0.0) for a in nlk]
                sinc[j].append([jnp.dot(a.astype(BF16), u, preferred_element_type=F32) for a in nlk])
        for j, (slot, _, steps, fresh) in enumerate(jobs):
            car = [None, None] if fresh else [car_ref[slot, h] for h in heads]
            contrib = None
            for i, (kb, diagonal) in enumerate(steps):
                v_blk = v_ref[rows_of(kb), :]
                for h in heads:
                    e = z2[j][i][h] - sinc[j][i][h]
                    if car[h] is not None:
                        e = e - jnp.concatenate([car[h], car[h]], axis=1)
                    w = jnp.exp2(e)
                    if diagonal:
                        w = jnp.where(causal, w, 0.0)
                    vh = jnp.where(head_masks[h], v_blk, jnp.zeros_like(v_blk))
                    pv = jnp.dot(w.astype(BF16), vh, preferred_element_type=F32)
                    contrib = pv if contrib is None else contrib + pv
                    tot = jnp.broadcast_to(sinc[j][i][h][:, 0:1], (SB_BLK, LANES))
                    car[h] = tot if car[h] is None else car[h] + tot
            for h in heads:
                car_ref[slot, h] = car[h]
            if fresh:
                acc_ref[slot] = contrib
            else:
                acc_ref[slot] += contrib

    def finish(jobs):
        pending = [job for job in jobs if job[2] is not None]
        if pending:
            lo, hi = pending[0][0], pending[-1][0] + 1

            @pl.when(jnp.min(car_ref[lo:hi]) < SB_SKIP_LOG2)
            def _():
                for slot, qi, first_kb in pending:
                    def more(c):
                        kb, min_carry = c
                        return (kb >= 0) & (min_carry < SB_SKIP_LOG2)

                    def k_step(c, slot=slot, qi=qi):
                        kb, _ = c
                        sweep([(slot, qi, [(kb, False)], False)])
                        return kb - 1, jnp.min(car_ref[slot])

                    lax.while_loop(more, k_step, (first_kb, jnp.min(car_ref[slot])))
        for slot, qi, _ in jobs:
            o_ref[rows_of(qi), :] = acc_ref[slot].astype(BF16)

    def first_steps(qi):
        return [(qi, True)] + ([(qi - 1, False)] if qi > 0 else [])

    sweep([(s, s, first_steps(s), True) for s in range(SB_GROUP)])
    finish([(s, s, s - 2 if s >= 2 else None) for s in range(SB_GROUP)])

    def q_group(g, carry):
        q0 = g * SB_GROUP
        sweep([(s, q0 + s, [(q0 + s, True), (q0 + s - 1, False)], True) for s in range(SB_GROUP)])
        finish([(s, q0 + s, q0 + s - 2) for s in range(SB_GROUP)])
        return carry

    lax.fori_loop(1, nq // SB_GROUP, q_group, 0)


def _sb_call(proj, seq):
    t = proj.shape[0]
    assert seq % (SB_GROUP * SB_BLK) == 0
    n_pairs = SB_WIDTH // LANES
    return pl.pallas_call(
        _sb_kernel,
        grid=(t // seq, n_pairs),
        in_specs=[
            pl.BlockSpec((seq, LANES), lambda b, p: (b, p)),
            pl.BlockSpec((seq, LANES), lambda b, p: (b, n_pairs + p)),
            pl.BlockSpec((seq, LANES), lambda b, p: (b, 2 * n_pairs + p)),
        ],
        out_specs=pl.BlockSpec((seq, LANES), lambda b, p: (b, p)),
        out_shape=jax.ShapeDtypeStruct((t, SB_WIDTH), BF16),
        scratch_shapes=[
            pltpu.VMEM((SB_BLK, SB_BLK), BF16),
            pltpu.VMEM((SB_GROUP, SB_BLK, LANES), F32),
            pltpu.VMEM((SB_GROUP, 2, SB_BLK, LANES), F32),
        ],
        compiler_params=pltpu.CompilerParams(
            dimension_semantics=("arbitrary", "arbitrary"), vmem_limit_bytes=VMEM_LIMIT),
        name="stickbreak",
    )(proj, proj, proj)


def _ret_tables():
    h = np.arange(RET_HEADS, dtype=np.float64)
    log_gamma = np.log(1.0 - 2.0 ** (-5.0 - h))
    idx = np.arange(RET_BLK, dtype=np.float64)
    t, s = idx[:, None], idx[None, :]
    same = (t // CHUNK) == (s // CHUNK)
    earlier = (s // CHUNK) < (t // CHUNK)
    expo = np.where(same, np.abs(t - s), np.where(earlier, t - s, 0.0))
    dmat = np.exp(log_gamma[:, None, None] * expo) * (same | earlier)
    qdec = np.exp(log_gamma[:, None] * (idx + 1.0)[None, :])
    kdec = np.exp(log_gamma[:, None] * (RET_BLK - 1.0 - idx)[None, :])
    cdec = np.exp(log_gamma * RET_BLK)
    n_pairs = RET_HEADS // 2
    lane_head = (np.arange(LANES) % RET_QK_DIM) // (RET_QK_DIM // 2)
    col_head = np.arange(2 * RET_V_DIM) // RET_V_DIM
    pair_heads = np.arange(RET_HEADS).reshape(n_pairs, 2)
    qdec_pair = np.stack([qdec[pair_heads[p][lane_head]].T for p in range(n_pairs)])
    kdec_pair = np.stack([kdec[pair_heads[p][lane_head]].T for p in range(n_pairs)])
    cdec_pair = np.stack([cdec[pair_heads[p][col_head]][None, :] for p in range(n_pairs)])
    return (jnp.asarray(dmat.reshape((n_pairs, 2, RET_BLK, RET_BLK)), F32),
            jnp.asarray(qdec_pair, F32), jnp.asarray(kdec_pair, F32), jnp.asarray(cdec_pair, F32))


def _ret_kernel(q_ref, k_ref, v_ref, g_ref, gn_ref, dm_ref, qd_ref, kd_ref, cd_ref, o_ref, st_ref):
    seq = q_ref.shape[0]
    lane = lax.broadcasted_iota(jnp.int32, (1, LANES), 1)
    head0 = (lane % RET_QK_DIM) < (RET_QK_DIM // 2)
    head_masks = (head0, jnp.logical_not(head0))
    st_row = lax.broadcasted_iota(jnp.int32, st_ref.shape, 0)
    st_col = lax.broadcasted_iota(jnp.int32, st_ref.shape, 1)
    own_head = ((st_row % RET_QK_DIM) // (RET_QK_DIM // 2)) == (st_col // RET_V_DIM)
    heads = range(2)
    nt = (((1,), (1,)), ((), ()))
    tn = (((0,), (0,)), ((), ()))
    st_ref[...] = jnp.zeros_like(st_ref)

    def step(n, carry):
        subs = range(RET_UNROLL)
        rows = [pl.ds(pl.multiple_of((n * RET_UNROLL + u) * RET_BLK, RET_BLK), RET_BLK) for u in subs]
        col = [slice(h * RET_V_DIM, (h + 1) * RET_V_DIM) for h in heads]
        q_blk = [q_ref[r, :] for r in rows]
        k_blk = [k_ref[r, :] for r in rows]
        qm = [[jnp.where(head_masks[h], q_blk[u], jnp.zeros_like(q_blk[u])) for h in heads] for u in subs]
        v = [v_ref[r, :] for r in rows]
        scores = [[lax.dot_general(qm[u][h], k_blk[u], nt, preferred_element_type=F32)
                   for h in heads] for u in subs]
        kd = [(k_blk[u].astype(F32) * kd_ref[0]).astype(BF16) for u in subs]
        kv = [jnp.where(own_head, lax.dot_general(kd[u], v[u], tn, preferred_element_type=F32), 0.0)
              for u in subs]
        qd = [(q_blk[u].astype(F32) * qd_ref[0]).astype(BF16) for u in subs]
        p = [[(scores[u][h] * dm_ref[0, h]).astype(BF16) for h in heads] for u in subs]
        intra = [[jnp.dot(p[u][h], v[u][:, col[h]], preferred_element_type=F32) for h in heads]
                 for u in subs]
        state = st_ref[...]
        for u in subs:
            cross = jnp.dot(qd[u], state.astype(BF16), preferred_element_type=F32)
            state = state * cd_ref[0] + kv[u]
            for h in heads:
                y = intra[u][h] + cross[:, col[h]]
                ms = jnp.mean(y * y, axis=-1, keepdims=True)
                yn = (y * lax.rsqrt(ms + EPS)) * gn_ref[:, col[h]]
                o_ref[rows[u], col[h]] = (g_ref[rows[u], col[h]].astype(F32) * yn).astype(BF16)
        st_ref[...] = state
        return carry

    lax.fori_loop(0, seq // (RET_BLK * RET_UNROLL), step, 0)


def _ret_call(rqk, plain, swish, ret_norm_g, seq):
    t = rqk.shape[0]
    n_pairs = RET_HEADS // 2
    dmat, qdec, kdec, cdec = _ret_tables()
    k0 = RET_QK_WIDTH // LANES
    v0 = (3 * SB_WIDTH) // (2 * RET_V_DIM)
    return pl.pallas_call(
        _ret_kernel,
        grid=(t // seq, n_pairs),
        in_specs=[
            pl.BlockSpec((seq, LANES), lambda b, p: (b, p)),
            pl.BlockSpec((seq, LANES), lambda b, p: (b, k0 + p)),
            pl.BlockSpec((seq, 2 * RET_V_DIM), lambda b, p: (b, v0 + p)),
            pl.BlockSpec((seq, 2 * RET_V_DIM), lambda b, p: (b, p)),
            pl.BlockSpec((1, 2 * RET_V_DIM), lambda b, p: (0, p)),
            pl.BlockSpec((1, 2, RET_BLK, RET_BLK), lambda b, p: (p, 0, 0, 0)),
            pl.BlockSpec((1, RET_BLK, LANES), lambda b, p: (p, 0, 0)),
            pl.BlockSpec((1, RET_BLK, LANES), lambda b, p: (p, 0, 0)),
            pl.BlockSpec((1, 1, 2 * RET_V_DIM), lambda b, p: (p, 0, 0)),
        ],
        out_specs=pl.BlockSpec((seq, 2 * RET_V_DIM), lambda b, p: (b, p)),
        out_shape=jax.ShapeDtypeStruct((t, RET_V_WIDTH), BF16),
        scratch_shapes=[pltpu.VMEM((LANES, 2 * RET_V_DIM), F32)],
        compiler_params=pltpu.CompilerParams(
            dimension_semantics=("arbitrary", "arbitrary"), vmem_limit_bytes=VMEM_LIMIT),
        name="retention",
    )(rqk, rqk, plain, swish, ret_norm_g, dmat, qdec, kdec, cdec)


def _mix_kernel(ysb_ref, yret_ref, gs0_ref, gs1_ref, gr0_ref, gr1_ref, x_ref,
                wsb_ref, wret_ref, wout_ref, gffn_ref, wrh_ref, wrl_ref, br_ref, tri_ref,
                hext_ref, cls_ref, cnt_ref):
    tm = x_ref.shape[0]

    @pl.when(pl.program_id(0) == 0)
    def _():
        cnt_ref[...] = jnp.zeros_like(cnt_ref)

    a = jnp.dot(ysb_ref[...], wsb_ref[...], preferred_element_type=F32)
    b = jnp.dot(yret_ref[...], wret_ref[...], preferred_element_type=F32)
    g_sb = jnp.concatenate([gs0_ref[...], gs1_ref[...]], axis=1).astype(F32)
    g_ret = jnp.concatenate([gr0_ref[...], gr1_ref[...]], axis=1).astype(F32)
    mixed = (g_sb * a + g_ret * b).astype(BF16)
    h = x_ref[...] + jnp.dot(mixed, wout_ref[...], preferred_element_type=F32)
    hext_ref[:, 0:D_MODEL] = h

    ms = jnp.mean(h * h, axis=-1, keepdims=True)
    hn = (h * lax.rsqrt(ms + EPS)) * gffn_ref[...]
    hn_hi = hn.astype(BF16)
    hn_lo = (hn - hn_hi.astype(F32)).astype(BF16)
    nt = (((1,), (1,)), ((), ()))
    w_both = jnp.concatenate([wrh_ref[...], wrl_ref[...]], axis=0)
    by_hi = lax.dot_general(w_both, hn_hi, nt, preferred_element_type=F32)
    logits = (by_hi[0:ROUTER_ROWS] + by_hi[ROUTER_ROWS:2 * ROUTER_ROWS]
              + lax.dot_general(wrh_ref[...], hn_lo, nt, preferred_element_type=F32)
              + br_ref[...])

    def first_argmax(vals):
        m = functools.reduce(jnp.maximum, vals)
        idx = jnp.full(m.shape, len(vals) - 1, jnp.int32)
        for i in range(len(vals) - 2, -1, -1):
            idx = jnp.where(vals[i] >= m, i, idx)
        return m, idx

    gl = [logits[r:r + 1, :] for r in range(N_GROUPS)]
    gmax, gsel = first_argmax(gl)
    p_group = 1.0 / functools.reduce(lambda s, v: s + v, [jnp.exp(v - gmax) for v in gl])
    el = []
    for e in range(EXPERTS_PER_GROUP):
        v = logits[N_GROUPS + 3 * EXPERTS_PER_GROUP + e:N_GROUPS + 3 * EXPERTS_PER_GROUP + e + 1, :]
        for g in range(N_GROUPS - 2, -1, -1):
            r = N_GROUPS + g * EXPERTS_PER_GROUP + e
            v = jnp.where(gsel == g, logits[r:r + 1, :], v)
        el.append(v)
    m1, i1 = first_argmax(el)
    rest = [jnp.where(i1 == e, -jnp.inf, el[e]) for e in range(EXPERTS_PER_GROUP)]
    m2, i2 = first_argmax(rest)
    tt = jnp.exp(m2 - m1)
    gate1 = p_group * (1.0 / (1.0 + tt))
    gate2 = p_group * (tt / (1.0 + tt))
    lo_first = i1 < i2
    ea = jnp.minimum(i1, i2)
    eb = jnp.maximum(i1, i2)
    w_a = jnp.where(lo_first, gate1, gate2)
    w_b = jnp.where(lo_first, gate2, gate1)
    pid = jnp.where(ea == 0, eb - 1, jnp.where(ea == 1, eb + 1, PAIRS_PER_GROUP - 1))
    cls = gsel * PAIRS_PER_GROUP + pid
    info = jnp.concatenate([w_a, w_b, jnp.zeros((INFO_COLS - 2, tm), F32)], axis=0)
    hext_ref[:, D_MODEL:HEXT_COLS] = info.T

    class_row = lax.broadcasted_iota(jnp.int32, (ROUTER_ROWS, tm), 0)
    onehot = class_row == cls
    onehot_bf = onehot.astype(BF16)
    before = jnp.dot(onehot_bf, tri_ref[...], preferred_element_type=F32)
    cnt = cnt_ref[...]
    seen = before + jnp.concatenate([cnt] * (tm // LANES), axis=1)
    rank = jnp.sum(jnp.where(onehot, seen, 0.0), axis=0, keepdims=True)
    cnt_ref[...] = cnt + jnp.dot(onehot_bf, jnp.ones((tm, LANES), BF16), preferred_element_type=F32)
    cls_ref[...] = jnp.concatenate(
        [cls, rank.astype(jnp.int32), jnp.zeros((cls_ref.shape[0] - 2, tm), jnp.int32)], axis=0)


def _mix_call(ysb, yret, gates, x2, wsb, wret, wout, gffn, wr_hi, wr_lo, br):
    t = x2.shape[0]
    tm = MIX_TM
    gate0 = 0
    const = lambda i: (0, 0)
    idx = jnp.arange(tm, dtype=jnp.int32)
    tri = (idx[:, None] < idx[None, :]).astype(BF16)
    return pl.pallas_call(
        _mix_kernel,
        grid=(t // tm,),
        in_specs=[
            pl.BlockSpec((tm, SB_WIDTH), lambda i: (i, 0)),
            pl.BlockSpec((tm, RET_V_WIDTH), lambda i: (i, 0)),
            pl.BlockSpec((tm, GATE_BLK), lambda i: (i, gate0)),
            pl.BlockSpec((tm, GATE_BLK), lambda i: (i, gate0 + 1)),
            pl.BlockSpec((tm, GATE_BLK), lambda i: (i, gate0 + 2)),
            pl.BlockSpec((tm, GATE_BLK), lambda i: (i, gate0 + 3)),
            pl.BlockSpec((tm, D_MODEL), lambda i: (i, 0)),
            pl.BlockSpec((SB_WIDTH, D_MODEL), const),
            pl.BlockSpec((RET_V_WIDTH, D_MODEL), const),
            pl.BlockSpec((D_MODEL, D_MODEL), const),
            pl.BlockSpec((1, D_MODEL), const),
            pl.BlockSpec((ROUTER_ROWS, D_MODEL), const),
            pl.BlockSpec((ROUTER_ROWS, D_MODEL), const),
            pl.BlockSpec((ROUTER_ROWS, 1), const),
            pl.BlockSpec((tm, tm), const),
        ],
        out_specs=[
            pl.BlockSpec((tm, HEXT_COLS), lambda i: (i, 0)),
            pl.BlockSpec((8, tm), lambda i: (0, i)),
            pl.BlockSpec((ROUTER_ROWS, LANES), const),
        ],
        out_shape=[
            jax.ShapeDtypeStruct((t, HEXT_COLS), F32),
            jax.ShapeDtypeStruct((8, t), jnp.int32),
            jax.ShapeDtypeStruct((ROUTER_ROWS, LANES), F32),
        ],
        compiler_params=pltpu.CompilerParams(
            dimension_semantics=("arbitrary",), vmem_limit_bytes=VMEM_LIMIT),
        name="mix_router",
    )(ysb, yret, gates, gates, gates, gates, x2, wsb, wret, wout, gffn, wr_hi, wr_lo, br, tri)


def _moe_kernel(dest_ref, nvalid_ref, ea_ref, eb_ref,
                hext_ref, wga_ref, wua_ref, wda_ref, wgb_ref, wub_ref, wdb_ref,
                gffn_ref, gfin_ref, out_ref, hbuf, obuf, src, gsem, ssem):
    i = pl.program_id(0)
    n_tiles = pl.num_programs(0)
    tm = hbuf.shape[1]
    nv = nvalid_ref[i]
    nxt = jnp.minimum(i + 1, n_tiles - 1)
    nv_next = jnp.where(i + 1 < n_tiles, nvalid_ref[nxt], 0)
    has_next = nv_next > 0

    def gather_copy(tile, buf_slot, r):
        tok = src[tile * tm + r]
        return pltpu.make_async_copy(hext_ref.at[pl.ds(tok, 1), :],
                                     hbuf.at[buf_slot, pl.ds(r, 1), :], gsem.at[buf_slot])

    def scatter_copy(tile, buf_slot, r):
        tok = src[tile * tm + r]
        return pltpu.make_async_copy(obuf.at[buf_slot, pl.ds(r, 1), :],
                                     out_ref.at[pl.ds(tok, 1), :], ssem.at[buf_slot])

    def start_rows(copy_of_row, rows_valid):
        for r in range(tm):
            if rows_valid is None:
                copy_of_row(r).start(priority=r % 2)
            else:
                @pl.when(r < rows_valid)
                def _():
                    copy_of_row(r).start(priority=r % 2)

    def wait_rows(full_copy, row_copy, rows_valid):
        @pl.when(rows_valid == tm)
        def _():
            full_copy.wait()

        @pl.when(rows_valid < tm)
        def _():
            for r in range(tm):
                @pl.when(r < rows_valid)
                def _():
                    row_copy(r).wait()

    def wait_gather(buf_slot, rows_valid):
        wait_rows(pltpu.make_async_copy(hext_ref.at[pl.ds(0, tm), :], hbuf.at[buf_slot], gsem.at[buf_slot]),
                  lambda r: pltpu.make_async_copy(hext_ref.at[pl.ds(0, 1), :],
                                                  hbuf.at[buf_slot, pl.ds(r, 1), :], gsem.at[buf_slot]),
                  rows_valid)

    def wait_scatter(buf_slot, rows_valid):
        wait_rows(pltpu.make_async_copy(obuf.at[buf_slot], out_ref.at[pl.ds(0, tm), :], ssem.at[buf_slot]),
                  lambda r: pltpu.make_async_copy(obuf.at[buf_slot, pl.ds(r, 1), :],
                                                  out_ref.at[pl.ds(0, 1), :], ssem.at[buf_slot]),
                  rows_valid)

    @pl.when(i == 0)
    def _():
        def invert(c, carry):
            for u in range(MOE_INVERT_UNROLL):
                t = c * MOE_INVERT_UNROLL + u
                src[dest_ref[t]] = t
            return carry

        lax.fori_loop(0, dest_ref.shape[0] // MOE_INVERT_UNROLL, invert, 0)
        hbuf[...] = jnp.zeros_like(hbuf)
        start_rows(functools.partial(gather_copy, 0, 0), nv)

    def tile_body(slot):
        @pl.when(i >= 2)
        def _():
            wait_scatter(slot, nvalid_ref[jnp.maximum(i - 2, 0)])

        wait_gather(slot, nv)
        both_full = (nv == tm) & (nv_next == tm)
        pl.when(both_full)(functools.partial(tile_main, slot, None, None))
        pl.when(jnp.logical_not(both_full))(functools.partial(tile_main, slot, nv, nv_next))

        @pl.when(jnp.logical_not(has_next))
        def _():
            @pl.when(i >= 1)
            def _():
                wait_scatter(1 - slot, nvalid_ref[jnp.maximum(i - 1, 0)])

            wait_scatter(slot, nv)

    def tile_main(slot, rows_now, rows_next):
        start_rows(functools.partial(gather_copy, nxt, 1 - slot), rows_next)
        hrows = hbuf[slot]
        h = hrows[:, 0:D_MODEL]
        w_a = hrows[:, D_MODEL:D_MODEL + 1]
        w_b = hrows[:, D_MODEL + 1:D_MODEL + 2]
        ms = jnp.mean(h * h, axis=-1, keepdims=True)
        hn = ((h * lax.rsqrt(ms + EPS)) * gffn_ref[...]).astype(BF16)

        def expert(wg_ref, wu_ref, wd_ref):
            half = jnp.dot(hn, wg_ref[0], preferred_element_type=F32)
            up = jnp.dot(hn, wu_ref[0], preferred_element_type=F32)
            hidden = ((half * (jnp.tanh(half) + 1.0)) * up).astype(BF16)
            return jnp.dot(hidden, wd_ref[0], preferred_element_type=F32)

        y = w_a * expert(wga_ref, wua_ref, wda_ref) + w_b * expert(wgb_ref, wub_ref, wdb_ref)
        h2 = h + y
        ms2 = jnp.mean(h2 * h2, axis=-1, keepdims=True)
        obuf[slot] = (h2 * lax.rsqrt(ms2 + EPS)) * gfin_ref[...]
        start_rows(functools.partial(scatter_copy, i, slot), rows_now)

    for parity in range(2):
        pl.when((nv > 0) & (i % 2 == parity))(functools.partial(tile_body, parity))


def _moe_call(dest, nvalid, ea, eb, hext, wg, wu, wd, gffn, gfin):
    t = hext.shape[0]
    tm = MOE_TM
    n_tiles = nvalid.shape[0]
    assert t % MOE_INVERT_UNROLL == 0
    wa_map = lambda i, dest, nv, ea, eb: (ea[i], 0, 0)
    wb_map = lambda i, dest, nv, ea, eb: (eb[i], 0, 0)
    const = lambda i, dest, nv, ea, eb: (0, 0)
    return pl.pallas_call(
        _moe_kernel,
        grid_spec=pltpu.PrefetchScalarGridSpec(
            num_scalar_prefetch=4,
            grid=(n_tiles,),
            in_specs=[
                pl.BlockSpec(memory_space=pl.ANY),
                pl.BlockSpec((1, D_MODEL, D_FF), wa_map),
                pl.BlockSpec((1, D_MODEL, D_FF), wa_map),
                pl.BlockSpec((1, D_FF, D_MODEL), wa_map),
                pl.BlockSpec((1, D_MODEL, D_FF), wb_map),
                pl.BlockSpec((1, D_MODEL, D_FF), wb_map),
                pl.BlockSpec((1, D_FF, D_MODEL), wb_map),
                pl.BlockSpec((1, D_MODEL), const),
                pl.BlockSpec((1, D_MODEL), const),
            ],
            out_specs=pl.BlockSpec(memory_space=pl.ANY),
            scratch_shapes=[
                pltpu.VMEM((2, tm, HEXT_COLS), F32),
                pltpu.VMEM((2, tm, D_MODEL), F32),
                pltpu.SMEM((n_tiles * tm,), jnp.int32),
                pltpu.SemaphoreType.DMA((2,)),
                pltpu.SemaphoreType.DMA((2,)),
            ],
        ),
        out_shape=jax.ShapeDtypeStruct((t, D_MODEL), F32),
        compiler_params=pltpu.CompilerParams(
            dimension_semantics=("arbitrary",), vmem_limit_bytes=VMEM_LIMIT),
        name="experts",
    )(dest, nvalid, ea, eb, hext, wg, wu, wd, wg, wu, wd, gffn, gfin)


def _routing_plan(cls, rank, counts, tm):
    t = cls.shape[0]
    n_tiles = t // tm + N_CLASSES
    tiles_c = (counts + tm - 1) // tm
    tile_end = jnp.cumsum(tiles_c)
    tile_off = tile_end - tiles_c
    onehot = cls[:, None] == jnp.arange(N_CLASSES, dtype=jnp.int32)[None, :]
    dest = jnp.sum(jnp.where(onehot, tile_off[None, :], 0), axis=1) * tm + rank
    tile = jnp.arange(n_tiles, dtype=jnp.int32)
    used = tile < tile_end[-1]
    tile_cls = jnp.minimum(jnp.sum((tile[:, None] >= tile_end[None, :]).astype(jnp.int32), axis=1),
                           N_CLASSES - 1)
    cls_onehot = tile_cls[:, None] == jnp.arange(N_CLASSES, dtype=jnp.int32)[None, :]
    pick = lambda table: jnp.sum(jnp.where(cls_onehot, table[None, :], 0), axis=1)
    nvalid = jnp.where(used, jnp.clip(pick(counts) - (tile - pick(tile_off)) * tm, 0, tm), 0)
    classes = np.arange(N_CLASSES)
    ea_tab = jnp.asarray((classes // PAIRS_PER_GROUP) * EXPERTS_PER_GROUP
                         + np.asarray(_PAIR_A)[classes % PAIRS_PER_GROUP], jnp.int32)
    eb_tab = jnp.asarray((classes // PAIRS_PER_GROUP) * EXPERTS_PER_GROUP
                         + np.asarray(_PAIR_B)[classes % PAIRS_PER_GROUP], jnp.int32)
    last_used = jnp.sum(jnp.where(tile == tile_end[-1] - 1, tile_cls, 0))
    tile_cls = jnp.where(used, tile_cls, last_used)
    cls_onehot = tile_cls[:, None] == jnp.arange(N_CLASSES, dtype=jnp.int32)[None, :]
    return dest.astype(jnp.int32), nvalid.astype(jnp.int32), pick(ea_tab), pick(eb_tab)


def _rope_tables(seq):
    half = RET_QK_DIM // 2
    inv_freq = ROPE_BASE ** (-jnp.arange(half, dtype=F32) / half)
    ang = jnp.arange(seq, dtype=F32)[:, None] * inv_freq[None, :]
    cos, sin = jnp.cos(ang), jnp.sin(ang)
    cos_t = jnp.tile(cos, (1, LANES // half))
    sin_t = jnp.concatenate([-sin, -sin, sin, sin], axis=1)
    return cos_t, sin_t


def kernel(x, norm_mix_g, w_in, w_gate, b_gate, w_sb_out, w_ret_out, ret_norm_g, w_out,
           norm_ffn_g, w_group_router, b_group_router, w_expert_router, b_expert_router,
           w_exp_gate, w_exp_up, w_exp_down, norm_final_g):
    bsz, seq, d = x.shape
    t = bsz * seq
    assert d == D_MODEL and w_in.shape[0] == 1 and t % MIX_TM == 0
    assert seq % (SB_GROUP * SB_BLK) == 0 and seq % (RET_BLK * RET_UNROLL) == 0
    x2 = x.reshape(t, d)
    wi = w_in[0]
    c_rq = 3 * SB_WIDTH
    c_rk = c_rq + RET_QK_WIDTH
    c_rv = c_rk + RET_QK_WIDTH
    c_rg = c_rv + RET_V_WIDTH
    ones = functools.partial(jnp.ones, dtype=F32)
    w_plain = jnp.concatenate([wi[:, :c_rq], wi[:, c_rv:c_rg]], axis=1).astype(BF16)
    s_plain = jnp.concatenate([jnp.full((SB_WIDTH,), SB_HEAD_DIM ** -0.5 * LOG2E, F32),
                               ones((2 * SB_WIDTH + RET_V_WIDTH,))])[None, :]
    perm = _rot_perm()
    w_rot = jnp.concatenate([wi[:, c_rq:c_rk][:, perm],
                             (RET_QK_DIM ** -0.5) * wi[:, c_rk:c_rv][:, perm]], axis=1).astype(BF16)
    w_swish_half = (0.5 * wi[:, c_rg:]).astype(BF16)
    w_gate_half = (0.5 * w_gate[0]).astype(BF16)
    b_gate_half = (0.5 * b_gate[0])[None, :]
    cos_t, sin_t = _rope_tables(seq)

    plain, xn = _proj_plain_call(x2, norm_mix_g[0][None, :], w_plain, s_plain, seq)
    rqk = _proj_rot_call(xn, w_rot, cos_t, sin_t, seq)
    swish = _proj_swish_call(xn, w_swish_half, seq)
    gates = _proj_gate_call(xn, w_gate_half, b_gate_half, seq)
    ysb = _sb_call(plain, seq)
    yret = _ret_call(rqk, plain, swish, ret_norm_g[0][None, :], seq)

    wr = jnp.concatenate([w_group_router[0], w_expert_router[0]], axis=1).T
    wr = jnp.pad(wr, ((0, ROUTER_ROWS - wr.shape[0]), (0, 0)))
    wr_hi = wr.astype(BF16)
    wr_lo = (wr - wr_hi.astype(F32)).astype(BF16)
    br = jnp.concatenate([b_group_router[0], b_expert_router[0]])
    br = jnp.pad(br, (0, ROUTER_ROWS - br.shape[0]))[:, None]
    hext, cls8, cnt = _mix_call(ysb, yret, gates, x2, w_sb_out[0].astype(BF16), w_ret_out[0].astype(BF16),
                           w_out[0].astype(BF16), norm_ffn_g[0][None, :], wr_hi, wr_lo, br)

    counts = cnt[:N_CLASSES, 0].astype(jnp.int32)
    dest, nvalid, ea, eb = _routing_plan(cls8[0], cls8[1], counts, MOE_TM)
    out = _moe_call(dest, nvalid, ea, eb, hext, (0.5 * w_exp_gate[0]).astype(BF16), w_exp_up[0].astype(BF16),
                    w_exp_down[0].astype(BF16), norm_ffn_g[0][None, :], norm_final_g[None, :])
    return out.reshape(bsz, seq, d)
```

```python
import functools
import math

import numpy as np
import jax
import jax.numpy as jnp
from jax import lax
from jax.experimental import pallas as pl
from jax.experimental.pallas import tpu as pltpu

F32 = jnp.float32
BF16 = jnp.bfloat16

D_MODEL = 1024
SB_HEADS = 8
SB_HEAD_DIM = 64
SB_WIDTH = SB_HEADS * SB_HEAD_DIM
RET_HEADS = 8
RET_QK_DIM = 64
RET_V_DIM = 128
RET_QK_WIDTH = RET_HEADS * RET_QK_DIM
RET_V_WIDTH = RET_HEADS * RET_V_DIM
CHUNK = 64
ROPE_BASE = 10000.0
N_GROUPS = 4
EXPERTS_PER_GROUP = 4
D_FF = 512
EPS = 1e-6

LANES = 128
PROJ_TN = 1280
GATE_BLK = 512
PROJ_ROWS = 256
SB_BLK = 256
SB_GROUP = 8
RET_BLK = 256
RET_UNROLL = 8
MIX_TM = 1024
MOE_TM = 256
MOE_INVERT_UNROLL = 32
ROUTER_ROWS = 32
PAIRS_PER_GROUP = 6
N_CLASSES = N_GROUPS * PAIRS_PER_GROUP
INFO_COLS = LANES
HEXT_COLS = D_MODEL + INFO_COLS
VMEM_LIMIT = 56 * 1024 * 1024
LOG2E = math.log2(math.e)
SB_SKIP_LOG2 = 156.0

_PAIR_A = (0, 0, 0, 1, 1, 2)
_PAIR_B = (1, 2, 3, 2, 3, 3)


def _proj_plain_kernel(x_ref, g_ref, w_ref, s_ref, o_ref, xn_ref):
    def body(normalise):
        for rows in _row_chunks(xn_ref.shape[0]):
            if normalise:
                x = x_ref[rows, :]
                ms = jnp.mean(x * x, axis=-1, keepdims=True)
                xn_ref[rows, :] = ((x * lax.rsqrt(ms + EPS)) * g_ref[...]).astype(BF16)
            acc = jnp.dot(xn_ref[rows, :], w_ref[...], preferred_element_type=F32)
            o_ref[rows, :] = (acc * s_ref[...]).astype(BF16)

    first = pl.program_id(1) == 0
    pl.when(first)(functools.partial(body, True))
    pl.when(jnp.logical_not(first))(functools.partial(body, False))


def _row_chunks(n_rows):
    return [slice(r, r + PROJ_ROWS) for r in range(0, n_rows, PROJ_ROWS)]


def _proj_rot_kernel(xn_ref, w_ref, cos_ref, sin_ref, o_ref):
    for rows in _row_chunks(xn_ref.shape[0]):
        acc = jnp.dot(xn_ref[rows, :], w_ref[...], preferred_element_type=F32)
        cos = cos_ref[rows, :]
        sin = sin_ref[rows, :]
        for p in range(w_ref.shape[1] // LANES):
            cols = slice(p * LANES, (p + 1) * LANES)
            seg = acc[:, cols]
            o_ref[rows, cols] = (seg * cos + pltpu.roll(seg, LANES // 2, 1) * sin).astype(BF16)


def _proj_swish_kernel(xn_ref, w_ref, o_ref):
    for rows in _row_chunks(xn_ref.shape[0]):
        half = jnp.dot(xn_ref[rows, :], w_ref[...], preferred_element_type=F32)
        o_ref[rows, :] = (half * (jnp.tanh(half) + 1.0)).astype(BF16)


def _proj_gate_kernel(xn_ref, w_ref, b_ref, o_ref):
    for rows in _row_chunks(xn_ref.shape[0]):
        half = jnp.dot(xn_ref[rows, :], w_ref[...], preferred_element_type=F32) + b_ref[...]
        o_ref[rows, :] = (0.5 * jnp.tanh(half) + 0.5).astype(BF16)


def _proj_params():
    return pltpu.CompilerParams(dimension_semantics=("arbitrary", "arbitrary"),
                                vmem_limit_bytes=VMEM_LIMIT)


def _proj_tile(n):
    tn = PROJ_TN
    while n % tn:
        tn -= LANES
    return tn


def _proj_plain_call(x2, g, w, scale, seq):
    t, n = x2.shape[0], w.shape[1]
    tn = _proj_tile(n)
    return pl.pallas_call(
        _proj_plain_kernel,
        grid=(t // seq, n // tn),
        in_specs=[
            pl.BlockSpec((seq, D_MODEL), lambda i, j: (i, 0)),
            pl.BlockSpec((1, D_MODEL), lambda i, j: (0, 0)),
            pl.BlockSpec((D_MODEL, tn), lambda i, j: (0, j)),
            pl.BlockSpec((1, tn), lambda i, j: (0, j)),
        ],
        out_specs=[
            pl.BlockSpec((seq, tn), lambda i, j: (i, j)),
            pl.BlockSpec((seq, D_MODEL), lambda i, j: (i, 0)),
        ],
        out_shape=[jax.ShapeDtypeStruct((t, n), BF16), jax.ShapeDtypeStruct((t, D_MODEL), BF16)],
        compiler_params=_proj_params(),
        name="proj_plain",
    )(x2, g, w, scale)


def _proj_rot_call(xn, w, cos_t, sin_t, seq):
    t, n = xn.shape[0], w.shape[1]
    tn = _proj_tile(n)
    return pl.pallas_call(
        _proj_rot_kernel,
        grid=(t // seq, n // tn),
        in_specs=[
            pl.BlockSpec((seq, D_MODEL), lambda i, j: (i, 0)),
            pl.BlockSpec((D_MODEL, tn), lambda i, j: (0, j)),
            pl.BlockSpec((seq, LANES), lambda i, j: (0, 0)),
            pl.BlockSpec((seq, LANES), lambda i, j: (0, 0)),
        ],
        out_specs=pl.BlockSpec((seq, tn), lambda i, j: (i, j)),
        out_shape=jax.ShapeDtypeStruct((t, n), BF16),
        compiler_params=_proj_params(),
        name="proj_rotary",
    )(xn, w, cos_t, sin_t)


def _proj_swish_call(xn, w_half, seq):
    t, n = xn.shape[0], w_half.shape[1]
    tn = _proj_tile(n)
    return pl.pallas_call(
        _proj_swish_kernel,
        grid=(t // seq, n // tn),
        in_specs=[
            pl.BlockSpec((seq, D_MODEL), lambda i, j: (i, 0)),
            pl.BlockSpec((D_MODEL, tn), lambda i, j: (0, j)),
        ],
        out_specs=pl.BlockSpec((seq, tn), lambda i, j: (i, j)),
        out_shape=jax.ShapeDtypeStruct((t, n), BF16),
        compiler_params=_proj_params(),
        name="proj_swish",
    )(xn, w_half)


def _proj_gate_call(xn, w_half, bias_half, seq):
    t, n = xn.shape[0], w_half.shape[1]
    tn = _proj_tile(n)
    return pl.pallas_call(
        _proj_gate_kernel,
        grid=(t // seq, n // tn),
        in_specs=[
            pl.BlockSpec((seq, D_MODEL), lambda i, j: (i, 0)),
            pl.BlockSpec((D_MODEL, tn), lambda i, j: (0, j)),
            pl.BlockSpec((1, tn), lambda i, j: (0, j)),
        ],
        out_specs=pl.BlockSpec((seq, tn), lambda i, j: (i, j)),
        out_shape=jax.ShapeDtypeStruct((t, n), BF16),
        compiler_params=_proj_params(),
        name="proj_gate",
    )(xn, w_half, bias_half)


def _rot_perm():
    half = RET_QK_DIM // 2
    order = [p * LANES + hh * RET_QK_DIM + part * half + d
             for p in range(RET_QK_WIDTH // LANES) for part in range(2) for hh in range(2)
             for d in range(half)]
    return np.asarray(order, np.int32)


def _sb_kernel(q_ref, k_ref, v_ref, o_ref, u_ref, acc_ref, car_ref):
    seq = q_ref.shape[0]
    nq = seq // SB_BLK
    row = lax.broadcasted_iota(jnp.int32, (SB_BLK, SB_BLK), 0)
    col = lax.broadcasted_iota(jnp.int32, (SB_BLK, SB_BLK), 1)
    u_ref[...] = (row >= col).astype(BF16)
    lane = lax.broadcasted_iota(jnp.int32, (1, LANES), 1)
    head_masks = (lane < SB_HEAD_DIM, lane >= SB_HEAD_DIM)

    heads = range(2)
    causal = col < row

    def rows_of(blk):
        return pl.ds(pl.multiple_of(blk * SB_BLK, SB_BLK), SB_BLK)

    def q_heads_of(qi):
        q_blk = q_ref[rows_of(qi), :]
        return tuple(jnp.where(m, q_blk, jnp.zeros_like(q_blk)) for m in head_masks)

    def sweep(jobs):
        u = u_ref[...]
        q_heads = [q_heads_of(qi) for _, qi, _, _ in jobs]
        z2 = [[[lax.dot_general(q_heads[j][h], k_ref[rows_of(kb), :], (((1,), (1,)), ((), ())),
                                preferred_element_type=F32) for h in heads]
               for kb, _ in steps] for j, (_, _, steps, _) in enumerate(jobs)]
        sinc = []
        for j, (_, _, steps, _) in enumerate(jobs):
            sinc.append([])
            for i, (_, diagonal) in enumerate(steps):
                nlk = [jnp.maximum(z, 0.0) + jnp.log2(1.0 + jnp.exp2(-jnp.abs(z))) for z in z2[j][i]]
                if diagonal:
                    nlk = [jnp.where(causal, a, 0.0) for a in nlk]
                sinc[j].append([jnp.dot(a.astype(BF16), u, preferred_element_type=F32) for a in nlk])
        for j, (slot, _, steps, fresh) in enumerate(jobs):
            car = [None, None] if fresh else [car_ref[slot, h] for h in heads]
            contrib = None
            for i, (kb, diagonal) in enumerate(steps):
                v_blk = v_ref[rows_of(kb), :]
                for h in heads:
                    e = z2[j][i][h] - sinc[j][i][h]
                    if car[h] is not None:
                        e = e - jnp.concatenate([car[h], car[h]], axis=1)
                    w = jnp.exp2(e)
                    if diagonal:
                        w = jnp.where(causal, w, 0.0)
                    vh = jnp.where(head_masks[h], v_blk, jnp.zeros_like(v_blk))
                    pv = jnp.dot(w.astype(BF16), vh, preferred_element_type=F32)
                    contrib = pv if contrib is None else contrib + pv
                    tot = jnp.broadcast_to(sinc[j][i][h][:, 0:1], (SB_BLK, LANES))
                    car[h] = tot if car[h] is None else car[h] + tot
            for h in heads:
                car_ref[slot, h] = car[h]
            if fresh:
                acc_ref[slot] = contrib
            else:
                acc_ref[slot] += contrib

    def finish(jobs):
        pending = [job for job in jobs if job[2] is not None]
        if pending:
            lo, hi = pending[0][0], pending[-1][0] + 1

            @pl.when(jnp.min(car_ref[lo:hi]) < SB_SKIP_LOG2)
            def _():
                for slot, qi, first_kb in pending:
                    def more(c):
                        kb, min_carry = c
                        return (kb >= 0) & (min_carry < SB_SKIP_LOG2)

                    def k_step(c, slot=slot, qi=qi):
                        kb, _ = c
                        sweep([(slot, qi, [(kb, False)], False)])
                        return kb - 1, jnp.min(car_ref[slot])

                    lax.while_loop(more, k_step, (first_kb, jnp.min(car_ref[slot])))
        for slot, qi, _ in jobs:
            o_ref[rows_of(qi), :] = acc_ref[slot].astype(BF16)

    def first_steps(qi):
        return [(qi, True)] + ([(qi - 1, False)] if qi > 0 else [])

    sweep([(s, s, first_steps(s), True) for s in range(SB_GROUP)])
    finish([(s, s, s - 2 if s >= 2 else None) for s in range(SB_GROUP)])

    def q_group(g, carry):
        q0 = g * SB_GROUP
        sweep([(s, q0 + s, [(q0 + s, True), (q0 + s - 1, False)], True) for s in range(SB_GROUP)])
        finish([(s, q0 + s, q0 + s - 2) for s in range(SB_GROUP)])
        return carry

    lax.fori_loop(1, nq // SB_GROUP, q_group, 0)


def _sb_call(proj, seq):
    t = proj.shape[0]
    assert seq % (SB_GROUP * SB_BLK) == 0
    n_pairs = SB_WIDTH // LANES
    return pl.pallas_call(
        _sb_kernel,
        grid=(t // seq, n_pairs),
        in_specs=[
            pl.BlockSpec((seq, LANES), lambda b, p: (b, p)),
            pl.BlockSpec((seq, LANES), lambda b, p: (b, n_pairs + p)),
            pl.BlockSpec((seq, LANES), lambda b, p: (b, 2 * n_pairs + p)),
        ],
        out_specs=pl.BlockSpec((seq, LANES), lambda b, p: (b, p)),
        out_shape=jax.ShapeDtypeStruct((t, SB_WIDTH), BF16),
        scratch_shapes=[
            pltpu.VMEM((SB_BLK, SB_BLK), BF16),
            pltpu.VMEM((SB_GROUP, SB_BLK, LANES), F32),
            pltpu.VMEM((SB_GROUP, 2, SB_BLK, LANES), F32),
        ],
        compiler_params=pltpu.CompilerParams(
            dimension_semantics=("arbitrary", "arbitrary"), vmem_limit_bytes=VMEM_LIMIT),
        name="stickbreak",
    )(proj, proj, proj)


def _ret_tables():
    h = np.arange(RET_HEADS, dtype=np.float64)
    log_gamma = np.log(1.0 - 2.0 ** (-5.0 - h))
    idx = np.arange(RET_BLK, dtype=np.float64)
    t, s = idx[:, None], idx[None, :]
    same = (t // CHUNK) == (s // CHUNK)
    earlier = (s // CHUNK) < (t // CHUNK)
    expo = np.where(same, np.abs(t - s), np.where(earlier, t - s, 0.0))
    dmat = np.exp(log_gamma[:, None, None] * expo) * (same | earlier)
    qdec = np.exp(log_gamma[:, None] * (idx + 1.0)[None, :])
    kdec = np.exp(log_gamma[:, None] * (RET_BLK - 1.0 - idx)[None, :])
    cdec = np.exp(log_gamma * RET_BLK)
    n_pairs = RET_HEADS // 2
    lane_head = (np.arange(LANES) % RET_QK_DIM) // (RET_QK_DIM // 2)
    col_head = np.arange(2 * RET_V_DIM) // RET_V_DIM
    pair_heads = np.arange(RET_HEADS).reshape(n_pairs, 2)
    qdec_pair = np.stack([qdec[pair_heads[p][lane_head]].T for p in range(n_pairs)])
    kdec_pair = np.stack([kdec[pair_heads[p][lane_head]].T for p in range(n_pairs)])
    cdec_pair = np.stack([cdec[pair_heads[p][col_head]][None, :] for p in range(n_pairs)])
    return (jnp.asarray(dmat.reshape((n_pairs, 2, RET_BLK, RET_BLK)), F32),
            jnp.asarray(qdec_pair, F32), jnp.asarray(kdec_pair, F32), jnp.asarray(cdec_pair, F32))


def _ret_kernel(q_ref, k_ref, v_ref, g_ref, gn_ref, dm_ref, qd_ref, kd_ref, cd_ref, o_ref, st_ref):
    seq = q_ref.shape[0]
    lane = lax.broadcasted_iota(jnp.int32, (1, LANES), 1)
    head0 = (lane % RET_QK_DIM) < (RET_QK_DIM // 2)
    head_masks = (head0, jnp.logical_not(head0))
    st_row = lax.broadcasted_iota(jnp.int32, st_ref.shape, 0)
    st_col = lax.broadcasted_iota(jnp.int32, st_ref.shape, 1)
    own_head = ((st_row % RET_QK_DIM) // (RET_QK_DIM // 2)) == (st_col // RET_V_DIM)
    heads = range(2)
    nt = (((1,), (1,)), ((), ()))
    tn = (((0,), (0,)), ((), ()))
    st_ref[...] = jnp.zeros_like(st_ref)

    def step(n, carry):
        subs = range(RET_UNROLL)
        rows = [pl.ds(pl.multiple_of((n * RET_UNROLL + u) * RET_BLK, RET_BLK), RET_BLK) for u in subs]
        col = [slice(h * RET_V_DIM, (h + 1) * RET_V_DIM) for h in heads]
        q_blk = [q_ref[r, :] for r in rows]
        k_blk = [k_ref[r, :] for r in rows]
        qm = [[jnp.where(head_masks[h], q_blk[u], jnp.zeros_like(q_blk[u])) for h in heads] for u in subs]
        v = [v_ref[r, :] for r in rows]
        scores = [[lax.dot_general(qm[u][h], k_blk[u], nt, preferred_element_type=F32)
                   for h in heads] for u in subs]
        kd = [(k_blk[u].astype(F32) * kd_ref[0]).astype(BF16) for u in subs]
        kv = [jnp.where(own_head, lax.dot_general(kd[u], v[u], tn, preferred_element_type=F32), 0.0)
              for u in subs]
        qd = [(q_blk[u].astype(F32) * qd_ref[0]).astype(BF16) for u in subs]
        p = [[(scores[u][h] * dm_ref[0, h]).astype(BF16) for h in heads] for u in subs]
        intra = [[jnp.dot(p[u][h], v[u][:, col[h]], preferred_element_type=F32) for h in heads]
                 for u in subs]
        state = st_ref[...]
        for u in subs:
            cross = jnp.dot(qd[u], state.astype(BF16), preferred_element_type=F32)
            state = state * cd_ref[0] + kv[u]
            for h in heads:
                y = intra[u][h] + cross[:, col[h]]
                ms = jnp.mean(y * y, axis=-1, keepdims=True)
                yn = (y * lax.rsqrt(ms + EPS)) * gn_ref[:, col[h]]
                o_ref[rows[u], col[h]] = (g_ref[rows[u], col[h]].astype(F32) * yn).astype(BF16)
        st_ref[...] = state
        return carry

    lax.fori_loop(0, seq // (RET_BLK * RET_UNROLL), step, 0)


def _ret_call(rqk, plain, swish, ret_norm_g, seq):
    t = rqk.shape[0]
    n_pairs = RET_HEADS // 2
    dmat, qdec, kdec, cdec = _ret_tables()
    k0 = RET_QK_WIDTH // LANES
    v0 = (3 * SB_WIDTH) // (2 * RET_V_DIM)
    return pl.pallas_call(
        _ret_kernel,
        grid=(t // seq, n_pairs),
        in_specs=[
            pl.BlockSpec((seq, LANES), lambda b, p: (b, p)),
            pl.BlockSpec((seq, LANES), lambda b, p: (b, k0 + p)),
            pl.BlockSpec((seq, 2 * RET_V_DIM), lambda b, p: (b, v0 + p)),
            pl.BlockSpec((seq, 2 * RET_V_DIM), lambda b, p: (b, p)),
            pl.BlockSpec((1, 2 * RET_V_DIM), lambda b, p: (0, p)),
            pl.BlockSpec((1, 2, RET_BLK, RET_BLK), lambda b, p: (p, 0, 0, 0)),
            pl.BlockSpec((1, RET_BLK, LANES), lambda b, p: (p, 0, 0)),
            pl.BlockSpec((1, RET_BLK, LANES), lambda b, p: (p, 0, 0)),
            pl.BlockSpec((1, 1, 2 * RET_V_DIM), lambda b, p: (p, 0, 0)),
        ],
        out_specs=pl.BlockSpec((seq, 2 * RET_V_DIM), lambda b, p: (b, p)),
        out_shape=jax.ShapeDtypeStruct((t, RET_V_WIDTH), BF16),
        scratch_shapes=[pltpu.VMEM((LANES, 2 * RET_V_DIM), F32)],
        compiler_params=pltpu.CompilerParams(
            dimension_semantics=("arbitrary", "arbitrary"), vmem_limit_bytes=VMEM_LIMIT),
        name="retention",
    )(rqk, rqk, plain, swish, ret_norm_g, dmat, qdec, kdec, cdec)


def _mix_kernel(ysb_ref, yret_ref, gs0_ref, gs1_ref, gr0_ref, gr1_ref, x_ref,
                wsb_ref, wret_ref, wout_ref, gffn_ref, wrh_ref, wrl_ref, br_ref, tri_ref,
                hext_ref, cls_ref, cnt_ref):
    tm = x_ref.shape[0]

    @pl.when(pl.program_id(0) == 0)
    def _():
        cnt_ref[...] = jnp.zeros_like(cnt_ref)

    a = jnp.dot(ysb_ref[...], wsb_ref[...], preferred_element_type=F32)
    b = jnp.dot(yret_ref[...], wret_ref[...], preferred_element_type=F32)
    g_sb = jnp.concatenate([gs0_ref[...], gs1_ref[...]], axis=1).astype(F32)
    g_ret = jnp.concatenate([gr0_ref[...], gr1_ref[...]], axis=1).astype(F32)
    mixed = (g_sb * a + g_ret * b).astype(BF16)
    h = x_ref[...] + jnp.dot(mixed, wout_ref[...], preferred_element_type=F32)
    hext_ref[:, 0:D_MODEL] = h

    ms = jnp.mean(h * h, axis=-1, keepdims=True)
    hn = (h * lax.rsqrt(ms + EPS)) * gffn_ref[...]
    hn_hi = hn.astype(BF16)
    hn_lo = (hn - hn_hi.astype(F32)).astype(BF16)
    nt = (((1,), (1,)), ((), ()))
    w_both = jnp.concatenate([wrh_ref[...], wrl_ref[...]], axis=0)
    by_hi = lax.dot_general(w_both, hn_hi, nt, preferred_element_type=F32)
    logits = (by_hi[0:ROUTER_ROWS] + by_hi[ROUTER_ROWS:2 * ROUTER_ROWS]
              + lax.dot_general(wrh_ref[...], hn_lo, nt, preferred_element_type=F32)
              + br_ref[...])

    def first_argmax(vals):
        m = functools.reduce(jnp.maximum, vals)
        idx = jnp.full(m.shape, len(vals) - 1, jnp.int32)
        for i in range(len(vals) - 2, -1, -1):
            idx = jnp.where(vals[i] >= m, i, idx)
        return m, idx

    gl = [logits[r:r + 1, :] for r in range(N_GROUPS)]
    gmax, gsel = first_argmax(gl)
    p_group = 1.0 / functools.reduce(lambda s, v: s + v, [jnp.exp(v - gmax) for v in gl])
    el = []
    for e in range(EXPERTS_PER_GROUP):
        v = logits[N_GROUPS + 3 * EXPERTS_PER_GROUP + e:N_GROUPS + 3 * EXPERTS_PER_GROUP + e + 1, :]
        for g in range(N_GROUPS - 2, -1, -1):
            r = N_GROUPS + g * EXPERTS_PER_GROUP + e
            v = jnp.where(gsel == g, logits[r:r + 1, :], v)
        el.append(v)
    m1, i1 = first_argmax(el)
    rest = [jnp.where(i1 == e, -jnp.inf, el[e]) for e in range(EXPERTS_PER_GROUP)]
    m2, i2 = first_argmax(rest)
    tt = jnp.exp(m2 - m1)
    gate1 = p_group * (1.0 / (1.0 + tt))
    gate2 = p_group * (tt / (1.0 + tt))
    lo_first = i1 < i2
    ea = jnp.minimum(i1, i2)
    eb = jnp.maximum(i1, i2)
    w_a = jnp.where(lo_first, gate1, gate2)
    w_b = jnp.where(lo_first, gate2, gate1)
    pid = jnp.where(ea == 0, eb - 1, jnp.where(ea == 1, eb + 1, PAIRS_PER_GROUP - 1))
    cls = gsel * PAIRS_PER_GROUP + pid
    info = jnp.concatenate([w_a, w_b, jnp.zeros((INFO_COLS - 2, tm), F32)], axis=0)
    hext_ref[:, D_MODEL:HEXT_COLS] = info.T

    class_row = lax.broadcasted_iota(jnp.int32, (ROUTER_ROWS, tm), 0)
    onehot = class_row == cls
    onehot_bf = onehot.astype(BF16)
    before = jnp.dot(onehot_bf, tri_ref[...], preferred_element_type=F32)
    cnt = cnt_ref[...]
    seen = before + jnp.concatenate([cnt] * (tm // LANES), axis=1)
    rank = jnp.sum(jnp.where(onehot, seen, 0.0), axis=0, keepdims=True)
    cnt_ref[...] = cnt + jnp.dot(onehot_bf, jnp.ones((tm, LANES), BF16), preferred_element_type=F32)
    cls_ref[...] = jnp.concatenate(
        [cls, rank.astype(jnp.int32), jnp.zeros((cls_ref.shape[0] - 2, tm), jnp.int32)], axis=0)


def _mix_call(ysb, yret, gates, x2, wsb, wret, wout, gffn, wr_hi, wr_lo, br):
    t = x2.shape[0]
    tm = MIX_TM
    gate0 = 0
    const = lambda i: (0, 0)
    idx = jnp.arange(tm, dtype=jnp.int32)
    tri = (idx[:, None] < idx[None, :]).astype(BF16)
    return pl.pallas_call(
        _mix_kernel,
        grid=(t // tm,),
        in_specs=[
            pl.BlockSpec((tm, SB_WIDTH), lambda i: (i, 0)),
            pl.BlockSpec((tm, RET_V_WIDTH), lambda i: (i, 0)),
            pl.BlockSpec((tm, GATE_BLK), lambda i: (i, gate0)),
            pl.BlockSpec((tm, GATE_BLK), lambda i: (i, gate0 + 1)),
            pl.BlockSpec((tm, GATE_BLK), lambda i: (i, gate0 + 2)),
            pl.BlockSpec((tm, GATE_BLK), lambda i: (i, gate0 + 3)),
            pl.BlockSpec((tm, D_MODEL), lambda i: (i, 0)),
            pl.BlockSpec((SB_WIDTH, D_MODEL), const),
            pl.BlockSpec((RET_V_WIDTH, D_MODEL), const),
            pl.BlockSpec((D_MODEL, D_MODEL), const),
            pl.BlockSpec((1, D_MODEL), const),
            pl.BlockSpec((ROUTER_ROWS, D_MODEL), const),
            pl.BlockSpec((ROUTER_ROWS, D_MODEL), const),
            pl.BlockSpec((ROUTER_ROWS, 1), const),
            pl.BlockSpec((tm, tm), const),
        ],
        out_specs=[
            pl.BlockSpec((tm, HEXT_COLS), lambda i: (i, 0)),
            pl.BlockSpec((8, tm), lambda i: (0, i)),
            pl.BlockSpec((ROUTER_ROWS, LANES), const),
        ],
        out_shape=[
            jax.ShapeDtypeStruct((t, HEXT_COLS), F32),
            jax.ShapeDtypeStruct((8, t), jnp.int32),
            jax.ShapeDtypeStruct((ROUTER_ROWS, LANES), F32),
        ],
        compiler_params=pltpu.CompilerParams(
            dimension_semantics=("arbitrary",), vmem_limit_bytes=VMEM_LIMIT),
        name="mix_router",
    )(ysb, yret, gates, gates, gates, gates, x2, wsb, wret, wout, gffn, wr_hi, wr_lo, br, tri)


def _moe_kernel(dest_ref, nvalid_ref, ea_ref, eb_ref,
                hext_ref, wga_ref, wua_ref, wda_ref, wgb_ref, wub_ref, wdb_ref,
                gffn_ref, gfin_ref, out_ref, hbuf, obuf, src, gsem, ssem):
    i = pl.program_id(0)
    n_tiles = pl.num_programs(0)
    tm = hbuf.shape[1]
    nv = nvalid_ref[i]
    nxt = jnp.minimum(i + 1, n_tiles - 1)
    nv_next = jnp.where(i + 1 < n_tiles, nvalid_ref[nxt], 0)
    has_next = nv_next > 0

    def gather_copy(tile, buf_slot, r):
        tok = src[tile * tm + r]
        return pltpu.make_async_copy(hext_ref.at[pl.ds(tok, 1), :],
                                     hbuf.at[buf_slot, pl.ds(r, 1), :], gsem.at[buf_slot])

    def scatter_copy(tile, buf_slot, r):
        tok = src[tile * tm + r]
        return pltpu.make_async_copy(obuf.at[buf_slot, pl.ds(r, 1), :],
                                     out_ref.at[pl.ds(tok, 1), :], ssem.at[buf_slot])

    def start_rows(copy_of_row, rows_valid):
        for r in range(tm):
            if rows_valid is None:
                copy_of_row(r).start(priority=r % 2)
            else:
                @pl.when(r < rows_valid)
                def _():
                    copy_of_row(r).start(priority=r % 2)

    def wait_rows(full_copy, row_copy, rows_valid):
        @pl.when(rows_valid == tm)
        def _():
            full_copy.wait()

        @pl.when(rows_valid < tm)
        def _():
            for r in range(tm):
                @pl.when(r < rows_valid)
                def _():
                    row_copy(r).wait()

    def wait_gather(buf_slot, rows_valid):
        wait_rows(pltpu.make_async_copy(hext_ref.at[pl.ds(0, tm), :], hbuf.at[buf_slot], gsem.at[buf_slot]),
                  lambda r: pltpu.make_async_copy(hext_ref.at[pl.ds(0, 1), :],
                                                  hbuf.at[buf_slot, pl.ds(r, 1), :], gsem.at[buf_slot]),
                  rows_valid)

    def wait_scatter(buf_slot, rows_valid):
        wait_rows(pltpu.make_async_copy(obuf.at[buf_slot], out_ref.at[pl.ds(0, tm), :], ssem.at[buf_slot]),
                  lambda r: pltpu.make_async_copy(obuf.at[buf_slot, pl.ds(r, 1), :],
                                                  out_ref.at[pl.ds(0, 1), :], ssem.at[buf_slot]),
                  rows_valid)

    @pl.when(i == 0)
    def _():
        def invert(c, carry):
            for u in range(MOE_INVERT_UNROLL):
                t = c * MOE_INVERT_UNROLL + u
                src[dest_ref[t]] = t
            return carry

        lax.fori_loop(0, dest_ref.shape[0] // MOE_INVERT_UNROLL, invert, 0)
        hbuf[...] = jnp.zeros_like(hbuf)
        start_rows(functools.partial(gather_copy, 0, 0), nv)

    def tile_body(slot):
        @pl.when(i >= 2)
        def _():
            wait_scatter(slot, nvalid_ref[jnp.maximum(i - 2, 0)])

        wait_gather(slot, nv)
        both_full = (nv == tm) & (nv_next == tm)
        pl.when(both_full)(functools.partial(tile_main, slot, None, None))
        pl.when(jnp.logical_not(both_full))(functools.partial(tile_main, slot, nv, nv_next))

        @pl.when(jnp.logical_not(has_next))
        def _():
            @pl.when(i >= 1)
            def _():
                wait_scatter(1 - slot, nvalid_ref[jnp.maximum(i - 1, 0)])

            wait_scatter(slot, nv)

    def tile_main(slot, rows_now, rows_next):
        start_rows(functools.partial(gather_copy, nxt, 1 - slot), rows_next)
        hrows = hbuf[slot]
        h = hrows[:, 0:D_MODEL]
        w_a = hrows[:, D_MODEL:D_MODEL + 1]
        w_b = hrows[:, D_MODEL + 1:D_MODEL + 2]
        ms = jnp.mean(h * h, axis=-1, keepdims=True)
        hn = ((h * lax.rsqrt(ms + EPS)) * gffn_ref[...]).astype(BF16)

        def expert(wg_ref, wu_ref, wd_ref):
            half = jnp.dot(hn, wg_ref[0], preferred_element_type=F32)
            up = jnp.dot(hn, wu_ref[0], preferred_element_type=F32)
            hidden = ((half * (jnp.tanh(half) + 1.0)) * up).astype(BF16)
            return jnp.dot(hidden, wd_ref[0], preferred_element_type=F32)

        y = w_a * expert(wga_ref, wua_ref, wda_ref) + w_b * expert(wgb_ref, wub_ref, wdb_ref)
        h2 = h + y
        ms2 = jnp.mean(h2 * h2, axis=-1, keepdims=True)
        obuf[slot] = (h2 * lax.rsqrt(ms2 + EPS)) * gfin_ref[...]
        start_rows(functools.partial(scatter_copy, i, slot), rows_now)

    for parity in range(2):
        pl.when((nv > 0) & (i % 2 == parity))(functools.partial(tile_body, parity))


def _moe_call(dest, nvalid, ea, eb, hext, wg, wu, wd, gffn, gfin):
    t = hext.shape[0]
    tm = MOE_TM
    n_tiles = nvalid.shape[0]
    assert t % MOE_INVERT_UNROLL == 0
    wa_map = lambda i, dest, nv, ea, eb: (ea[i], 0, 0)
    wb_map = lambda i, dest, nv, ea, eb: (eb[i], 0, 0)
    const = lambda i, dest, nv, ea, eb: (0, 0)
    return pl.pallas_call(
        _moe_kernel,
        grid_spec=pltpu.PrefetchScalarGridSpec(
            num_scalar_prefetch=4,
            grid=(n_tiles,),
            in_specs=[
                pl.BlockSpec(memory_space=pl.ANY),
                pl.BlockSpec((1, D_MODEL, D_FF), wa_map),
                pl.BlockSpec((1, D_MODEL, D_FF), wa_map),
                pl.BlockSpec((1, D_FF, D_MODEL), wa_map),
                pl.BlockSpec((1, D_MODEL, D_FF), wb_map),
                pl.BlockSpec((1, D_MODEL, D_FF), wb_map),
                pl.BlockSpec((1, D_FF, D_MODEL), wb_map),
                pl.BlockSpec((1, D_MODEL), const),
                pl.BlockSpec((1, D_MODEL), const),
            ],
            out_specs=pl.BlockSpec(memory_space=pl.ANY),
            scratch_shapes=[
                pltpu.VMEM((2, tm, HEXT_COLS), F32),
                pltpu.VMEM((2, tm, D_MODEL), F32),
                pltpu.SMEM((n_tiles * tm,), jnp.int32),
                pltpu.SemaphoreType.DMA((2,)),
                pltpu.SemaphoreType.DMA((2,)),
            ],
        ),
        out_shape=jax.ShapeDtypeStruct((t, D_MODEL), F32),
        compiler_params=pltpu.CompilerParams(
            dimension_semantics=("arbitrary",), vmem_limit_bytes=VMEM_LIMIT),
        name="experts",
    )(dest, nvalid, ea, eb, hext, wg, wu, wd, wg, wu, wd, gffn, gfin)


def _routing_plan(cls, rank, counts, tm):
    t = cls.shape[0]
    n_tiles = t // tm + N_CLASSES
    tiles_c = (counts + tm - 1) // tm
    tile_end = jnp.cumsum(tiles_c)
    tile_off = tile_end - tiles_c
    onehot = cls[:, None] == jnp.arange(N_CLASSES, dtype=jnp.int32)[None, :]
    dest = jnp.sum(jnp.where(onehot, tile_off[None, :], 0), axis=1) * tm + rank
    tile = jnp.arange(n_tiles, dtype=jnp.int32)
    used = tile < tile_end[-1]
    tile_cls = jnp.minimum(jnp.sum((tile[:, None] >= tile_end[None, :]).astype(jnp.int32), axis=1),
                           N_CLASSES - 1)
    cls_onehot = tile_cls[:, None] == jnp.arange(N_CLASSES, dtype=jnp.int32)[None, :]
    pick = lambda table: jnp.sum(jnp.where(cls_onehot, table[None, :], 0), axis=1)
    nvalid = jnp.where(used, jnp.clip(pick(counts) - (tile - pick(tile_off)) * tm, 0, tm), 0)
    classes = np.arange(N_CLASSES)
    ea_tab = jnp.asarray((classes // PAIRS_PER_GROUP) * EXPERTS_PER_GROUP
                         + np.asarray(_PAIR_A)[classes % PAIRS_PER_GROUP], jnp.int32)
    eb_tab = jnp.asarray((classes // PAIRS_PER_GROUP) * EXPERTS_PER_GROUP
                         + np.asarray(_PAIR_B)[classes % PAIRS_PER_GROUP], jnp.int32)
    last_used = jnp.sum(jnp.where(tile == tile_end[-1] - 1, tile_cls, 0))
    tile_cls = jnp.where(used, tile_cls, last_used)
    cls_onehot = tile_cls[:, None] == jnp.arange(N_CLASSES, dtype=jnp.int32)[None, :]
    return dest.astype(jnp.int32), nvalid.astype(jnp.int32), pick(ea_tab), pick(eb_tab)


def _rope_tables(seq):
    half = RET_QK_DIM // 2
    inv_freq = ROPE_BASE ** (-jnp.arange(half, dtype=F32) / half)
    ang = jnp.arange(seq, dtype=F32)[:, None] * inv_freq[None, :]
    cos, sin = jnp.cos(ang), jnp.sin(ang)
    cos_t = jnp.tile(cos, (1, LANES // half))
    sin_t = jnp.concatenate([-sin, -sin, sin, sin], axis=1)
    return cos_t, sin_t


def kernel(x, norm_mix_g, w_in, w_gate, b_gate, w_sb_out, w_ret_out, ret_norm_g, w_out,
           norm_ffn_g, w_group_router, b_group_router, w_expert_router, b_expert_router,
           w_exp_gate, w_exp_up, w_exp_down, norm_final_g):
    bsz, seq, d = x.shape
    t = bsz * seq
    assert d == D_MODEL and w_in.shape[0] == 1 and t % MIX_TM == 0
    assert seq % (SB_GROUP * SB_BLK) == 0 and seq % (RET_BLK * RET_UNROLL) == 0
    x2 = x.reshape(t, d)
    wi = w_in[0]
    c_rq = 3 * SB_WIDTH
    c_rk = c_rq + RET_QK_WIDTH
    c_rv = c_rk + RET_QK_WIDTH
    c_rg = c_rv + RET_V_WIDTH
    ones = functools.partial(jnp.ones, dtype=F32)
    w_plain = jnp.concatenate([wi[:, :c_rq], wi[:, c_rv:c_rg]], axis=1).astype(BF16)
    s_plain = jnp.concatenate([jnp.full((SB_WIDTH,), SB_HEAD_DIM ** -0.5 * LOG2E, F32),
                               ones((2 * SB_WIDTH + RET_V_WIDTH,))])[None, :]
    perm = _rot_perm()
    w_rot = jnp.concatenate([wi[:, c_rq:c_rk][:, perm],
                             (RET_QK_DIM ** -0.5) * wi[:, c_rk:c_rv][:, perm]], axis=1).astype(BF16)
    w_swish_half = (0.5 * wi[:, c_rg:]).astype(BF16)
    w_gate_half = (0.5 * w_gate[0]).astype(BF16)
    b_gate_half = (0.5 * b_gate[0])[None, :]
    cos_t, sin_t = _rope_tables(seq)

    plain, xn = _proj_plain_call(x2, norm_mix_g[0][None, :], w_plain, s_plain, seq)
    rqk = _proj_rot_call(xn, w_rot, cos_t, sin_t, seq)
    swish = _proj_swish_call(xn, w_swish_half, seq)
    gates = _proj_gate_call(xn, w_gate_half, b_gate_half, seq)
    ysb = _sb_call(plain, seq)
    yret = _ret_call(rqk, plain, swish, ret_norm_g[0][None, :], seq)

    wr = jnp.concatenate([w_group_router[0], w_expert_router[0]], axis=1).T
    wr = jnp.pad(wr, ((0, ROUTER_ROWS - wr.shape[0]), (0, 0)))
    wr_hi = wr.astype(BF16)
    wr_lo = (wr - wr_hi.astype(F32)).astype(BF16)
    br = jnp.concatenate([b_group_router[0], b_expert_router[0]])
    br = jnp.pad(br, (0, ROUTER_ROWS - br.shape[0]))[:, None]
    hext, cls8, cnt = _mix_call(ysb, yret, gates, x2, w_sb_out[0].astype(BF16), w_ret_out[0].astype(BF16),
                           w_out[0].astype(BF16), norm_ffn_g[0][None, :], wr_hi, wr_lo, br)

    counts = cnt[:N_CLASSES, 0].astype(jnp.int32)
    dest, nvalid, ea, eb = _routing_plan(cls8[0], cls8[1], counts, MOE_TM)
    out = _moe_call(dest, nvalid, ea, eb, hext, (0.5 * w_exp_gate[0]).astype(BF16), w_exp_up[0].astype(BF16),
                    w_exp_down[0].astype(BF16), norm_ffn_g[0][None, :], norm_final_g[None, :])
    return out.reshape(bsz, seq, d)
```
